```python
import math
import jax, jax.numpy as jnp
from jax import lax
import numpy as np

D_MODEL = 1024
BATCH = 8
SEQ = 4096
DEPTH = 2

MEM_LEN = 256
N_BRANCH = 5
BRANCH_W = D_MODEL // 2
CONV_K = 4
EPS = 1e-6
GDN_HEADS = 4
GDN_DK = BRANCH_W // GDN_HEADS
GDN_DV = BRANCH_W // GDN_HEADS
GDN_CHUNK = 64
GDN_QKV = 2 * GDN_HEADS * GDN_DK + GDN_HEADS * GDN_DV
SG_CHUNK = 128
SG_GROUPS = 4
SG_GW = BRANCH_W // SG_GROUPS
DSA_HEADS = 8
DSA_HD = BRANCH_W // DSA_HEADS
IDX_HEADS = 8
IDX_HD = 64
DSA_TOPK_MAX = 256
Q_BLOCK = 128
SSD_HEADS = 8
SSD_HD = BRANCH_W // SSD_HEADS
SSD_GROUPS = 2
SSD_STATE = 128
SSD_CHUNK = 128
SSD_XBC = SSD_HEADS * SSD_HD + 2 * SSD_GROUPS * SSD_STATE
MEM_HEADS = 4
MEM_HD = BRANCH_W // MEM_HEADS

IN_SPLITS = (
    GDN_QKV, GDN_HEADS, GDN_HEADS, BRANCH_W,
    BRANCH_W, BRANCH_W, BRANCH_W,
    DSA_HEADS * DSA_HD, DSA_HD, DSA_HD, IDX_HEADS * IDX_HD, IDX_HD, IDX_HEADS, BRANCH_W,
    BRANCH_W, SSD_XBC, SSD_HEADS,
    MEM_HEADS * MEM_HD, BRANCH_W,
)
IN_COLS = sum(IN_SPLITS)

kernel_name = 'hybrid_gated_parallel_mixer'


def rms_norm(x, g):
    xf = x.astype(jnp.float32)
    y = xf * lax.rsqrt(jnp.mean(xf * xf, axis=-1, keepdims=True) + EPS)
    return (y * g.astype(jnp.float32)).astype(x.dtype)


def l2_norm(x):
    xf = x.astype(jnp.float32)
    return (xf * lax.rsqrt(jnp.sum(xf * xf, axis=-1, keepdims=True) + EPS)).astype(x.dtype)


def layer_norm(x, g, b):
    xf = x.astype(jnp.float32)
    mu = jnp.mean(xf, axis=-1, keepdims=True)
    var = jnp.mean(jnp.square(xf - mu), axis=-1, keepdims=True)
    return ((xf - mu) * lax.rsqrt(var + 1e-5) * g.astype(jnp.float32) + b.astype(jnp.float32)).astype(x.dtype)


def causal_depthwise_conv(x, w, b=None):
    K = w.shape[0]
    L = x.shape[1]
    xp = jnp.pad(x, ((0, 0), (K - 1, 0), (0, 0)))
    y = xp[:, 0:L] * w[0]
    for j in range(1, K):
        y = y + xp[:, j:j + L] * w[j]
    return y if b is None else y + b


def alibi_slopes(n):
    return 2.0 ** (-8.0 * jnp.arange(1, n + 1, dtype=jnp.float32) / n)


def gated_delta_rule(q, k, v, log_decay, beta):
    f32 = jnp.float32
    Bsz, L, H, DK = q.shape
    DV = v.shape[-1]
    C = GDN_CHUNK
    N = L // C

    def chunks(t):
        return jnp.swapaxes(t.astype(f32).reshape(Bsz, N, C, H, *t.shape[3:]), 2, 3)

    q, k, v, gl, beta = map(chunks, (q, k, v, log_decay, beta))
    gc = jnp.cumsum(gl, axis=-1)
    incl = jnp.tril(jnp.ones((C, C), bool))
    strict = jnp.tril(jnp.ones((C, C), bool), -1)
    diff = gc[..., :, None] - gc[..., None, :]
    decay = jnp.where(incl, jnp.exp(jnp.where(incl, diff, 0.0)), 0.0)
    kk = jnp.einsum('bnhcd,bnhsd->bnhcs', k, k)
    a_mat = jnp.eye(C, dtype=f32) + jnp.where(strict, beta[..., :, None] * kk * decay, 0.0)
    rhs = jnp.concatenate([(beta * jnp.exp(gc))[..., None] * k, beta[..., None] * v], axis=-1)
    sol = lax.linalg.triangular_solve(a_mat, rhs, left_side=True, lower=True, unit_diagonal=True)
    w_k, u0 = sol[..., :DK], sol[..., DK:]
    qk = jnp.einsum('bnhcd,bnhsd->bnhcs', q, k) * decay
    q_dec = q * jnp.exp(gc)[..., None]
    k_dec = k * jnp.exp(gc[..., -1:] - gc)[..., None]
    chunk_decay = jnp.exp(gc[..., -1])

    def step(S, inp):
        qd, kd, qkc, wk, u0c, cd = inp
        u = u0c - jnp.einsum('bhcd,bhvd->bhcv', wk, S)
        o = jnp.einsum('bhcd,bhvd->bhcv', qd, S) + jnp.einsum('bhcs,bhsv->bhcv', qkc, u)
        S = cd[..., None, None] * S + jnp.einsum('bhsv,bhsd->bhvd', u, kd)
        return S, o

    xs = tuple(jnp.moveaxis(t, 1, 0) for t in (q_dec, k_dec, qk, w_k, u0, chunk_decay))
    _, o = lax.scan(step, jnp.zeros((Bsz, H, DV, DK), f32), xs)
    return jnp.transpose(o, (1, 0, 3, 2, 4)).reshape(Bsz, L, H, DV)


def gdn_branch(qkv, a_in, b_in, gate, conv_w, a_log, dt_bias, norm_g):
    Bsz, L, _ = qkv.shape
    qkv = jax.nn.silu(causal_depthwise_conv(qkv, conv_w))
    q, k, v = jnp.split(qkv, [GDN_HEADS * GDN_DK, 2 * GDN_HEADS * GDN_DK], axis=-1)
    q = l2_norm(q.reshape(Bsz, L, GDN_HEADS, GDN_DK)) * GDN_DK ** -0.5
    k = l2_norm(k.reshape(Bsz, L, GDN_HEADS, GDN_DK))
    v = v.reshape(Bsz, L, GDN_HEADS, GDN_DV)
    log_decay = -jnp.exp(a_log.astype(jnp.float32)) * jax.nn.softplus(a_in.astype(jnp.float32) + dt_bias)
    beta = jax.nn.sigmoid(b_in.astype(jnp.float32))
    o = gated_delta_rule(q, k, v, log_decay, beta).astype(qkv.dtype)
    o = rms_norm(o, norm_g).reshape(Bsz, L, BRANCH_W)
    return o * jax.nn.silu(gate)


def spatial_gating_branch(u, v, gate, ln_g, ln_b, w_s, b_s):
    Bsz, L, _ = u.shape
    N = L // SG_CHUNK
    u = jax.nn.gelu(u)
    v = layer_norm(jax.nn.gelu(v), ln_g, ln_b).reshape(Bsz, N, SG_CHUNK, SG_GROUPS, SG_GW)
    w_causal = jnp.where(jnp.tril(jnp.ones((SG_CHUNK, SG_CHUNK), bool)), w_s, 0.0)
    mixed = jnp.einsum('gts,bnsgc->bntgc', w_causal, v) + jnp.swapaxes(b_s, 0, 1)[None, None, :, :, None]
    return u * mixed.reshape(Bsz, L, BRANCH_W) * jax.nn.silu(gate)


def dsa_branch(q, k, v, iq, ik, iw, gate, q_norm_g, k_norm_g):
    f32 = jnp.float32
    Bsz, L, _ = q.shape
    top_k = min(DSA_TOPK_MAX, L // 4)
    n_blocks = L // Q_BLOCK
    q = rms_norm(q.reshape(Bsz, L, DSA_HEADS, DSA_HD), q_norm_g)
    k = rms_norm(k, k_norm_g)
    iq = iq.reshape(Bsz, L, IDX_HEADS, IDX_HD)
    iw = iw * (IDX_HEADS ** -0.5 * IDX_HD ** -0.5)
    slopes = alibi_slopes(DSA_HEADS)
    batch_ix = jnp.arange(Bsz)[:, None, None]
    pos_k = jnp.arange(L)

    def block(t0):
        qb = lax.dynamic_slice_in_dim(q, t0, Q_BLOCK, axis=1)
        iqb = lax.dynamic_slice_in_dim(iq, t0, Q_BLOCK, axis=1)
        iwb = lax.dynamic_slice_in_dim(iw, t0, Q_BLOCK, axis=1)
        pos_q = t0 + jnp.arange(Q_BLOCK)
        s_idx = jnp.einsum('bths,bth->bts', jax.nn.relu(jnp.einsum('bthd,bsd->bths', iqb, ik)), iwb).astype(f32)
        s_idx = jnp.where(pos_k[None, :] <= pos_q[:, None], s_idx, -jnp.inf)
        _, sel = lax.top_k(s_idx, top_k)
        valid = sel <= pos_q[None, :, None]
        k_sel = k[batch_ix, sel]
        v_sel = v[batch_ix, sel]
        logits = jnp.einsum('bthd,btkd->bhtk', qb, k_sel).astype(f32) * DSA_HD ** -0.5
        dist = (pos_q[:, None] - sel).astype(f32)
        logits = logits - slopes[None, :, None, None] * dist[:, None]
        logits = jnp.where(valid[:, None], logits, -jnp.inf)
        p = jax.nn.softmax(logits, axis=-1).astype(v.dtype)
        return jnp.einsum('bhtk,btkd->bthd', p, v_sel)

    out = lax.map(block, jnp.arange(n_blocks) * Q_BLOCK)
    out = jnp.swapaxes(out, 0, 1).reshape(Bsz, L, BRANCH_W)
    return out * jax.nn.silu(gate)


def ssd_branch(z, xbc, dt_in, conv_w, conv_b, a_log, dt_bias, d_skip, norm_g):
    f32 = jnp.float32
    Bsz, L, _ = z.shape
    C = SSD_CHUNK
    N = L // C
    G, R, P, DS = SSD_GROUPS, SSD_HEADS // SSD_GROUPS, SSD_HD, SSD_STATE
    xbc = jax.nn.silu(causal_depthwise_conv(xbc, conv_w, conv_b)).astype(f32)
    xs, bm, cm = jnp.split(xbc, [SSD_HEADS * P, SSD_HEADS * P + G * DS], axis=-1)
    xs = xs.reshape(Bsz, N, C, G, R, P)
    bm = bm.reshape(Bsz, N, C, G, DS)
    cm = cm.reshape(Bsz, N, C, G, DS)
    dt = jax.nn.softplus(dt_in.astype(f32) + dt_bias)
    a = -jnp.exp(a_log.astype(f32))
    log_a = jnp.transpose((dt * a).reshape(Bsz, N, C, G, R), (0, 1, 3, 4, 2))
    cs = jnp.cumsum(log_a, axis=-1)
    xdt = xs * dt.reshape(Bsz, N, C, G, R)[..., None]
    incl = jnp.tril(jnp.ones((C, C), bool))
    seg = cs[..., :, None] - cs[..., None, :]
    lmat = jnp.where(incl, jnp.exp(jnp.where(incl, seg, 0.0)), 0.0)
    cb = jnp.einsum('bncgd,bnsgd->bngcs', cm, bm)
    y_diag = jnp.einsum('bngcs,bngrcs,bnsgrp->bncgrp', cb, lmat, xdt)
    decay_states = jnp.exp(cs[..., -1:] - cs)
    states = jnp.einsum('bnsgd,bngrs,bnsgrp->bngrpd', bm, decay_states, xdt)
    chunk_decay = jnp.exp(cs[..., -1])

    def step(S, inp):
        st, cd = inp
        return cd[..., None, None] * S + st, S

    _, s_prev = lax.scan(step, jnp.zeros((Bsz, G, R, P, DS), f32),
                         (jnp.moveaxis(states, 1, 0), jnp.moveaxis(chunk_decay, 1, 0)))
    s_prev = jnp.moveaxis(s_prev, 0, 1)
    y_off = jnp.einsum('bncgd,bngrpd,bngrc->bncgrp', cm, s_prev, jnp.exp(cs))
    y = y_diag + y_off + xs * d_skip.reshape(G, R)[:, :, None]
    y = y.reshape(Bsz, L, BRANCH_W).astype(z.dtype)
    return rms_norm(y * jax.nn.silu(z), norm_g)


def memory_branch(q, gate, mem, mem_norm_g, w_mem_kv, q_norm_g, k_norm_g):
    Bsz, L, _ = q.shape
    M = mem.shape[1]
    q = rms_norm(q.reshape(Bsz, L, MEM_HEADS, MEM_HD), q_norm_g)
    k, v = jnp.split(rms_norm(mem, mem_norm_g) @ w_mem_kv, 2, axis=-1)
    k = rms_norm(k.reshape(Bsz, M, MEM_HEADS, MEM_HD), k_norm_g)
    v = v.reshape(Bsz, M, MEM_HEADS, MEM_HD)
    logits = jnp.einsum('blhd,bmhd->bhlm', q, k).astype(jnp.float32) * MEM_HD ** -0.5
    p = jax.nn.softmax(logits, axis=-1).astype(v.dtype)
    o = jnp.einsum('bhlm,bmhd->blhd', p, v).reshape(Bsz, L, BRANCH_W)
    return o * jax.nn.silu(gate)


def _normal(key, shape, scale):
    return scale * jax.random.normal(key, shape, jnp.float32)


def _gain(key, shape, scale=0.02):
    return 1.0 + scale * jax.random.normal(key, shape, jnp.float32)


def _dt_bias(key, shape):
    dt = jnp.exp(jax.random.uniform(key, shape, jnp.float32, math.log(1e-3), math.log(1e-1)))
    return dt + jnp.log(-jnp.expm1(-dt))


def _a_log(key, shape):
    return jnp.log(jax.random.uniform(key, shape, jnp.float32, 1.0, 16.0))


def setup_inputs(seed: int = 0) -> dict:
    key = jax.random.key(seed)
    ks = jax.random.split(key, 27)
    return {
        'x': _normal(ks[0], (BATCH, SEQ, D_MODEL), 1.0),
        'mem': _normal(ks[1], (BATCH, MEM_LEN, D_MODEL), 1.0),
        'norm_g': _gain(ks[2], (DEPTH, D_MODEL)),
        'w_in': _normal(ks[3], (DEPTH, D_MODEL, IN_COLS), D_MODEL ** -0.5),
        'gdn_conv_w': _normal(ks[4], (DEPTH, CONV_K, GDN_QKV), CONV_K ** -0.5),
        'gdn_a_log': _a_log(ks[5], (DEPTH, GDN_HEADS)),
        'gdn_dt_bias': _dt_bias(ks[6], (DEPTH, GDN_HEADS)),
        'gdn_norm_g': _gain(ks[7], (DEPTH, GDN_DV)),
        'sg_ln_g': _gain(ks[8], (DEPTH, BRANCH_W)),
        'sg_ln_b': _normal(ks[9], (DEPTH, BRANCH_W), 0.02),
        'sg_w': _normal(ks[10], (DEPTH, SG_GROUPS, SG_CHUNK, SG_CHUNK), SG_CHUNK ** -0.5),
        'sg_b': _gain(ks[11], (DEPTH, SG_GROUPS, SG_CHUNK)),
        'dsa_q_norm_g': _gain(ks[12], (DEPTH, DSA_HD)),
        'dsa_k_norm_g': _gain(ks[13], (DEPTH, DSA_HD)),
        'ssd_conv_w': _normal(ks[14], (DEPTH, CONV_K, SSD_XBC), CONV_K ** -0.5),
        'ssd_conv_b': _normal(ks[15], (DEPTH, SSD_XBC), 0.02),
        'ssd_a_log': _a_log(ks[16], (DEPTH, SSD_HEADS)),
        'ssd_dt_bias': _dt_bias(ks[17], (DEPTH, SSD_HEADS)),
        'ssd_d': _gain(ks[18], (DEPTH, SSD_HEADS), 0.1),
        'ssd_norm_g': _gain(ks[19], (DEPTH, BRANCH_W)),
        'mem_norm_g': _gain(ks[20], (DEPTH, D_MODEL)),
        'w_mem_kv': _normal(ks[21], (DEPTH, D_MODEL, 2 * BRANCH_W), D_MODEL ** -0.5),
        'mem_q_norm_g': _gain(ks[22], (DEPTH, MEM_HD)),
        'mem_k_norm_g': _gain(ks[23], (DEPTH, MEM_HD)),
        'w_gate': _normal(ks[24], (DEPTH, N_BRANCH, D_MODEL, D_MODEL), D_MODEL ** -0.5),
        'w_branch': _normal(ks[25], (DEPTH, N_BRANCH, BRANCH_W, D_MODEL), BRANCH_W ** -0.5),
        'w_out': _normal(ks[26], (DEPTH, D_MODEL, D_MODEL), 0.5 * D_MODEL ** -0.5),
    }


def reference(x, mem, norm_g, w_in, gdn_conv_w, gdn_a_log, gdn_dt_bias, gdn_norm_g,
              sg_ln_g, sg_ln_b, sg_w, sg_b, dsa_q_norm_g, dsa_k_norm_g,
              ssd_conv_w, ssd_conv_b, ssd_a_log, ssd_dt_bias, ssd_d, ssd_norm_g,
              mem_norm_g, w_mem_kv, mem_q_norm_g, mem_k_norm_g, w_gate, w_branch, w_out):
    split_at = np.cumsum(IN_SPLITS)[:-1].tolist()
    for i in range(DEPTH):
        h = rms_norm(x, norm_g[i])
        (a_qkv, a_a, a_b, a_gate,
         b_u, b_v, b_gate,
         c_q, c_k, c_v, c_iq, c_ik, c_iw, c_gate,
         d_z, d_xbc, d_dt,
         m_q, m_gate) = jnp.split(h @ w_in[i], split_at, axis=-1)
        ys = (
            gdn_branch(a_qkv, a_a, a_b, a_gate, gdn_conv_w[i], gdn_a_log[i], gdn_dt_bias[i], gdn_norm_g[i]),
            spatial_gating_branch(b_u, b_v, b_gate, sg_ln_g[i], sg_ln_b[i], sg_w[i], sg_b[i]),
            dsa_branch(c_q, c_k, c_v, c_iq, c_ik, c_iw, c_gate, dsa_q_norm_g[i], dsa_k_norm_g[i]),
            ssd_branch(d_z, d_xbc, d_dt, ssd_conv_w[i], ssd_conv_b[i], ssd_a_log[i], ssd_dt_bias[i],
                       ssd_d[i], ssd_norm_g[i]),
            memory_branch(m_q, m_gate, mem, mem_norm_g[i], w_mem_kv[i], mem_q_norm_g[i], mem_k_norm_g[i]),
        )
        merged = jnp.zeros_like(x)
        for p in range(N_BRANCH):
            merged = merged + jax.nn.sigmoid(h @ w_gate[i, p]) * (ys[p] @ w_branch[i, p])
        x = x + merged @ w_out[i]
    return x
```

```python
import functools

import numpy as np
import jax
import jax.numpy as jnp
from jax import lax
from jax.experimental import pallas as pl
from jax.experimental.pallas import tpu as pltpu

F32 = jnp.float32
BF16 = jnp.bfloat16
I32 = jnp.int32
HIGHEST = lax.Precision.HIGHEST

D_MODEL = 1024
N_BRANCH = 5
BRANCH_W = D_MODEL // 2
CONV_K = 4
EPS = 1e-6
GDN_HEADS = 4
GDN_DK = BRANCH_W // GDN_HEADS
GDN_CHUNK = 64
GDN_QKV = 3 * BRANCH_W
SG_CHUNK = 128
SG_GROUPS = 4
DSA_HEADS = 8
DSA_HD = BRANCH_W // DSA_HEADS
IDX_HEADS = 8
IDX_HD = 64
DSA_TOPK_MAX = 256
Q_BLOCK = 128
SSD_HEADS = 8
SSD_HD = BRANCH_W // SSD_HEADS
SSD_GROUPS = 2
SSD_STATE = 128
SSD_CHUNK = 128
SSD_XBC = SSD_HEADS * SSD_HD + 2 * SSD_GROUPS * SSD_STATE
MEM_HEADS = 4
MEM_HD = BRANCH_W // MEM_HEADS

IN_SPLITS = (
    GDN_QKV, GDN_HEADS, GDN_HEADS, BRANCH_W,
    BRANCH_W, BRANCH_W, BRANCH_W,
    DSA_HEADS * DSA_HD, DSA_HD, DSA_HD, IDX_HEADS * IDX_HD, IDX_HD, IDX_HEADS, BRANCH_W,
    BRANCH_W, SSD_XBC, SSD_HEADS,
    MEM_HEADS * MEM_HD, BRANCH_W,
)
_SEG_NAMES = ("a_qkv", "a_a", "a_b", "a_gate", "b_u", "b_v", "b_gate", "c_q", "c_k", "c_v", "c_iq", "c_ik",
              "c_iw", "c_gate", "d_z", "d_xbc", "d_dt", "m_q", "m_gate")
_SEG_START = dict(zip(_SEG_NAMES, np.concatenate([[0], np.cumsum(IN_SPLITS)[:-1]]).tolist()))
_SEG_WIDTH = dict(zip(_SEG_NAMES, IN_SPLITS))

LANE = 128
HALF = LANE // 2
TILE = 128
VMEM_LIMIT = 56 * 1024 * 1024

_P_LAYOUT = (("a_qkv", 1536), ("b_u", 512), ("d_xbc", 1024), ("a_gate", 512), ("b_v", 512), ("b_gate", 512),
             ("c_q", 512), ("c_iq", 512), ("c_gate", 512), ("d_z", 512), ("m_q", 512), ("m_gate", 512),
             ("c_kk", 128), ("c_vv", 128), ("c_ikik", 128), ("small", 128))
P_OFF = {}
_o = 0
for _n, _w in _P_LAYOUT:
    P_OFF[_n] = _o
    _o += _w
P_COLS = _o
SM_A, SM_B, SM_DT, SM_IW = 0, 4, 8, 16


def _p_columns():
    cols = []
    for name, width in _P_LAYOUT:
        if name in _SEG_START:
            cols += list(range(_SEG_START[name], _SEG_START[name] + width))
        elif name in ("c_kk", "c_vv", "c_ikik"):
            src = {"c_kk": "c_k", "c_vv": "c_v", "c_ikik": "c_ik"}[name]
            one = list(range(_SEG_START[src], _SEG_START[src] + _SEG_WIDTH[src]))
            cols += one + one
        else:
            sm = [-1] * LANE
            for seg, at in (("a_a", SM_A), ("a_b", SM_B), ("d_dt", SM_DT), ("c_iw", SM_IW)):
                for j in range(_SEG_WIDTH[seg]):
                    sm[at + j] = _SEG_START[seg] + j
            cols += sm
    return np.asarray(cols, np.int32)


_P_COLUMNS = _p_columns()


def _dot(a, b, precision=None):
    return jnp.dot(a, b, preferred_element_type=F32, precision=precision)


def _dot_nt(a, b):
    return lax.dot_general(a, b, (((1,), (1,)), ((), ())), preferred_element_type=F32)


def _sigmoid(x):
    return 1.0 / (1.0 + jnp.exp(-x))


def _silu(x):
    return x * _sigmoid(x)


def _softplus(x):
    return jnp.maximum(x, 0.0) + jnp.log1p(jnp.exp(-jnp.abs(x)))


def _gelu_tanh(x):
    return 0.5 * x * (1.0 + jnp.tanh(np.sqrt(2.0 / np.pi).astype(np.float32) * (x + 0.044715 * (x * x * x))))


def _rms(x, g, eps=EPS):
    return x * lax.rsqrt(jnp.mean(x * x, axis=-1, keepdims=True) + eps) * g


def _full_spec(shape):
    nd = len(shape)
    return pl.BlockSpec(shape, lambda *_: (0,) * nd)


def _in_proj_kernel(x_ref, g_ref, w_ref, o_ref, h_ref):
    @pl.when(pl.program_id(1) == 0)
    def _():
        h_ref[...] = _rms(x_ref[...], g_ref[...]).astype(BF16)

    o_ref[...] = _dot(h_ref[...], w_ref[...])


def _in_proj(x2d, g, w_bf16):
    m = x2d.shape[0]
    tm = min(1024, m)
    tn = 512
    return pl.pallas_call(
        _in_proj_kernel,
        grid=(m // tm, P_COLS // tn),
        in_specs=[pl.BlockSpec((tm, D_MODEL), lambda i, j: (i, 0)),
                  pl.BlockSpec((1, D_MODEL), lambda i, j: (0, 0)),
                  pl.BlockSpec((D_MODEL, tn), lambda i, j: (0, j))],
        out_specs=pl.BlockSpec((tm, tn), lambda i, j: (i, j)),
        out_shape=jax.ShapeDtypeStruct((m, P_COLS), F32),
        scratch_shapes=[pltpu.VMEM((tm, D_MODEL), BF16)],
        compiler_params=pltpu.CompilerParams(dimension_semantics=("arbitrary", "arbitrary"),
                                             vmem_limit_bytes=VMEM_LIMIT),
        name="in_proj",
    )(x2d, g, w_bf16)


def _merge_kernel(x_ref, g_ref, y0, y1, y2, y3, y4, wg_ref, wb_ref, wo_ref, o_ref):
    x = x_ref[...]
    h = _rms(x, g_ref[...]).astype(BF16)
    acc = None
    for p, y in enumerate((y0, y1, y2, y3, y4)):
        term = _sigmoid(_dot(h, wg_ref[p])) * _dot(y[...], wb_ref[p])
        acc = term if acc is None else acc + term
    o_ref[...] = x + _dot(acc.astype(BF16), wo_ref[...])


def _merge(x2d, g, ys, wg, wb, wo):
    m = x2d.shape[0]
    tm = min(512, m)
    row = lambda i: (i, 0)
    const = pl.Buffered(1)
    return pl.pallas_call(
        _merge_kernel,
        grid=(m // tm,),
        in_specs=[pl.BlockSpec((tm, D_MODEL), row), pl.BlockSpec((1, D_MODEL), lambda i: (0, 0))]
        + [pl.BlockSpec((tm, BRANCH_W), row)] * N_BRANCH
        + [pl.BlockSpec((N_BRANCH, D_MODEL, D_MODEL), lambda i: (0, 0, 0), pipeline_mode=const),
           pl.BlockSpec((N_BRANCH, BRANCH_W, D_MODEL), lambda i: (0, 0, 0), pipeline_mode=const),
           pl.BlockSpec((D_MODEL, D_MODEL), lambda i: (0, 0), pipeline_mode=const)],
        out_specs=pl.BlockSpec((tm, D_MODEL), row),
        out_shape=jax.ShapeDtypeStruct((m, D_MODEL), F32),
        compiler_params=pltpu.CompilerParams(dimension_semantics=("arbitrary",), vmem_limit_bytes=VMEM_LIMIT),
        name="merge_out",
    )(x2d, g, *ys, wg, wb, wo)


def _causal_conv(x, w_ref, halo_ref, buf_ref):
    buf_ref[0:8, :] = halo_ref[...]
    buf_ref[8:8 + TILE, :] = x
    y = w_ref[CONV_K - 1:CONV_K, :] * x
    for j in range(CONV_K - 1):
        y = y + w_ref[j:j + 1, :] * buf_ref[8 - (CONV_K - 1) + j:8 - (CONV_K - 1) + j + TILE, :]
    halo_ref[...] = x[TILE - 8:TILE, :]
    return y


def _np_lane_expand(first_row, n_heads, width):
    e = np.zeros((LANE, n_heads * width), np.float32)
    for h in range(n_heads):
        e[first_row + h, h * width:(h + 1) * width] = 1.0
    return e


def _pad_lanes(vec, at):
    out = jnp.zeros((1, LANE), F32)
    return lax.dynamic_update_slice(out, vec.reshape(1, -1).astype(F32), (0, at))


def _gdn_kernel(qkv_ref, small_ref, gate_ref, cw_ref, apad_ref, dtb_ref, ng_ref, btril_ref, eg_ref, eb_ref,
                o_ref, st_ref, halo_ref, buf_ref):
    @pl.when(pl.program_id(1) == 0)
    def _():
        st_ref[...] = jnp.zeros_like(st_ref)
        halo_ref[...] = jnp.zeros_like(halo_ref)

    act = _silu(_causal_conv(qkv_ref[...], cw_ref, halo_ref, buf_ref))
    small = small_ref[...]
    g_full = -apad_ref[...] * _softplus(small + dtb_ref[...])
    beta_full = _sigmoid(small)
    gc_full = _dot(btril_ref[...], g_full, HIGHEST)
    g_lb = _dot(gc_full, eg_ref[...], HIGHEST)
    b_lb = _dot(beta_full, eb_ref[...], HIGHEST)

    r = lax.broadcasted_iota(I32, (TILE, TILE), 0)
    c = lax.broadcasted_iota(I32, (TILE, TILE), 1)
    same = (r >> 6) == (c >> 6)
    incl = same & (r >= c)
    strict = same & (r > c)
    eye = (r == c).astype(F32)
    n_sub = TILE // GDN_CHUNK
    ng = ng_ref[...]
    outs = []
    for h in range(GDN_HEADS):
        hs = slice(h * LANE, (h + 1) * LANE)
        q = act[:, h * GDN_DK:(h + 1) * GDN_DK]
        k = act[:, BRANCH_W + h * GDN_DK:BRANCH_W + (h + 1) * GDN_DK]
        v = act[:, 2 * BRANCH_W + h * GDN_DK:2 * BRANCH_W + (h + 1) * GDN_DK]
        q = q * lax.rsqrt(jnp.sum(q * q, axis=-1, keepdims=True) + EPS) * (GDN_DK ** -0.5)
        k = k * lax.rsqrt(jnp.sum(k * k, axis=-1, keepdims=True) + EPS)
        cb = g_lb[:, hs]
        rb = cb.T
        bb = b_lb[:, hs]
        dm = jnp.where(incl, jnp.exp(jnp.where(incl, cb - rb, 0.0)), 0.0)
        kb = k.astype(BF16)
        kk = _dot_nt(kb, kb)
        qk = _dot_nt(q.astype(BF16), kb) * dm
        mm = jnp.where(strict, -(bb * kk * dm), 0.0)
        inv = eye + mm
        mp = mm
        for _ in range(int(np.log2(GDN_CHUNK)) - 1):
            mp = _dot(mp, mp, HIGHEST)
            inv = inv + _dot(inv, mp, HIGHEST)
        eg = jnp.exp(cb)
        rhs = jnp.concatenate([bb * eg * k, bb * v], axis=1)
        sol = _dot(inv.astype(BF16), rhs.astype(BF16))
        wk = sol[:, :GDN_DK].astype(BF16)
        u0 = sol[:, GDN_DK:]
        qd = (q * eg).astype(BF16)
        qkb = qk.astype(BF16)
        zero = jnp.zeros((GDN_CHUNK, LANE), F32)
        o_parts = []
        for cc in range(n_sub):
            rows = slice(cc * GDN_CHUNK, (cc + 1) * GDN_CHUNK)
            glast = cb[(cc + 1) * GDN_CHUNK - 1:(cc + 1) * GDN_CHUNK, :]
            st = st_ref[h]
            stb = st.astype(BF16)
            u = u0[rows] - _dot(wk[rows], stb)
            upad = jnp.concatenate([u if j == cc else zero for j in range(n_sub)], axis=0).astype(BF16)
            o_parts.append(_dot(qd[rows], stb) + _dot(qkb[rows], upad))
            in_chunk = (r >> 6) == cc
            kd = jnp.where(in_chunk, k * jnp.exp(jnp.where(in_chunk, glast - cb, 0.0)), 0.0)
            st_ref[h] = jnp.exp(glast) * st + _dot(kd.T.astype(BF16), upad)
        o = jnp.concatenate(o_parts, axis=0)
        outs.append(_rms(o, ng))
    y = jnp.concatenate(outs, axis=1) * _silu(gate_ref[...])
    o_ref[...] = y.astype(o_ref.dtype)


def _gdn(p3, conv_w, a_log, dt_bias, norm_g):
    b, l, _ = p3.shape
    btril = np.zeros((TILE, TILE), np.float32)
    for i in range(TILE):
        lo = (i // GDN_CHUNK) * GDN_CHUNK
        btril[i, lo:i + 1] = 1.0
    consts = [conv_w.astype(F32), _pad_lanes(jnp.exp(a_log.astype(F32)), SM_A), _pad_lanes(dt_bias, SM_A),
              norm_g.reshape(1, -1).astype(F32), jnp.asarray(btril),
              jnp.asarray(_np_lane_expand(SM_A, GDN_HEADS, LANE)), jnp.asarray(_np_lane_expand(SM_B, GDN_HEADS, LANE))]
    return pl.pallas_call(
        _gdn_kernel,
        grid=(b, l // TILE),
        in_specs=[pl.BlockSpec((None, TILE, GDN_QKV), lambda i, j: (i, j, P_OFF["a_qkv"] // GDN_QKV)),
                  pl.BlockSpec((None, TILE, LANE), lambda i, j: (i, j, P_OFF["small"] // LANE)),
                  pl.BlockSpec((None, TILE, BRANCH_W), lambda i, j: (i, j, P_OFF["a_gate"] // BRANCH_W))]
        + [_full_spec(a.shape) for a in consts],
        out_specs=pl.BlockSpec((None, TILE, BRANCH_W), lambda i, j: (i, j, 0)),
        out_shape=jax.ShapeDtypeStruct((b, l, BRANCH_W), BF16),
        scratch_shapes=[pltpu.VMEM((GDN_HEADS, GDN_DK, GDN_DK), F32), pltpu.VMEM((8, GDN_QKV), F32),
                        pltpu.VMEM((TILE + 8, GDN_QKV), F32)],
        compiler_params=pltpu.CompilerParams(dimension_semantics=("arbitrary", "arbitrary"),
                                             vmem_limit_bytes=VMEM_LIMIT),
        name="gdn",
    )(p3, p3, p3, *consts)


def _sg_kernel(u_ref, v_ref, gate_ref, lng_ref, lnb_ref, w_ref, bs_ref, o_ref):
    u = _gelu_tanh(u_ref[...])
    v = _gelu_tanh(v_ref[...])
    mu = jnp.mean(v, axis=-1, keepdims=True)
    var = jnp.mean(jnp.square(v - mu), axis=-1, keepdims=True)
    vn = ((v - mu) * lax.rsqrt(var + 1e-5) * lng_ref[...] + lnb_ref[...]).astype(BF16)
    r = lax.broadcasted_iota(I32, (SG_CHUNK, SG_CHUNK), 0)
    c = lax.broadcasted_iota(I32, (SG_CHUNK, SG_CHUNK), 1)
    gw = BRANCH_W // SG_GROUPS
    mixed = []
    for g in range(SG_GROUPS):
        wc = jnp.where(r >= c, w_ref[g], 0.0).astype(BF16)
        mixed.append(_dot(wc, vn[:, g * gw:(g + 1) * gw]))
    mixed = jnp.concatenate(mixed, axis=1) + bs_ref[...]
    o_ref[...] = (u * mixed * _silu(gate_ref[...])).astype(o_ref.dtype)


def _sg(p3, ln_g, ln_b, w_s, b_s):
    b, l, _ = p3.shape
    gw = BRANCH_W // SG_GROUPS
    bs_full = jnp.repeat(jnp.swapaxes(b_s, 0, 1).astype(F32), gw, axis=1)
    consts = [ln_g.reshape(1, -1).astype(F32), ln_b.reshape(1, -1).astype(F32), w_s.astype(F32), bs_full]
    blk = lambda name: pl.BlockSpec((None, SG_CHUNK, BRANCH_W), lambda i, j: (i, j, P_OFF[name] // BRANCH_W))
    return pl.pallas_call(
        _sg_kernel,
        grid=(b, l // SG_CHUNK),
        in_specs=[blk("b_u"), blk("b_v"), blk("b_gate")] + [_full_spec(a.shape) for a in consts],
        out_specs=pl.BlockSpec((None, SG_CHUNK, BRANCH_W), lambda i, j: (i, j, 0)),
        out_shape=jax.ShapeDtypeStruct((b, l, BRANCH_W), BF16),
        compiler_params=pltpu.CompilerParams(dimension_semantics=("arbitrary", "arbitrary"),
                                             vmem_limit_bytes=VMEM_LIMIT),
        name="spatial_gating",
    )(p3, p3, p3, *consts)


_NEG = -1e30
_INT_MIN = -2 ** 31


def _dsa_kernel(q_ref, iq_ref, small_ref, gate_ref, kk_ref, vv_ref, ik_ref, qg_ref, kg_ref, eiw_ref, su_ref,
                o_ref, kn_ref, key_ref, tau_ref, m_ref, l_ref, acc_ref, run_ref, *, top_k):
    i = pl.program_id(1)
    nh = DSA_HEADS

    @pl.when(i == 0)
    def _():
        kn_ref[...] = _rms(kk_ref[...], kg_ref[...]).astype(BF16)

    lane = lax.broadcasted_iota(I32, (TILE, LANE), 1)
    rowi = lax.broadcasted_iota(I32, (TILE, LANE), 0)
    lo = lane < HALF
    qpos = i * TILE + rowi

    iq = iq_ref[...]
    iq_st = jnp.concatenate(
        [jnp.where(lo if h % 2 == 0 else ~lo, iq[:, (h // 2) * LANE:(h // 2 + 1) * LANE], 0.0) for h in range(nh)],
        axis=0).astype(BF16)
    iw_lb = _dot(small_ref[...] * (IDX_HEADS ** -0.5 * IDX_HD ** -0.5), eiw_ref[...], HIGHEST)
    iw_st = jnp.concatenate([iw_lb[:, h * LANE:(h + 1) * LANE] for h in range(nh)], axis=0)

    def score_chunk(j, carry):
        ikj = ik_ref[pl.ds(pl.multiple_of(j * TILE, TILE), TILE), :].astype(BF16)
        s_h = jnp.maximum(_dot_nt(iq_st, ikj), 0.0) * iw_st
        s = s_h[0:TILE]
        for h in range(1, nh):
            s = s + s_h[h * TILE:(h + 1) * TILE]
        bits = pltpu.bitcast(s, I32)
        key = jnp.where(bits < 0, bits ^ 0x7FFFFFFF, bits)
        key_ref[j] = jnp.where(j * TILE + lane <= qpos, key, _INT_MIN)
        return carry

    lax.fori_loop(0, i + 1, score_chunk, 0)

    def count(pred):
        def body(j, acc):
            return acc + jnp.where(pred(key_ref[j]), 1.0, 0.0)
        acc = lax.fori_loop(0, i + 1, body, jnp.zeros((TILE, LANE), F32))
        return jnp.sum(acc, axis=1, keepdims=True)

    tau_ref[...] = jnp.full((TILE, LANE), _INT_MIN + 1, I32)

    @pl.when((i + 1) * TILE > top_k)
    def _():
        zero = jnp.zeros((TILE, LANE), I32)
        base = jnp.where(count(lambda kx: kx >= zero) >= top_k, zero, _INT_MIN)

        def bit_body(bi, base):
            cand = base | lax.shift_left(jnp.int32(1), 30 - bi)
            return jnp.where(count(lambda kx: kx >= cand) >= top_k, cand, base)

        tau = lax.fori_loop(0, 31, bit_body, base)
        tau_ref[...] = jnp.where(qpos + 1 <= top_k, _INT_MIN + 1, tau)

    tau = tau_ref[...]
    need = top_k - count(lambda kx: kx > tau)

    q = q_ref[...]
    q_parts = []
    for mpair in range(nh // 2):
        qp = q[:, mpair * LANE:(mpair + 1) * LANE]
        sq = qp * qp
        s_lo = jnp.sum(jnp.where(lo, sq, 0.0), axis=1, keepdims=True)
        s_hi = jnp.sum(jnp.where(lo, 0.0, sq), axis=1, keepdims=True)
        qn = qp * lax.rsqrt(jnp.where(lo, s_lo, s_hi) * (1.0 / DSA_HD) + EPS) * qg_ref[...] * (DSA_HD ** -0.5)
        q_parts += [jnp.where(lo, qn, 0.0), jnp.where(lo, 0.0, qn)]
    q_st = jnp.concatenate(q_parts, axis=0).astype(BF16)

    m_ref[...] = jnp.full(m_ref.shape, _NEG, F32)
    l_ref[...] = jnp.zeros_like(l_ref)
    acc_ref[...] = jnp.zeros_like(acc_ref)
    run_ref[...] = jnp.zeros_like(run_ref)

    def attend_chunk(j, carry):
        key = key_ref[j]
        eq = key == tau
        eqf = jnp.where(eq, 1.0, 0.0)
        run = run_ref[...]
        prefix = _dot(eqf.astype(BF16), su_ref[...]) + run
        sel = (key > tau) | (eq & (prefix < need))
        run_ref[...] = run + jnp.sum(eqf, axis=1, keepdims=True)
        off = pl.multiple_of(j * TILE, TILE)
        lg = _dot_nt(q_st, kn_ref[pl.ds(off, TILE), :])
        vj = vv_ref[pl.ds(off, TILE), :].astype(BF16)
        dist = (qpos - (j * TILE + lane)).astype(F32)
        for h in range(nh):
            rows = slice(h * TILE, (h + 1) * TILE)
            lgh = jnp.where(sel, lg[rows] - (2.0 ** (-8.0 * (h + 1) / nh)) * dist, _NEG)
            m_old = m_ref[rows]
            m_new = jnp.maximum(m_old, jnp.max(lgh, axis=1, keepdims=True))
            p = jnp.exp(lgh - m_new)
            alpha = jnp.exp(m_old - m_new)
            l_ref[rows] = alpha * l_ref[rows] + jnp.sum(p, axis=1, keepdims=True)
            acc_ref[rows] = alpha * acc_ref[rows] + _dot(p.astype(BF16), vj)
            m_ref[rows] = m_new
        return carry

    lax.fori_loop(0, i + 1, attend_chunk, 0)

    out = acc_ref[...] / l_ref[...]
    pairs = [jnp.where(lo, out[(2 * mp) * TILE:(2 * mp + 1) * TILE], out[(2 * mp + 1) * TILE:(2 * mp + 2) * TILE])
             for mp in range(nh // 2)]
    o_ref[...] = (jnp.concatenate(pairs, axis=1) * _silu(gate_ref[...])).astype(o_ref.dtype)


def _dsa(p3, q_norm_g, k_norm_g):
    b, l, _ = p3.shape
    top_k = min(DSA_TOPK_MAX, l // 4)
    nb = l // TILE
    su = np.triu(np.ones((LANE, LANE), np.float32), 1)
    dup = lambda g: jnp.concatenate([g, g]).reshape(1, LANE).astype(F32)
    consts = [dup(q_norm_g), dup(k_norm_g), jnp.asarray(_np_lane_expand(SM_IW, IDX_HEADS, LANE)),
              jnp.asarray(su, BF16)]
    qblk = lambda name: pl.BlockSpec((None, TILE, BRANCH_W), lambda i, j: (i, j, P_OFF[name] // BRANCH_W))
    seq = lambda name: pl.BlockSpec((None, l, LANE), lambda i, j: (i, 0, P_OFF[name] // LANE))
    return pl.pallas_call(
        functools.partial(_dsa_kernel, top_k=top_k),
        grid=(b, nb),
        in_specs=[qblk("c_q"), qblk("c_iq"),
                  pl.BlockSpec((None, TILE, LANE), lambda i, j: (i, j, P_OFF["small"] // LANE)),
                  qblk("c_gate"), seq("c_kk"), seq("c_vv"), seq("c_ikik")]
        + [_full_spec(a.shape) for a in consts],
        out_specs=pl.BlockSpec((None, TILE, BRANCH_W), lambda i, j: (i, j, 0)),
        out_shape=jax.ShapeDtypeStruct((b, l, BRANCH_W), BF16),
        scratch_shapes=[pltpu.VMEM((l, LANE), BF16), pltpu.VMEM((nb, TILE, LANE), I32), pltpu.VMEM((TILE, LANE), I32),
                        pltpu.VMEM((DSA_HEADS * TILE, 1), F32), pltpu.VMEM((DSA_HEADS * TILE, 1), F32),
                        pltpu.VMEM((DSA_HEADS * TILE, LANE), F32), pltpu.VMEM((TILE, 1), F32)],
        compiler_params=pltpu.CompilerParams(dimension_semantics=("arbitrary", "arbitrary"),
                                             vmem_limit_bytes=VMEM_LIMIT),
        name="dsa",
    )(p3, p3, p3, p3, p3, p3, p3, *consts)


def _ssd_kernel(z_ref, xbc_ref, small_ref, cw_ref, cbias_ref, dtb_ref, apad_ref, dexp_ref, ng_ref, tril_ref,
                e64_ref, e128_ref, o_ref, st_ref, halo_ref, buf_ref):
    @pl.when(pl.program_id(1) == 0)
    def _():
        st_ref[...] = jnp.zeros_like(st_ref)
        halo_ref[...] = jnp.zeros_like(halo_ref)

    nhp = SSD_HEADS * SSD_HD
    gs = SSD_STATE
    act = _silu(_causal_conv(xbc_ref[...], cw_ref, halo_ref, buf_ref) + cbias_ref[...])
    xs = act[:, :nhp]
    bm = act[:, nhp:nhp + SSD_GROUPS * gs]
    cm = act[:, nhp + SSD_GROUPS * gs:]
    dt_full = _softplus(small_ref[...] + dtb_ref[...])
    cs_full = _dot(tril_ref[...], dt_full * apad_ref[...], HIGHEST)
    dt_exp = _dot(dt_full, e64_ref[...], HIGHEST)
    cs_exp = _dot(cs_full, e64_ref[...], HIGHEST)
    cs_lb = _dot(cs_full, e128_ref[...], HIGHEST)
    xdt = xs * dt_exp
    cs_last = cs_exp[TILE - 1:TILE, :]
    xds = (xdt * jnp.exp(cs_last - cs_exp)).astype(BF16)
    ecs = jnp.exp(cs_exp)
    xdt_b = xdt.astype(BF16)

    r = lax.broadcasted_iota(I32, (TILE, TILE), 0)
    c = lax.broadcasted_iota(I32, (TILE, TILE), 1)
    incl = r >= c
    lo = c < HALF
    hpg = SSD_HEADS // SSD_GROUPS
    gw = hpg * SSD_HD
    y_parts = []
    for g in range(SSD_GROUPS):
        bmg = bm[:, g * gs:(g + 1) * gs]
        cmg = cm[:, g * gs:(g + 1) * gs].astype(BF16)
        cbg = _dot_nt(cmg, bmg.astype(BF16))
        for mp in range(hpg // 2):
            pair = (g * hpg) // 2 + mp
            xpair = xdt_b[:, pair * LANE:(pair + 1) * LANE]
            ys = []
            for e in range(2):
                h = 2 * pair + e
                cb = cs_lb[:, h * LANE:(h + 1) * LANE]
                lm = jnp.where(incl, jnp.exp(jnp.where(incl, cb - cb.T, 0.0)), 0.0)
                ys.append(_dot((cbg * lm).astype(BF16), xpair))
            y_parts.append(jnp.where(lo, ys[0], ys[1]))
        st = st_ref[g]
        y_off = _dot(cmg, st.astype(BF16)) * ecs[:, g * gw:(g + 1) * gw]
        y_parts[-(hpg // 2):] = [yp + y_off[:, k * LANE:(k + 1) * LANE]
                                 for k, yp in enumerate(y_parts[-(hpg // 2):])]
        new = _dot(bmg.T.astype(BF16), xds[:, g * gw:(g + 1) * gw])
        st_ref[g] = jnp.exp(cs_last[:, g * gw:(g + 1) * gw]) * st + new
    y = jnp.concatenate(y_parts, axis=1) + xs * dexp_ref[...]
    yz = y * _silu(z_ref[...])
    o_ref[...] = _rms(yz, ng_ref[...]).astype(o_ref.dtype)


def _ssd(p3, conv_w, conv_b, a_log, dt_bias, d_skip, norm_g):
    b, l, _ = p3.shape
    consts = [conv_w.astype(F32), conv_b.reshape(1, -1).astype(F32), _pad_lanes(dt_bias, SM_DT),
              _pad_lanes(-jnp.exp(a_log.astype(F32)), SM_DT),
              jnp.repeat(d_skip.astype(F32), SSD_HD).reshape(1, -1), norm_g.reshape(1, -1).astype(F32),
              jnp.asarray(np.tril(np.ones((TILE, TILE), np.float32))),
              jnp.asarray(_np_lane_expand(SM_DT, SSD_HEADS, SSD_HD)), jnp.asarray(_np_lane_expand(SM_DT, SSD_HEADS, LANE))]
    return pl.pallas_call(
        _ssd_kernel,
        grid=(b, l // TILE),
        in_specs=[pl.BlockSpec((None, TILE, BRANCH_W), lambda i, j: (i, j, P_OFF["d_z"] // BRANCH_W)),
                  pl.BlockSpec((None, TILE, SSD_XBC), lambda i, j: (i, j, P_OFF["d_xbc"] // SSD_XBC)),
                  pl.BlockSpec((None, TILE, LANE), lambda i, j: (i, j, P_OFF["small"] // LANE))]
        + [_full_spec(a.shape) for a in consts],
        out_specs=pl.BlockSpec((None, TILE, BRANCH_W), lambda i, j: (i, j, 0)),
        out_shape=jax.ShapeDtypeStruct((b, l, BRANCH_W), BF16),
        scratch_shapes=[pltpu.VMEM((SSD_GROUPS, SSD_STATE, (SSD_HEADS // SSD_GROUPS) * SSD_HD), F32),
                        pltpu.VMEM((8, SSD_XBC), F32), pltpu.VMEM((TILE + 8, SSD_XBC), F32)],
        compiler_params=pltpu.CompilerParams(dimension_semantics=("arbitrary", "arbitrary"),
                                             vmem_limit_bytes=VMEM_LIMIT),
        name="ssd",
    )(p3, p3, p3, *consts)


def _mem_kv_kernel(mem_ref, g_ref, w_ref, kg_ref, k_ref, v_ref):
    kv = _dot(_rms(mem_ref[...], g_ref[...]).astype(BF16), w_ref[...])
    for h in range(MEM_HEADS):
        hs = slice(h * MEM_HD, (h + 1) * MEM_HD)
        k_ref[:, hs] = _rms(kv[:, hs], kg_ref[...]).astype(k_ref.dtype)
    v_ref[...] = kv[:, BRANCH_W:].astype(v_ref.dtype)


def _mem_kv(mem, mem_norm_g, w_kv_bf16, k_norm_g):
    b, m, _ = mem.shape
    consts = [mem_norm_g.reshape(1, -1).astype(F32), w_kv_bf16, k_norm_g.reshape(1, -1).astype(F32)]
    out = jax.ShapeDtypeStruct((b, m, BRANCH_W), BF16)
    return pl.pallas_call(
        _mem_kv_kernel,
        grid=(b,),
        in_specs=[pl.BlockSpec((None, m, D_MODEL), lambda i: (i, 0, 0))] + [_full_spec(a.shape) for a in consts],
        out_specs=[pl.BlockSpec((None, m, BRANCH_W), lambda i: (i, 0, 0))] * 2,
        out_shape=[out, out],
        compiler_params=pltpu.CompilerParams(dimension_semantics=("arbitrary",), vmem_limit_bytes=VMEM_LIMIT),
        name="mem_kv",
    )(mem, *consts)


def _mem_attn_kernel(q_ref, gate_ref, k_ref, v_ref, qg_ref, o_ref):
    q = q_ref[...]
    outs = []
    for h in range(MEM_HEADS):
        hs = slice(h * MEM_HD, (h + 1) * MEM_HD)
        qn = _rms(q[:, hs], qg_ref[...]).astype(BF16)
        lg = _dot_nt(qn, k_ref[:, hs]) * (MEM_HD ** -0.5)
        p = jnp.exp(lg - jnp.max(lg, axis=1, keepdims=True))
        outs.append(_dot(p.astype(BF16), v_ref[:, hs]) / jnp.sum(p, axis=1, keepdims=True))
    o_ref[...] = (jnp.concatenate(outs, axis=1) * _silu(gate_ref[...])).astype(o_ref.dtype)


def _mem_attn(p3, k, v, q_norm_g):
    b, l, _ = p3.shape
    m = k.shape[1]
    tm = min(512, l)
    qg = q_norm_g.reshape(1, -1).astype(F32)
    blk = lambda name: pl.BlockSpec((None, tm, BRANCH_W), lambda i, j: (i, j, P_OFF[name] // BRANCH_W))
    kv = pl.BlockSpec((None, m, BRANCH_W), lambda i, j: (i, 0, 0))
    return pl.pallas_call(
        _mem_attn_kernel,
        grid=(b, l // tm),
        in_specs=[blk("m_q"), blk("m_gate"), kv, kv, _full_spec(qg.shape)],
        out_specs=pl.BlockSpec((None, tm, BRANCH_W), lambda i, j: (i, j, 0)),
        out_shape=jax.ShapeDtypeStruct((b, l, BRANCH_W), BF16),
        compiler_params=pltpu.CompilerParams(dimension_semantics=("arbitrary", "arbitrary"),
                                             vmem_limit_bytes=VMEM_LIMIT),
        name="mem_attn",
    )(p3, p3, k, v, qg)


def _reordered_w_in(w):
    cols = jnp.asarray(np.maximum(_P_COLUMNS, 0))
    keep = jnp.asarray((_P_COLUMNS >= 0).astype(np.float32))
    return (jnp.take(w, cols, axis=1) * keep).astype(BF16)


def kernel(x, mem, norm_g, w_in, gdn_conv_w, gdn_a_log, gdn_dt_bias, gdn_norm_g, sg_ln_g, sg_ln_b, sg_w, sg_b, dsa_q_norm_g, dsa_k_norm_g, ssd_conv_w, ssd_conv_b, ssd_a_log, ssd_dt_bias, ssd_d, ssd_norm_g, mem_norm_g, w_mem_kv, mem_q_norm_g, mem_k_norm_g, w_gate, w_branch, w_out):
    b, l, d = x.shape
    depth = norm_g.shape[0]
    x2d = x.reshape(b * l, d)
    for i in range(depth):
        g = norm_g[i].reshape(1, d).astype(F32)
        p3 = _in_proj(x2d, g, _reordered_w_in(w_in[i])).reshape(b, l, P_COLS)
        mk, mv = _mem_kv(mem, mem_norm_g[i], w_mem_kv[i].astype(BF16), mem_k_norm_g[i])
        ys = (
            _gdn(p3, gdn_conv_w[i], gdn_a_log[i], gdn_dt_bias[i], gdn_norm_g[i]),
            _sg(p3, sg_ln_g[i], sg_ln_b[i], sg_w[i], sg_b[i]),
            _dsa(p3, dsa_q_norm_g[i], dsa_k_norm_g[i]),
            _ssd(p3, ssd_conv_w[i], ssd_conv_b[i], ssd_a_log[i], ssd_dt_bias[i], ssd_d[i], ssd_norm_g[i]),
            _mem_attn(p3, mk, mv, mem_q_norm_g[i]),
        )
        ys = [y.reshape(b * l, BRANCH_W) for y in ys]
        x2d = _merge(x2d, g, ys, w_gate[i].astype(BF16), w_branch[i].astype(BF16), w_out[i].astype(BF16))
    return x2d.reshape(b, l, d)
```

```python
import functools

import numpy as np
import jax
import jax.numpy as jnp
from jax import lax
from jax.experimental import pallas as pl
from jax.experimental.pallas import tpu as pltpu

F32 = jnp.float32
BF16 = jnp.bfloat16
I32 = jnp.int32
HIGHEST = lax.Precision.HIGHEST

D_MODEL = 1024
N_BRANCH = 5
BRANCH_W = D_MODEL // 2
CONV_K = 4
EPS = 1e-6
GDN_HEADS = 4
GDN_DK = BRANCH_W // GDN_HEADS
GDN_CHUNK = 64
GDN_QKV = 3 * BRANCH_W
SG_CHUNK = 128
SG_GROUPS = 4
DSA_HEADS = 8
DSA_HD = BRANCH_W // DSA_HEADS
IDX_HEADS = 8
IDX_HD = 64
DSA_TOPK_MAX = 256
Q_BLOCK = 128
SSD_HEADS = 8
SSD_HD = BRANCH_W // SSD_HEADS
SSD_GROUPS = 2
SSD_STATE = 128
SSD_CHUNK = 128
SSD_XBC = SSD_HEADS * SSD_HD + 2 * SSD_GROUPS * SSD_STATE
MEM_HEADS = 4
MEM_HD = BRANCH_W // MEM_HEADS

IN_SPLITS = (
    GDN_QKV, GDN_HEADS, GDN_HEADS, BRANCH_W,
    BRANCH_W, BRANCH_W, BRANCH_W,
    DSA_HEADS * DSA_HD, DSA_HD, DSA_HD, IDX_HEADS * IDX_HD, IDX_HD, IDX_HEADS, BRANCH_W,
    BRANCH_W, SSD_XBC, SSD_HEADS,
    MEM_HEADS * MEM_HD, BRANCH_W,
)
_SEG_NAMES = ("a_qkv", "a_a", "a_b", "a_gate", "b_u", "b_v", "b_gate", "c_q", "c_k", "c_v", "c_iq", "c_ik",
              "c_iw", "c_gate", "d_z", "d_xbc", "d_dt", "m_q", "m_gate")
_SEG_START = dict(zip(_SEG_NAMES, np.concatenate([[0], np.cumsum(IN_SPLITS)[:-1]]).tolist()))
_SEG_WIDTH = dict(zip(_SEG_NAMES, IN_SPLITS))

LANE = 128
HALF = LANE // 2
TILE = 128
VMEM_LIMIT = 56 * 1024 * 1024

_P_LAYOUT = (("a_qkv", 1536), ("b_u", 512), ("d_xbc", 1024), ("a_gate", 512), ("b_v", 512), ("b_gate", 512),
             ("c_q", 512), ("c_iq", 512), ("c_gate", 512), ("d_z", 512), ("m_q", 512), ("m_gate", 512),
             ("c_kk", 128), ("c_vv", 128), ("c_ikik", 128), ("small", 128))
P_OFF = {}
_o = 0
for _n, _w in _P_LAYOUT:
    P_OFF[_n] = _o
    _o += _w
P_COLS = _o
SM_A, SM_B, SM_DT, SM_IW = 0, 4, 8, 16


def _p_columns():
    cols = []
    for name, width in _P_LAYOUT:
        if name in _SEG_START:
            cols += list(range(_SEG_START[name], _SEG_START[name] + width))
        elif name in ("c_kk", "c_vv", "c_ikik"):
            src = {"c_kk": "c_k", "c_vv": "c_v", "c_ikik": "c_ik"}[name]
            one = list(range(_SEG_START[src], _SEG_START[src] + _SEG_WIDTH[src]))
            cols += one + one
        else:
            sm = [-1] * LANE
            for seg, at in (("a_a", SM_A), ("a_b", SM_B), ("d_dt", SM_DT), ("c_iw", SM_IW)):
                for j in range(_SEG_WIDTH[seg]):
                    sm[at + j] = _SEG_START[seg] + j
            cols += sm
    return np.asarray(cols, np.int32)


_P_COLUMNS = _p_columns()


def _dot(a, b, precision=None):
    return jnp.dot(a, b, preferred_element_type=F32, precision=precision)


def _dot_nt(a, b):
    return lax.dot_general(a, b, (((1,), (1,)), ((), ())), preferred_element_type=F32)


def _sigmoid(x):
    return 1.0 / (1.0 + jnp.exp(-x))


def _silu(x):
    return x * _sigmoid(x)


def _softplus(x):
    return jnp.maximum(x, 0.0) + jnp.log1p(jnp.exp(-jnp.abs(x)))


def _gelu_tanh(x):
    return 0.5 * x * (1.0 + jnp.tanh(np.sqrt(2.0 / np.pi).astype(np.float32) * (x + 0.044715 * (x * x * x))))


def _rms(x, g, eps=EPS):
    return x * lax.rsqrt(jnp.mean(x * x, axis=-1, keepdims=True) + eps) * g


def _full_spec(shape):
    nd = len(shape)
    return pl.BlockSpec(shape, lambda *_: (0,) * nd)


def _in_proj_kernel(x_ref, g_ref, w_ref, o_ref, h_ref):
    @pl.when(pl.program_id(1) == 0)
    def _():
        h_ref[...] = _rms(x_ref[...], g_ref[...]).astype(BF16)

    o_ref[...] = _dot(h_ref[...], w_ref[...])


def _in_proj(x2d, g, w_bf16):
    m = x2d.shape[0]
    tm = min(1024, m)
    tn = 512
    return pl.pallas_call(
        _in_proj_kernel,
        grid=(m // tm, P_COLS // tn),
        in_specs=[pl.BlockSpec((tm, D_MODEL), lambda i, j: (i, 0)),
                  pl.BlockSpec((1, D_MODEL), lambda i, j: (0, 0)),
                  pl.BlockSpec((D_MODEL, tn), lambda i, j: (0, j))],
        out_specs=pl.BlockSpec((tm, tn), lambda i, j: (i, j)),
        out_shape=jax.ShapeDtypeStruct((m, P_COLS), F32),
        scratch_shapes=[pltpu.VMEM((tm, D_MODEL), BF16)],
        compiler_params=pltpu.CompilerParams(dimension_semantics=("arbitrary", "arbitrary"),
                                             vmem_limit_bytes=VMEM_LIMIT),
        name="in_proj",
    )(x2d, g, w_bf16)


def _merge_kernel(x_ref, g_ref, y0, y1, y2, y3, y4, wg_ref, wb_ref, wo_ref, o_ref):
    x = x_ref[...]
    h = _rms(x, g_ref[...]).astype(BF16)
    acc = None
    for p, y in enumerate((y0, y1, y2, y3, y4)):
        term = _sigmoid(_dot(h, wg_ref[p])) * _dot(y[...], wb_ref[p])
        acc = term if acc is None else acc + term
    o_ref[...] = x + _dot(acc.astype(BF16), wo_ref[...])


def _merge(x2d, g, ys, wg, wb, wo):
    m = x2d.shape[0]
    tm = min(512, m)
    row = lambda i: (i, 0)
    const = pl.Buffered(1)
    return pl.pallas_call(
        _merge_kernel,
        grid=(m // tm,),
        in_specs=[pl.BlockSpec((tm, D_MODEL), row), pl.BlockSpec((1, D_MODEL), lambda i: (0, 0))]
        + [pl.BlockSpec((tm, BRANCH_W), row)] * N_BRANCH
        + [pl.BlockSpec((N_BRANCH, D_MODEL, D_MODEL), lambda i: (0, 0, 0), pipeline_mode=const),
           pl.BlockSpec((N_BRANCH, BRANCH_W, D_MODEL), lambda i: (0, 0, 0), pipeline_mode=const),
           pl.BlockSpec((D_MODEL, D_MODEL), lambda i: (0, 0), pipeline_mode=const)],
        out_specs=pl.BlockSpec((tm, D_MODEL), row),
        out_shape=jax.ShapeDtypeStruct((m, D_MODEL), F32),
        compiler_params=pltpu.CompilerParams(dimension_semantics=("arbitrary",), vmem_limit_bytes=VMEM_LIMIT),
        name="merge_out",
    )(x2d, g, *ys, wg, wb, wo)


def _causal_conv(x, w_ref, halo_ref, buf_ref):
    buf_ref[0:8, :] = halo_ref[...]
    buf_ref[8:8 + TILE, :] = x
    y = w_ref[CONV_K - 1:CONV_K, :] * x
    for j in range(CONV_K - 1):
        y = y + w_ref[j:j + 1, :] * buf_ref[8 - (CONV_K - 1) + j:8 - (CONV_K - 1) + j + TILE, :]
    halo_ref[...] = x[TILE - 8:TILE, :]
    return y


def _np_lane_expand(first_row, n_heads, width):
    e = np.zeros((LANE, n_heads * width), np.float32)
    for h in range(n_heads):
        e[first_row + h, h * width:(h + 1) * width] = 1.0
    return e


def _pad_lanes(vec, at):
    out = jnp.zeros((1, LANE), F32)
    return lax.dynamic_update_slice(out, vec.reshape(1, -1).astype(F32), (0, at))


def _gdn_kernel(qkv_ref, small_ref, gate_ref, cw_ref, apad_ref, dtb_ref, ng_ref, btril_ref, eg_ref, eb_ref,
                o_ref, st_ref, halo_ref, buf_ref):
    @pl.when(pl.program_id(1) == 0)
    def _():
        st_ref[...] = jnp.zeros_like(st_ref)
        halo_ref[...] = jnp.zeros_like(halo_ref)

    act = _silu(_causal_conv(qkv_ref[...], cw_ref, halo_ref, buf_ref))
    small = small_ref[...]
    g_full = -apad_ref[...] * _softplus(small + dtb_ref[...])
    beta_full = _sigmoid(small)
    gc_full = _dot(btril_ref[...], g_full, HIGHEST)
    g_lb = _dot(gc_full, eg_ref[...], HIGHEST)
    b_lb = _dot(beta_full, eb_ref[...], HIGHEST)

    r = lax.broadcasted_iota(I32, (TILE, TILE), 0)
    c = lax.broadcasted_iota(I32, (TILE, TILE), 1)
    same = (r >> 6) == (c >> 6)
    incl = same & (r >= c)
    strict = same & (r > c)
    eye = (r == c).astype(F32)
    n_sub = TILE // GDN_CHUNK
    ng = ng_ref[...]
    outs = []
    for h in range(GDN_HEADS):
        hs = slice(h * LANE, (h + 1) * LANE)
        q = act[:, h * GDN_DK:(h + 1) * GDN_DK]
        k = act[:, BRANCH_W + h * GDN_DK:BRANCH_W + (h + 1) * GDN_DK]
        v = act[:, 2 * BRANCH_W + h * GDN_DK:2 * BRANCH_W + (h + 1) * GDN_DK]
        q = q * lax.rsqrt(jnp.sum(q * q, axis=-1, keepdims=True) + EPS) * (GDN_DK ** -0.5)
        k = k * lax.rsqrt(jnp.sum(k * k, axis=-1, keepdims=True) + EPS)
        cb = g_lb[:, hs]
        rb = cb.T
        bb = b_lb[:, hs]
        dm = jnp.where(incl, jnp.exp(jnp.where(incl, cb - rb, 0.0)), 0.0)
        kb = k.astype(BF16)
        kk = _dot_nt(kb, kb)
        qk = _dot_nt(q.astype(BF16), kb) * dm
        mm = jnp.where(strict, -(bb * kk * dm), 0.0)
        inv = eye + mm
        mp = mm
        for _ in range(int(np.log2(GDN_CHUNK)) - 1):
            mp = _dot(mp, mp, HIGHEST)
            inv = inv + _dot(inv, mp, HIGHEST)
        eg = jnp.exp(cb)
        rhs = jnp.concatenate([bb * eg * k, bb * v], axis=1)
        sol = _dot(inv.astype(BF16), rhs.astype(BF16))
        wk = sol[:, :GDN_DK].astype(BF16)
        u0 = sol[:, GDN_DK:]
        qd = (q * eg).astype(BF16)
        qkb = qk.astype(BF16)
        zero = jnp.zeros((GDN_CHUNK, LANE), F32)
        o_parts = []
        for cc in range(n_sub):
            rows = slice(cc * GDN_CHUNK, (cc + 1) * GDN_CHUNK)
            glast = cb[(cc + 1) * GDN_CHUNK - 1:(cc + 1) * GDN_CHUNK, :]
            st = st_ref[h]
            stb = st.astype(BF16)
            u = u0[rows] - _dot(wk[rows], stb)
            upad = jnp.concatenate([u if j == cc else zero for j in range(n_sub)], axis=0).astype(BF16)
            o_parts.append(_dot(qd[rows], stb) + _dot(qkb[rows], upad))
            in_chunk = (r >> 6) == cc
            kd = jnp.where(in_chunk, k * jnp.exp(jnp.where(in_chunk, glast - cb, 0.0)), 0.0)
            st_ref[h] = jnp.exp(glast) * st + _dot(kd.T.astype(BF16), upad)
        o = jnp.concatenate(o_parts, axis=0)
        outs.append(_rms(o, ng))
    y = jnp.concatenate(outs, axis=1) * _silu(gate_ref[...])
    o_ref[...] = y.astype(o_ref.dtype)


def _gdn(p3, conv_w, a_log, dt_bias, norm_g):
    b, l, _ = p3.shape
    btril = np.zeros((TILE, TILE), np.float32)
    for i in range(TILE):
        lo = (i // GDN_CHUNK) * GDN_CHUNK
        btril[i, lo:i + 1] = 1.0
    consts = [conv_w.astype(F32), _pad_lanes(jnp.exp(a_log.astype(F32)), SM_A), _pad_lanes(dt_bias, SM_A),
              norm_g.reshape(1, -1).astype(F32), jnp.asarray(btril),
              jnp.asarray(_np_lane_expand(SM_A, GDN_HEADS, LANE)), jnp.asarray(_np_lane_expand(SM_B, GDN_HEADS, LANE))]
    return pl.pallas_call(
        _gdn_kernel,
        grid=(b, l // TILE),
        in_specs=[pl.BlockSpec((None, TILE, GDN_QKV), lambda i, j: (i, j, P_OFF["a_qkv"] // GDN_QKV)),
                  pl.BlockSpec((None, TILE, LANE), lambda i, j: (i, j, P_OFF["small"] // LANE)),
                  pl.BlockSpec((None, TILE, BRANCH_W), lambda i, j: (i, j, P_OFF["a_gate"] // BRANCH_W))]
        + [_full_spec(a.shape) for a in consts],
        out_specs=pl.BlockSpec((None, TILE, BRANCH_W), lambda i, j: (i, j, 0)),
        out_shape=jax.ShapeDtypeStruct((b, l, BRANCH_W), BF16),
        scratch_shapes=[pltpu.VMEM((GDN_HEADS, GDN_DK, GDN_DK), F32), pltpu.VMEM((8, GDN_QKV), F32),
                        pltpu.VMEM((TILE + 8, GDN_QKV), F32)],
        compiler_params=pltpu.CompilerParams(dimension_semantics=("arbitrary", "arbitrary"),
                                             vmem_limit_bytes=VMEM_LIMIT),
        name="gdn",
    )(p3, p3, p3, *consts)


def _sg_kernel(u_ref, v_ref, gate_ref, lng_ref, lnb_ref, w_ref, bs_ref, o_ref):
    u = _gelu_tanh(u_ref[...])
    v = _gelu_tanh(v_ref[...])
    mu = jnp.mean(v, axis=-1, keepdims=True)
    var = jnp.mean(jnp.square(v - mu), axis=-1, keepdims=True)
    vn = ((v - mu) * lax.rsqrt(var + 1e-5) * lng_ref[...] + lnb_ref[...]).astype(BF16)
    r = lax.broadcasted_iota(I32, (SG_CHUNK, SG_CHUNK), 0)
    c = lax.broadcasted_iota(I32, (SG_CHUNK, SG_CHUNK), 1)
    gw = BRANCH_W // SG_GROUPS
    mixed = []
    for g in range(SG_GROUPS):
        wc = jnp.where(r >= c, w_ref[g], 0.0).astype(BF16)
        mixed.append(_dot(wc, vn[:, g * gw:(g + 1) * gw]))
    mixed = jnp.concatenate(mixed, axis=1) + bs_ref[...]
    o_ref[...] = (u * mixed * _silu(gate_ref[...])).astype(o_ref.dtype)


def _sg(p3, ln_g, ln_b, w_s, b_s):
    b, l, _ = p3.shape
    gw = BRANCH_W // SG_GROUPS
    bs_full = jnp.repeat(jnp.swapaxes(b_s, 0, 1).astype(F32), gw, axis=1)
    consts = [ln_g.reshape(1, -1).astype(F32), ln_b.reshape(1, -1).astype(F32), w_s.astype(F32), bs_full]
    blk = lambda name: pl.BlockSpec((None, SG_CHUNK, BRANCH_W), lambda i, j: (i, j, P_OFF[name] // BRANCH_W))
    return pl.pallas_call(
        _sg_kernel,
        grid=(b, l // SG_CHUNK),
        in_specs=[blk("b_u"), blk("b_v"), blk("b_gate")] + [_full_spec(a.shape) for a in consts],
        out_specs=pl.BlockSpec((None, SG_CHUNK, BRANCH_W), lambda i, j: (i, j, 0)),
        out_shape=jax.ShapeDtypeStruct((b, l, BRANCH_W), BF16),
        compiler_params=pltpu.CompilerParams(dimension_semantics=("arbitrary", "arbitrary"),
                                             vmem_limit_bytes=VMEM_LIMIT),
        name="spatial_gating",
    )(p3, p3, p3, *consts)


_NEG = -1e30
_INT_MIN = -2 ** 31


def _fold8(x, op):
    parts = [x[r:r + 8] for r in range(0, x.shape[0], 8)]
    while len(parts) > 1:
        parts = [op(parts[k], parts[k + 1]) for k in range(0, len(parts) - 1, 2)] + (
            [parts[-1]] if len(parts) % 2 else [])
    return parts[0]


def _dsa_kernel(q_ref, iq_ref, small_ref, gate_ref, kk_ref, vv_ref, ik_ref, qg_ref, kg_ref, sl_ref,
                o_ref, kn_ref, vt_ref, ikb_ref, key_ref, tau_ref, acc_ref, *, top_k):
    i = pl.program_id(1)
    nh = DSA_HEADS
    nb = key_ref.shape[0]

    @pl.when(i == 0)
    def _():
        def prep(j, carry):
            rows = pl.ds(pl.multiple_of(j * TILE, TILE), TILE)
            kn_ref[j] = _rms(kk_ref[rows, :], kg_ref[...]).astype(BF16)
            vt_ref[j] = vv_ref[rows, :].T.astype(BF16)
            ikb_ref[j] = ik_ref[rows, :].astype(BF16)
            return carry
        lax.fori_loop(0, nb, prep, 0)

    srow = lax.broadcasted_iota(I32, (TILE, LANE), 0)
    lane = lax.broadcasted_iota(I32, (TILE, LANE), 1)
    lo = lane < HALF
    qpos = i * TILE + lane

    iq = iq_ref[...]
    iq_st = jnp.concatenate(
        [jnp.where(lo if h % 2 == 0 else ~lo, iq[:, (h // 2) * LANE:(h // 2 + 1) * LANE], 0.0) for h in range(nh)],
        axis=0).astype(BF16)
    small_t = small_ref[...].T * (IDX_HEADS ** -0.5 * IDX_HD ** -0.5)
    iw_row = jnp.concatenate([small_t[SM_IW + h:SM_IW + h + 1, :] for h in range(nh)], axis=1)

    def score_chunk(j, carry):
        s_h = jnp.maximum(_dot_nt(ikb_ref[j], iq_st), 0.0) * iw_row
        s = s_h[:, 0:TILE]
        for h in range(1, nh):
            s = s + s_h[:, h * TILE:(h + 1) * TILE]
        bits = pltpu.bitcast(s, I32)
        key = jnp.where(bits < 0, bits ^ 0x7FFFFFFF, bits)
        key_ref[j] = jnp.where(j * TILE + srow <= qpos, key, _INT_MIN)
        return carry

    lax.fori_loop(0, i + 1, score_chunk, 0)

    def count(pred):
        def body(j, acc):
            return acc + _fold8(jnp.where(pred(key_ref[j]), 1.0, 0.0), jnp.add)
        acc = lax.fori_loop(0, i + 1, body, jnp.zeros((8, LANE), F32))
        return jnp.sum(acc, axis=0, keepdims=True)

    tau_ref[...] = jnp.full(tau_ref.shape, _INT_MIN + 1, I32)

    @pl.when((i + 1) * TILE > top_k)
    def _():
        zero = jnp.zeros((1, LANE), I32)
        base = jnp.where(count(lambda kx: kx >= zero) >= top_k, zero, _INT_MIN)

        def bit_body(bi, base):
            cand = base | lax.shift_left(jnp.int32(1), 30 - bi)
            return jnp.where(count(lambda kx: kx >= cand) >= top_k, cand, base)

        tau = lax.fori_loop(0, 31, bit_body, base)
        tau = jnp.where(qpos[0:1, :] + 1 <= top_k, _INT_MIN + 1, tau)
        tau_ref[...] = jnp.broadcast_to(tau, tau_ref.shape)

    tau = tau_ref[0:1, :]
    need = top_k - count(lambda kx: kx > tau)

    q = q_ref[...]
    q_parts = []
    for mpair in range(nh // 2):
        qp = q[:, mpair * LANE:(mpair + 1) * LANE]
        sq = qp * qp
        s_lo = jnp.sum(jnp.where(lo, sq, 0.0), axis=1, keepdims=True)
        s_hi = jnp.sum(jnp.where(lo, 0.0, sq), axis=1, keepdims=True)
        qn = qp * lax.rsqrt(jnp.where(lo, s_lo, s_hi) * (1.0 / DSA_HD) + EPS) * qg_ref[...] * (DSA_HD ** -0.5)
        q_parts += [jnp.where(lo, qn, 0.0), jnp.where(lo, 0.0, qn)]
    q_st = jnp.concatenate(q_parts, axis=0).astype(BF16)

    slopes = [2.0 ** (-8.0 * (h + 1) / nh) for h in range(nh)]
    srow_f = srow.astype(F32)

    def max_chunk(j, carry):
        run, mred = carry
        key = key_ref[j]
        eq = key == tau
        eqf = jnp.where(eq, 1.0, 0.0)
        prefix = _dot(sl_ref[...], eqf.astype(BF16)) + run
        sel = (key > tau) | (eq & (prefix < need))
        key_ref[j] = jnp.where(sel, 1, 0)
        lg = _dot_nt(kn_ref[j], q_st)
        base = (j * TILE).astype(F32)
        new = []
        for h in range(nh):
            x = jnp.where(sel, lg[:, h * TILE:(h + 1) * TILE] + slopes[h] * srow_f, _NEG)
            new.append(jnp.maximum(mred[h], _fold8(x, jnp.maximum) + slopes[h] * base))
        return run + jnp.sum(_fold8(eqf, jnp.add), axis=0, keepdims=True), tuple(new)

    _, mred = lax.fori_loop(0, i + 1, max_chunk,
                            (jnp.zeros((1, LANE), F32), tuple(jnp.full((8, LANE), _NEG, F32) for _ in range(nh))))
    m_row = [jnp.max(mh, axis=0, keepdims=True) for mh in mred]

    acc_ref[...] = jnp.zeros_like(acc_ref)

    def pv_chunk(j, lred):
        sel = key_ref[j] != 0
        lg = _dot_nt(kn_ref[j], q_st)
        base = (j * TILE).astype(F32)
        ps, new = [], []
        for h in range(nh):
            shift = slopes[h] * base - m_row[h]
            p = jnp.exp(jnp.where(sel, lg[:, h * TILE:(h + 1) * TILE] + slopes[h] * srow_f + shift, _NEG))
            new.append(lred[h] + _fold8(p, jnp.add))
            ps.append(p.astype(BF16))
        acc_ref[...] += _dot(vt_ref[j], jnp.concatenate(ps, axis=1))
        return tuple(new)

    lred = lax.fori_loop(0, i + 1, pv_chunk, tuple(jnp.zeros((8, LANE), F32) for _ in range(nh)))

    heads = []
    for h in range(nh):
        l_row = jnp.sum(lred[h], axis=0, keepdims=True)
        heads.append((acc_ref[:, h * TILE:(h + 1) * TILE] / l_row).T)
    pairs = [jnp.where(lo, heads[2 * mp], heads[2 * mp + 1]) for mp in range(nh // 2)]
    o_ref[...] = (jnp.concatenate(pairs, axis=1) * _silu(gate_ref[...])).astype(o_ref.dtype)


def _dsa(p3, q_norm_g, k_norm_g):
    b, l, _ = p3.shape
    top_k = min(DSA_TOPK_MAX, l // 4)
    nb = l // TILE
    sl = np.tril(np.ones((LANE, LANE), np.float32), -1)
    dup = lambda g: jnp.concatenate([g, g]).reshape(1, LANE).astype(F32)
    consts = [dup(q_norm_g), dup(k_norm_g), jnp.asarray(sl, BF16)]
    qblk = lambda name: pl.BlockSpec((None, TILE, BRANCH_W), lambda i, j: (i, j, P_OFF[name] // BRANCH_W))
    seq = lambda name: pl.BlockSpec((None, l, LANE), lambda i, j: (i, 0, P_OFF[name] // LANE))
    chunks = lambda dt: pltpu.VMEM((nb, TILE, LANE), dt)
    return pl.pallas_call(
        functools.partial(_dsa_kernel, top_k=top_k),
        grid=(b, nb),
        in_specs=[qblk("c_q"), qblk("c_iq"),
                  pl.BlockSpec((None, TILE, LANE), lambda i, j: (i, j, P_OFF["small"] // LANE)),
                  qblk("c_gate"), seq("c_kk"), seq("c_vv"), seq("c_ikik")]
        + [_full_spec(a.shape) for a in consts],
        out_specs=pl.BlockSpec((None, TILE, BRANCH_W), lambda i, j: (i, j, 0)),
        out_shape=jax.ShapeDtypeStruct((b, l, BRANCH_W), BF16),
        scratch_shapes=[chunks(BF16), chunks(BF16), chunks(BF16), chunks(I32), pltpu.VMEM((8, LANE), I32),
                        pltpu.VMEM((LANE, DSA_HEADS * TILE), F32)],
        compiler_params=pltpu.CompilerParams(dimension_semantics=("arbitrary", "arbitrary"),
                                             vmem_limit_bytes=VMEM_LIMIT),
        name="dsa",
    )(p3, p3, p3, p3, p3, p3, p3, *consts)


def _ssd_kernel(z_ref, xbc_ref, small_ref, cw_ref, cbias_ref, dtb_ref, apad_ref, dexp_ref, ng_ref, tril_ref,
                e64_ref, e128_ref, o_ref, st_ref, halo_ref, buf_ref):
    @pl.when(pl.program_id(1) == 0)
    def _():
        st_ref[...] = jnp.zeros_like(st_ref)
        halo_ref[...] = jnp.zeros_like(halo_ref)

    nhp = SSD_HEADS * SSD_HD
    gs = SSD_STATE
    act = _silu(_causal_conv(xbc_ref[...], cw_ref, halo_ref, buf_ref) + cbias_ref[...])
    xs = act[:, :nhp]
    bm = act[:, nhp:nhp + SSD_GROUPS * gs]
    cm = act[:, nhp + SSD_GROUPS * gs:]
    dt_full = _softplus(small_ref[...] + dtb_ref[...])
    cs_full = _dot(tril_ref[...], dt_full * apad_ref[...], HIGHEST)
    dt_exp = _dot(dt_full, e64_ref[...], HIGHEST)
    cs_exp = _dot(cs_full, e64_ref[...], HIGHEST)
    cs_lb = _dot(cs_full, e128_ref[...], HIGHEST)
    xdt = xs * dt_exp
    cs_last = cs_exp[TILE - 1:TILE, :]
    xds = (xdt * jnp.exp(cs_last - cs_exp)).astype(BF16)
    ecs = jnp.exp(cs_exp)
    xdt_b = xdt.astype(BF16)

    r = lax.broadcasted_iota(I32, (TILE, TILE), 0)
    c = lax.broadcasted_iota(I32, (TILE, TILE), 1)
    incl = r >= c
    lo = c < HALF
    hpg = SSD_HEADS // SSD_GROUPS
    gw = hpg * SSD_HD
    y_parts = []
    for g in range(SSD_GROUPS):
        bmg = bm[:, g * gs:(g + 1) * gs]
        cmg = cm[:, g * gs:(g + 1) * gs].astype(BF16)
        cbg = _dot_nt(cmg, bmg.astype(BF16))
        for mp in range(hpg // 2):
            pair = (g * hpg) // 2 + mp
            xpair = xdt_b[:, pair * LANE:(pair + 1) * LANE]
            ys = []
            for e in range(2):
                h = 2 * pair + e
                cb = cs_lb[:, h * LANE:(h + 1) * LANE]
                lm = jnp.where(incl, jnp.exp(jnp.where(incl, cb - cb.T, 0.0)), 0.0)
                ys.append(_dot((cbg * lm).astype(BF16), xpair))
            y_parts.append(jnp.where(lo, ys[0], ys[1]))
        st = st_ref[g]
        y_off = _dot(cmg, st.astype(BF16)) * ecs[:, g * gw:(g + 1) * gw]
        y_parts[-(hpg // 2):] = [yp + y_off[:, k * LANE:(k + 1) * LANE]
                                 for k, yp in enumerate(y_parts[-(hpg // 2):])]
        new = _dot(bmg.T.astype(BF16), xds[:, g * gw:(g + 1) * gw])
        st_ref[g] = jnp.exp(cs_last[:, g * gw:(g + 1) * gw]) * st + new
    y = jnp.concatenate(y_parts, axis=1) + xs * dexp_ref[...]
    yz = y * _silu(z_ref[...])
    o_ref[...] = _rms(yz, ng_ref[...]).astype(o_ref.dtype)


def _ssd(p3, conv_w, conv_b, a_log, dt_bias, d_skip, norm_g):
    b, l, _ = p3.shape
    consts = [conv_w.astype(F32), conv_b.reshape(1, -1).astype(F32), _pad_lanes(dt_bias, SM_DT),
              _pad_lanes(-jnp.exp(a_log.astype(F32)), SM_DT),
              jnp.repeat(d_skip.astype(F32), SSD_HD).reshape(1, -1), norm_g.reshape(1, -1).astype(F32),
              jnp.asarray(np.tril(np.ones((TILE, TILE), np.float32))),
              jnp.asarray(_np_lane_expand(SM_DT, SSD_HEADS, SSD_HD)), jnp.asarray(_np_lane_expand(SM_DT, SSD_HEADS, LANE))]
    return pl.pallas_call(
        _ssd_kernel,
        grid=(b, l // TILE),
        in_specs=[pl.BlockSpec((None, TILE, BRANCH_W), lambda i, j: (i, j, P_OFF["d_z"] // BRANCH_W)),
                  pl.BlockSpec((None, TILE, SSD_XBC), lambda i, j: (i, j, P_OFF["d_xbc"] // SSD_XBC)),
                  pl.BlockSpec((None, TILE, LANE), lambda i, j: (i, j, P_OFF["small"] // LANE))]
        + [_full_spec(a.shape) for a in consts],
        out_specs=pl.BlockSpec((None, TILE, BRANCH_W), lambda i, j: (i, j, 0)),
        out_shape=jax.ShapeDtypeStruct((b, l, BRANCH_W), BF16),
        scratch_shapes=[pltpu.VMEM((SSD_GROUPS, SSD_STATE, (SSD_HEADS // SSD_GROUPS) * SSD_HD), F32),
                        pltpu.VMEM((8, SSD_XBC), F32), pltpu.VMEM((TILE + 8, SSD_XBC), F32)],
        compiler_params=pltpu.CompilerParams(dimension_semantics=("arbitrary", "arbitrary"),
                                             vmem_limit_bytes=VMEM_LIMIT),
        name="ssd",
    )(p3, p3, p3, *consts)


def _mem_kv_kernel(mem_ref, g_ref, w_ref, kg_ref, k_ref, v_ref):
    kv = _dot(_rms(mem_ref[...], g_ref[...]).astype(BF16), w_ref[...])
    for h in range(MEM_HEADS):
        hs = slice(h * MEM_HD, (h + 1) * MEM_HD)
        k_ref[:, hs] = _rms(kv[:, hs], kg_ref[...]).astype(k_ref.dtype)
    v_ref[...] = kv[:, BRANCH_W:].astype(v_ref.dtype)


def _mem_kv(mem, mem_norm_g, w_kv_bf16, k_norm_g):
    b, m, _ = mem.shape
    consts = [mem_norm_g.reshape(1, -1).astype(F32), w_kv_bf16, k_norm_g.reshape(1, -1).astype(F32)]
    out = jax.ShapeDtypeStruct((b, m, BRANCH_W), BF16)
    return pl.pallas_call(
        _mem_kv_kernel,
        grid=(b,),
        in_specs=[pl.BlockSpec((None, m, D_MODEL), lambda i: (i, 0, 0))] + [_full_spec(a.shape) for a in consts],
        out_specs=[pl.BlockSpec((None, m, BRANCH_W), lambda i: (i, 0, 0))] * 2,
        out_shape=[out, out],
        compiler_params=pltpu.CompilerParams(dimension_semantics=("arbitrary",), vmem_limit_bytes=VMEM_LIMIT),
        name="mem_kv",
    )(mem, *consts)


def _mem_attn_kernel(q_ref, gate_ref, k_ref, v_ref, qg_ref, o_ref):
    q = q_ref[...]
    outs = []
    for h in range(MEM_HEADS):
        hs = slice(h * MEM_HD, (h + 1) * MEM_HD)
        qn = _rms(q[:, hs], qg_ref[...]).astype(BF16)
        lg = _dot_nt(qn, k_ref[:, hs]) * (MEM_HD ** -0.5)
        p = jnp.exp(lg - jnp.max(lg, axis=1, keepdims=True))
        outs.append(_dot(p.astype(BF16), v_ref[:, hs]) / jnp.sum(p, axis=1, keepdims=True))
    o_ref[...] = (jnp.concatenate(outs, axis=1) * _silu(gate_ref[...])).astype(o_ref.dtype)


def _mem_attn(p3, k, v, q_norm_g):
    b, l, _ = p3.shape
    m = k.shape[1]
    tm = min(512, l)
    qg = q_norm_g.reshape(1, -1).astype(F32)
    blk = lambda name: pl.BlockSpec((None, tm, BRANCH_W), lambda i, j: (i, j, P_OFF[name] // BRANCH_W))
    kv = pl.BlockSpec((None, m, BRANCH_W), lambda i, j: (i, 0, 0))
    return pl.pallas_call(
        _mem_attn_kernel,
        grid=(b, l // tm),
        in_specs=[blk("m_q"), blk("m_gate"), kv, kv, _full_spec(qg.shape)],
        out_specs=pl.BlockSpec((None, tm, BRANCH_W), lambda i, j: (i, j, 0)),
        out_shape=jax.ShapeDtypeStruct((b, l, BRANCH_W), BF16),
        compiler_params=pltpu.CompilerParams(dimension_semantics=("arbitrary", "arbitrary"),
                                             vmem_limit_bytes=VMEM_LIMIT),
        name="mem_attn",
    )(p3, p3, k, v, qg)


def _reordered_w_in(w):
    cols = jnp.asarray(np.maximum(_P_COLUMNS, 0))
    keep = jnp.asarray((_P_COLUMNS >= 0).astype(np.float32))
    return (jnp.take(w, cols, axis=1) * keep).astype(BF16)


def kernel(x, mem, norm_g, w_in, gdn_conv_w, gdn_a_log, gdn_dt_bias, gdn_norm_g, sg_ln_g, sg_ln_b, sg_w, sg_b, dsa_q_norm_g, dsa_k_norm_g, ssd_conv_w, ssd_conv_b, ssd_a_log, ssd_dt_bias, ssd_d, ssd_norm_g, mem_norm_g, w_mem_kv, mem_q_norm_g, mem_k_norm_g, w_gate, w_branch, w_out):
    b, l, d = x.shape
    depth = norm_g.shape[0]
    x2d = x.reshape(b * l, d)
    for i in range(depth):
        g = norm_g[i].reshape(1, d).astype(F32)
        p3 = _in_proj(x2d, g, _reordered_w_in(w_in[i])).reshape(b, l, P_COLS)
        mk, mv = _mem_kv(mem, mem_norm_g[i], w_mem_kv[i].astype(BF16), mem_k_norm_g[i])
        ys = (
            _gdn(p3, gdn_conv_w[i], gdn_a_log[i], gdn_dt_bias[i], gdn_norm_g[i]),
            _sg(p3, sg_ln_g[i], sg_ln_b[i], sg_w[i], sg_b[i]),
            _dsa(p3, dsa_q_norm_g[i], dsa_k_norm_g[i]),
            _ssd(p3, ssd_conv_w[i], ssd_conv_b[i], ssd_a_log[i], ssd_dt_bias[i], ssd_d[i], ssd_norm_g[i]),
            _mem_attn(p3, mk, mv, mem_q_norm_g[i]),
        )
        ys = [y.reshape(b * l, BRANCH_W) for y in ys]
        x2d = _merge(x2d, g, ys, w_gate[i].astype(BF16), w_branch[i].astype(BF16), w_out[i].astype(BF16))
    return x2d.reshape(b, l, d)
```

```python
import functools

import numpy as np
import jax
import jax.numpy as jnp
from jax import lax
from jax.experimental import pallas as pl
from jax.experimental.pallas import tpu as pltpu

F32 = jnp.float32
BF16 = jnp.bfloat16
I32 = jnp.int32
HIGHEST = lax.Precision.HIGHEST

D_MODEL = 1024
N_BRANCH = 5
BRANCH_W = D_MODEL // 2
CONV_K = 4
EPS = 1e-6
GDN_HEADS = 4
GDN_DK = BRANCH_W // GDN_HEADS
GDN_CHUNK = 64
GDN_QKV = 3 * BRANCH_W
SG_CHUNK = 128
SG_GROUPS = 4
DSA_HEADS = 8
DSA_HD = BRANCH_W // DSA_HEADS
IDX_HEADS = 8
IDX_HD = 64
DSA_TOPK_MAX = 256
Q_BLOCK = 128
SSD_HEADS = 8
SSD_HD = BRANCH_W // SSD_HEADS
SSD_GROUPS = 2
SSD_STATE = 128
SSD_CHUNK = 128
SSD_XBC = SSD_HEADS * SSD_HD + 2 * SSD_GROUPS * SSD_STATE
MEM_HEADS = 4
MEM_HD = BRANCH_W // MEM_HEADS

IN_SPLITS = (
    GDN_QKV, GDN_HEADS, GDN_HEADS, BRANCH_W,
    BRANCH_W, BRANCH_W, BRANCH_W,
    DSA_HEADS * DSA_HD, DSA_HD, DSA_HD, IDX_HEADS * IDX_HD, IDX_HD, IDX_HEADS, BRANCH_W,
    BRANCH_W, SSD_XBC, SSD_HEADS,
    MEM_HEADS * MEM_HD, BRANCH_W,
)
_SEG_NAMES = ("a_qkv", "a_a", "a_b", "a_gate", "b_u", "b_v", "b_gate", "c_q", "c_k", "c_v", "c_iq", "c_ik",
              "c_iw", "c_gate", "d_z", "d_xbc", "d_dt", "m_q", "m_gate")
_SEG_START = dict(zip(_SEG_NAMES, np.concatenate([[0], np.cumsum(IN_SPLITS)[:-1]]).tolist()))
_SEG_WIDTH = dict(zip(_SEG_NAMES, IN_SPLITS))

LANE = 128
HALF = LANE // 2
TILE = 128
VMEM_LIMIT = 56 * 1024 * 1024

_P_LAYOUT = (("a_qkv", 1536), ("b_u", 512), ("d_xbc", 1024), ("a_gate", 512), ("b_v", 512), ("b_gate", 512),
             ("c_q", 512), ("c_iq", 512), ("c_gate", 512), ("d_z", 512), ("m_q", 512), ("m_gate", 512),
             ("c_kk", 128), ("c_vv", 128), ("c_ikik", 128), ("small", 128))
P_OFF = {}
_o = 0
for _n, _w in _P_LAYOUT:
    P_OFF[_n] = _o
    _o += _w
P_COLS = _o
SM_A, SM_B, SM_DT, SM_IW = 0, 4, 8, 16


def _p_columns():
    cols = []
    for name, width in _P_LAYOUT:
        if name in _SEG_START:
            cols += list(range(_SEG_START[name], _SEG_START[name] + width))
        elif name in ("c_kk", "c_vv", "c_ikik"):
            src = {"c_kk": "c_k", "c_vv": "c_v", "c_ikik": "c_ik"}[name]
            one = list(range(_SEG_START[src], _SEG_START[src] + _SEG_WIDTH[src]))
            cols += one + one
        else:
            sm = [-1] * LANE
            for seg, at in (("a_a", SM_A), ("a_b", SM_B), ("d_dt", SM_DT), ("c_iw", SM_IW)):
                for j in range(_SEG_WIDTH[seg]):
                    sm[at + j] = _SEG_START[seg] + j
            cols += sm
    return np.asarray(cols, np.int32)


_P_COLUMNS = _p_columns()


def _dot(a, b, precision=None):
    return jnp.dot(a, b, preferred_element_type=F32, precision=precision)


def _dot_nt(a, b):
    return lax.dot_general(a, b, (((1,), (1,)), ((), ())), preferred_element_type=F32)


def _sigmoid(x):
    return 1.0 / (1.0 + jnp.exp(-x))


def _silu(x):
    return x * _sigmoid(x)


def _softplus(x):
    return jnp.maximum(x, 0.0) + jnp.log1p(jnp.exp(-jnp.abs(x)))


def _gelu_tanh(x):
    return 0.5 * x * (1.0 + jnp.tanh(np.sqrt(2.0 / np.pi).astype(np.float32) * (x + 0.044715 * (x * x * x))))


def _rms(x, g, eps=EPS):
    return x * lax.rsqrt(jnp.mean(x * x, axis=-1, keepdims=True) + eps) * g


def _full_spec(shape):
    nd = len(shape)
    return pl.BlockSpec(shape, lambda *_: (0,) * nd)


def _in_proj_kernel(x_ref, g_ref, w_ref, o_ref, h_ref):
    @pl.when(pl.program_id(1) == 0)
    def _():
        h_ref[...] = _rms(x_ref[...], g_ref[...]).astype(BF16)

    o_ref[...] = _dot(h_ref[...], w_ref[...])


def _in_proj(x2d, g, w_bf16):
    m = x2d.shape[0]
    tm = min(1024, m)
    tn = 512
    return pl.pallas_call(
        _in_proj_kernel,
        grid=(m // tm, P_COLS // tn),
        in_specs=[pl.BlockSpec((tm, D_MODEL), lambda i, j: (i, 0)),
                  pl.BlockSpec((1, D_MODEL), lambda i, j: (0, 0)),
                  pl.BlockSpec((D_MODEL, tn), lambda i, j: (0, j))],
        out_specs=pl.BlockSpec((tm, tn), lambda i, j: (i, j)),
        out_shape=jax.ShapeDtypeStruct((m, P_COLS), F32),
        scratch_shapes=[pltpu.VMEM((tm, D_MODEL), BF16)],
        compiler_params=pltpu.CompilerParams(dimension_semantics=("arbitrary", "arbitrary"),
                                             vmem_limit_bytes=VMEM_LIMIT),
        name="in_proj",
    )(x2d, g, w_bf16)


def _merge_kernel(x_ref, g_ref, y0, y1, y2, y3, y4, wg_ref, wb_ref, wo_ref, o_ref):
    x = x_ref[...]
    h = _rms(x, g_ref[...]).astype(BF16)
    acc = None
    for p, y in enumerate((y0, y1, y2, y3, y4)):
        term = _sigmoid(_dot(h, wg_ref[p])) * _dot(y[...], wb_ref[p])
        acc = term if acc is None else acc + term
    o_ref[...] = x + _dot(acc.astype(BF16), wo_ref[...])


def _merge(x2d, g, ys, wg, wb, wo):
    m = x2d.shape[0]
    tm = min(512, m)
    row = lambda i: (i, 0)
    const = pl.Buffered(1)
    return pl.pallas_call(
        _merge_kernel,
        grid=(m // tm,),
        in_specs=[pl.BlockSpec((tm, D_MODEL), row), pl.BlockSpec((1, D_MODEL), lambda i: (0, 0))]
        + [pl.BlockSpec((tm, BRANCH_W), row)] * N_BRANCH
        + [pl.BlockSpec((N_BRANCH, D_MODEL, D_MODEL), lambda i: (0, 0, 0), pipeline_mode=const),
           pl.BlockSpec((N_BRANCH, BRANCH_W, D_MODEL), lambda i: (0, 0, 0), pipeline_mode=const),
           pl.BlockSpec((D_MODEL, D_MODEL), lambda i: (0, 0), pipeline_mode=const)],
        out_specs=pl.BlockSpec((tm, D_MODEL), row),
        out_shape=jax.ShapeDtypeStruct((m, D_MODEL), F32),
        compiler_params=pltpu.CompilerParams(dimension_semantics=("arbitrary",), vmem_limit_bytes=VMEM_LIMIT),
        name="merge_out",
    )(x2d, g, *ys, wg, wb, wo)


def _causal_conv(x, w_ref, halo_ref, buf_ref):
    buf_ref[0:8, :] = halo_ref[...]
    buf_ref[8:8 + TILE, :] = x
    y = w_ref[CONV_K - 1:CONV_K, :] * x
    for j in range(CONV_K - 1):
        y = y + w_ref[j:j + 1, :] * buf_ref[8 - (CONV_K - 1) + j:8 - (CONV_K - 1) + j + TILE, :]
    halo_ref[...] = x[TILE - 8:TILE, :]
    return y


def _np_lane_expand(first_row, n_heads, width):
    e = np.zeros((LANE, n_heads * width), np.float32)
    for h in range(n_heads):
        e[first_row + h, h * width:(h + 1) * width] = 1.0
    return e


def _pad_lanes(vec, at):
    out = jnp.zeros((1, LANE), F32)
    return lax.dynamic_update_slice(out, vec.reshape(1, -1).astype(F32), (0, at))


def _gdn_kernel(qkv_ref, small_ref, gate_ref, cw_ref, apad_ref, dtb_ref, ng_ref, btril_ref, eg_ref, eb_ref,
                o_ref, st_ref, halo_ref, buf_ref):
    @pl.when(pl.program_id(1) == 0)
    def _():
        st_ref[...] = jnp.zeros_like(st_ref)
        halo_ref[...] = jnp.zeros_like(halo_ref)

    act = _silu(_causal_conv(qkv_ref[...], cw_ref, halo_ref, buf_ref))
    small = small_ref[...]
    g_full = -apad_ref[...] * _softplus(small + dtb_ref[...])
    beta_full = _sigmoid(small)
    gc_full = _dot(btril_ref[...], g_full, HIGHEST)
    g_lb = _dot(gc_full, eg_ref[...], HIGHEST)
    b_lb = _dot(beta_full, eb_ref[...], HIGHEST)

    r = lax.broadcasted_iota(I32, (TILE, TILE), 0)
    c = lax.broadcasted_iota(I32, (TILE, TILE), 1)
    same = (r >> 6) == (c >> 6)
    incl = same & (r >= c)
    strict = same & (r > c)
    eye = (r == c).astype(F32)
    n_sub = TILE // GDN_CHUNK
    ng = ng_ref[...]
    heads = range(GDN_HEADS)
    q, k, v, cb, bb = [], [], [], [], []
    for h in heads:
        qh = act[:, h * GDN_DK:(h + 1) * GDN_DK]
        kh = act[:, BRANCH_W + h * GDN_DK:BRANCH_W + (h + 1) * GDN_DK]
        q.append(qh * lax.rsqrt(jnp.sum(qh * qh, axis=-1, keepdims=True) + EPS) * (GDN_DK ** -0.5))
        k.append(kh * lax.rsqrt(jnp.sum(kh * kh, axis=-1, keepdims=True) + EPS))
        v.append(act[:, 2 * BRANCH_W + h * GDN_DK:2 * BRANCH_W + (h + 1) * GDN_DK])
        cb.append(g_lb[:, h * LANE:(h + 1) * LANE])
        bb.append(b_lb[:, h * LANE:(h + 1) * LANE])
    dm = [jnp.where(incl, jnp.exp(jnp.where(incl, cb[h] - cb[h].T, 0.0)), 0.0) for h in heads]
    kb = [k[h].astype(BF16) for h in heads]
    kk = [_dot_nt(kb[h], kb[h]) for h in heads]
    qkb = [(_dot_nt(q[h].astype(BF16), kb[h]) * dm[h]).astype(BF16) for h in heads]
    mp = [jnp.where(strict, -(bb[h] * kk[h] * dm[h]), 0.0) for h in heads]
    inv = [eye + mp[h] for h in heads]
    for _ in range(int(np.log2(GDN_CHUNK)) - 1):
        mpb = [mp[h].astype(BF16) for h in heads]
        mp = [_dot(mpb[h], mpb[h]) for h in heads]
        inv = [inv[h] + _dot(inv[h].astype(BF16), mp[h].astype(BF16)) for h in heads]
    eg = [jnp.exp(cb[h]) for h in heads]
    sol = [_dot(inv[h].astype(BF16),
                jnp.concatenate([bb[h] * eg[h] * k[h], bb[h] * v[h]], axis=1).astype(BF16)) for h in heads]
    wk = [sol[h][:, :GDN_DK].astype(BF16) for h in heads]
    u0 = [sol[h][:, GDN_DK:] for h in heads]
    qd = [(q[h] * eg[h]).astype(BF16) for h in heads]
    zero = jnp.zeros((GDN_CHUNK, LANE), F32)
    st = [st_ref[h] for h in heads]
    o_parts = [[] for _ in heads]
    for cc in range(n_sub):
        rows = slice(cc * GDN_CHUNK, (cc + 1) * GDN_CHUNK)
        in_chunk = (r >> 6) == cc
        glast = [cb[h][(cc + 1) * GDN_CHUNK - 1:(cc + 1) * GDN_CHUNK, :] for h in heads]
        stb = [st[h].astype(BF16) for h in heads]
        u = [u0[h][rows] - _dot(wk[h][rows], stb[h]) for h in heads]
        upad = [jnp.concatenate([u[h] if j == cc else zero for j in range(n_sub)], axis=0).astype(BF16)
                for h in heads]
        for h in heads:
            o_parts[h].append(_dot(qd[h][rows], stb[h]) + _dot(qkb[h][rows], upad[h]))
        kd = [jnp.where(in_chunk, k[h] * jnp.exp(jnp.where(in_chunk, glast[h] - cb[h], 0.0)), 0.0) for h in heads]
        st = [jnp.exp(glast[h]) * st[h] + _dot(kd[h].T.astype(BF16), upad[h]) for h in heads]
    for h in heads:
        st_ref[h] = st[h]
    outs = [_rms(jnp.concatenate(o_parts[h], axis=0), ng) for h in heads]
    y = jnp.concatenate(outs, axis=1) * _silu(gate_ref[...])
    o_ref[...] = y.astype(o_ref.dtype)


def _gdn(p3, conv_w, a_log, dt_bias, norm_g):
    b, l, _ = p3.shape
    btril = np.zeros((TILE, TILE), np.float32)
    for i in range(TILE):
        lo = (i // GDN_CHUNK) * GDN_CHUNK
        btril[i, lo:i + 1] = 1.0
    consts = [conv_w.astype(F32), _pad_lanes(jnp.exp(a_log.astype(F32)), SM_A), _pad_lanes(dt_bias, SM_A),
              norm_g.reshape(1, -1).astype(F32), jnp.asarray(btril),
              jnp.asarray(_np_lane_expand(SM_A, GDN_HEADS, LANE)), jnp.asarray(_np_lane_expand(SM_B, GDN_HEADS, LANE))]
    return pl.pallas_call(
        _gdn_kernel,
        grid=(b, l // TILE),
        in_specs=[pl.BlockSpec((None, TILE, GDN_QKV), lambda i, j: (i, j, P_OFF["a_qkv"] // GDN_QKV)),
                  pl.BlockSpec((None, TILE, LANE), lambda i, j: (i, j, P_OFF["small"] // LANE)),
                  pl.BlockSpec((None, TILE, BRANCH_W), lambda i, j: (i, j, P_OFF["a_gate"] // BRANCH_W))]
        + [_full_spec(a.shape) for a in consts],
        out_specs=pl.BlockSpec((None, TILE, BRANCH_W), lambda i, j: (i, j, 0)),
        out_shape=jax.ShapeDtypeStruct((b, l, BRANCH_W), BF16),
        scratch_shapes=[pltpu.VMEM((GDN_HEADS, GDN_DK, GDN_DK), F32), pltpu.VMEM((8, GDN_QKV), F32),
                        pltpu.VMEM((TILE + 8, GDN_QKV), F32)],
        compiler_params=pltpu.CompilerParams(dimension_semantics=("arbitrary", "arbitrary"),
                                             vmem_limit_bytes=VMEM_LIMIT),
        name="gdn",
    )(p3, p3, p3, *consts)


def _sg_kernel(u_ref, v_ref, gate_ref, lng_ref, lnb_ref, w_ref, bs_ref, o_ref):
    u = _gelu_tanh(u_ref[...])
    v = _gelu_tanh(v_ref[...])
    mu = jnp.mean(v, axis=-1, keepdims=True)
    var = jnp.mean(jnp.square(v - mu), axis=-1, keepdims=True)
    vn = ((v - mu) * lax.rsqrt(var + 1e-5) * lng_ref[...] + lnb_ref[...]).astype(BF16)
    r = lax.broadcasted_iota(I32, (SG_CHUNK, SG_CHUNK), 0)
    c = lax.broadcasted_iota(I32, (SG_CHUNK, SG_CHUNK), 1)
    gw = BRANCH_W // SG_GROUPS
    mixed = []
    for g in range(SG_GROUPS):
        wc = jnp.where(r >= c, w_ref[g], 0.0).astype(BF16)
        mixed.append(_dot(wc, vn[:, g * gw:(g + 1) * gw]))
    mixed = jnp.concatenate(mixed, axis=1) + bs_ref[...]
    o_ref[...] = (u * mixed * _silu(gate_ref[...])).astype(o_ref.dtype)


def _sg(p3, ln_g, ln_b, w_s, b_s):
    b, l, _ = p3.shape
    gw = BRANCH_W // SG_GROUPS
    bs_full = jnp.repeat(jnp.swapaxes(b_s, 0, 1).astype(F32), gw, axis=1)
    consts = [ln_g.reshape(1, -1).astype(F32), ln_b.reshape(1, -1).astype(F32), w_s.astype(F32), bs_full]
    blk = lambda name: pl.BlockSpec((None, SG_CHUNK, BRANCH_W), lambda i, j: (i, j, P_OFF[name] // BRANCH_W))
    return pl.pallas_call(
        _sg_kernel,
        grid=(b, l // SG_CHUNK),
        in_specs=[blk("b_u"), blk("b_v"), blk("b_gate")] + [_full_spec(a.shape) for a in consts],
        out_specs=pl.BlockSpec((None, SG_CHUNK, BRANCH_W), lambda i, j: (i, j, 0)),
        out_shape=jax.ShapeDtypeStruct((b, l, BRANCH_W), BF16),
        compiler_params=pltpu.CompilerParams(dimension_semantics=("arbitrary", "arbitrary"),
                                             vmem_limit_bytes=VMEM_LIMIT),
        name="spatial_gating",
    )(p3, p3, p3, *consts)


_NEG = -1e30
_INT_MIN = -2 ** 31


def _fold8(x, op):
    parts = [x[r:r + 8] for r in range(0, x.shape[0], 8)]
    while len(parts) > 1:
        parts = [op(parts[k], parts[k + 1]) for k in range(0, len(parts) - 1, 2)] + (
            [parts[-1]] if len(parts) % 2 else [])
    return parts[0]


KEY_CHUNK = 2 * TILE


def _dsa_kernel(q_ref, iq_ref, small_ref, gate_ref, kk_ref, vv_ref, ik_ref, qg_ref, kg_ref, sl_ref,
                o_ref, kn_ref, vt_ref, ikb_ref, key_ref, lg_ref, tau_ref, acc_ref, *, top_k):
    i = pl.program_id(1)
    nh = DSA_HEADS
    nc = key_ref.shape[0]
    kc = KEY_CHUNK

    srow = lax.broadcasted_iota(I32, (kc, LANE), 0)
    lane = lax.broadcasted_iota(I32, (kc, LANE), 1)
    qpos = i * TILE + lane
    pos_lanes = (lane == HALF) | (lane == HALF + 1)

    @pl.when(i == 0)
    def _():
        def prep(jc, carry):
            rows = pl.ds(pl.multiple_of(jc * kc, kc), kc)
            kpos = jnp.where(lane == HALF, srow, jc * kc).astype(F32)
            kn = jnp.where(lane < HALF, _rms(kk_ref[rows, :], kg_ref[...]), jnp.where(pos_lanes, kpos, 0.0))
            kn_ref[jc] = kn.astype(BF16)
            v = vv_ref[rows, :]
            vt_ref[jc] = jnp.concatenate([v[r:r + TILE].T for r in range(0, kc, TILE)], axis=1).astype(BF16)
            ikb_ref[jc] = ik_ref[rows, :].astype(BF16)
            return carry
        lax.fori_loop(0, nc, prep, 0)

    n_ch = (i + 2) // 2
    n_ch2 = (n_ch + 1) // 2

    @pl.when(n_ch < nc)
    def _():
        key_ref[n_ch] = jnp.full((kc, LANE), _INT_MIN, I32)

    iq = iq_ref[...]
    lane_q = lax.broadcasted_iota(I32, (TILE, LANE), 1)
    lo = lane_q < HALF
    pos_lanes_q = (lane_q == HALF) | (lane_q == HALF + 1)
    iq_st = jnp.concatenate(
        [jnp.where(lo if h % 2 == 0 else ~lo, iq[:, (h // 2) * LANE:(h // 2 + 1) * LANE], 0.0) for h in range(nh)],
        axis=0).astype(BF16)
    small_t = small_ref[...].T * (IDX_HEADS ** -0.5 * IDX_HD ** -0.5)
    iw_row = jnp.concatenate([small_t[SM_IW + h:SM_IW + h + 1, :] for h in range(nh)], axis=1)

    def score_chunk(jc, carry):
        s_h = jnp.maximum(_dot_nt(ikb_ref[jc], iq_st), 0.0) * iw_row
        s = s_h[:, 0:TILE]
        for h in range(1, nh):
            s = s + s_h[:, h * TILE:(h + 1) * TILE]
        bits = pltpu.bitcast(s, I32)
        key = jnp.where(bits < 0, bits ^ 0x7FFFFFFF, bits)
        key_ref[jc] = jnp.where(jc * kc + srow <= qpos, key, _INT_MIN)
        return carry

    lax.fori_loop(0, n_ch, score_chunk, 0)

    def count(pred):
        def body(j2, acc):
            for u in range(2):
                acc = acc + _fold8(jnp.where(pred(key_ref[2 * j2 + u]), 1.0, 0.0), jnp.add)
            return acc
        acc = lax.fori_loop(0, n_ch2, body, jnp.zeros((8, LANE), F32))
        return jnp.sum(acc, axis=0, keepdims=True)

    tau_ref[...] = jnp.full(tau_ref.shape, _INT_MIN + 1, I32)

    @pl.when((i + 1) * TILE > top_k)
    def _():
        zero = jnp.zeros((1, LANE), I32)
        base = jnp.where(count(lambda kx: kx >= zero) >= top_k, zero, _INT_MIN)

        def bit_body(bi, base):
            cand = base | lax.shift_left(jnp.int32(1), 30 - bi)
            return jnp.where(count(lambda kx: kx >= cand) >= top_k, cand, base)

        tau = lax.fori_loop(0, 31, bit_body, base)
        tau = jnp.where(qpos[0:1, :] + 1 <= top_k, _INT_MIN + 1, tau)
        tau_ref[...] = jnp.broadcast_to(tau, tau_ref.shape)

    tau = tau_ref[0:1, :]
    need = top_k - count(lambda kx: kx > tau)

    q = q_ref[...]
    q_parts = []
    for mpair in range(nh // 2):
        qp = q[:, mpair * LANE:(mpair + 1) * LANE]
        sq = qp * qp
        s_lo = jnp.sum(jnp.where(lo, sq, 0.0), axis=1, keepdims=True)
        s_hi = jnp.sum(jnp.where(lo, 0.0, sq), axis=1, keepdims=True)
        qn = qp * lax.rsqrt(jnp.where(lo, s_lo, s_hi) * (1.0 / DSA_HD) + EPS) * qg_ref[...] * (DSA_HD ** -0.5)
        for e, qh in enumerate((qn, pltpu.roll(qn, HALF, 1))):
            slope = 2.0 ** (-8.0 * (2 * mpair + e + 1) / nh)
            q_parts.append(jnp.where(lo, qh, jnp.where(pos_lanes_q, slope, 0.0)))
    q_st = jnp.concatenate(q_parts, axis=0).astype(BF16)

    def mask_chunk(jc, carry):
        run, mred = carry
        key = key_ref[jc]
        eq = key == tau
        eqf = jnp.where(eq, 1.0, 0.0)
        prefix = _dot(sl_ref[...], eqf.astype(BF16)) + run
        sel = (key > tau) | (eq & (prefix < need))
        lg = _dot_nt(kn_ref[jc], q_st)
        new = []
        for h in range(nh):
            x = jnp.where(sel, lg[:, h * TILE:(h + 1) * TILE], _NEG)
            lg_ref[jc, :, h * TILE:(h + 1) * TILE] = x
            new.append(jnp.maximum(mred[h], _fold8(x, jnp.maximum)))
        return run + jnp.sum(_fold8(eqf, jnp.add), axis=0, keepdims=True), tuple(new)

    _, mred = lax.fori_loop(0, n_ch, mask_chunk,
                            (jnp.zeros((1, LANE), F32), tuple(jnp.full((8, LANE), _NEG, F32) for _ in range(nh))))
    m_row = [jnp.max(mh, axis=0, keepdims=True) for mh in mred]

    acc_ref[...] = jnp.zeros_like(acc_ref)

    def pv_chunk(jc, lred):
        ps, new = [], []
        for h in range(nh):
            p = jnp.exp(lg_ref[jc, :, h * TILE:(h + 1) * TILE] - m_row[h])
            new.append(lred[h] + _fold8(p, jnp.add))
            ps.append(p.astype(BF16))
        acc_ref[...] += _dot(vt_ref[jc], jnp.concatenate(ps, axis=1))
        return tuple(new)

    lred = lax.fori_loop(0, n_ch, pv_chunk, tuple(jnp.zeros((8, LANE), F32) for _ in range(nh)))

    heads = []
    for h in range(nh):
        l_row = jnp.sum(lred[h], axis=0, keepdims=True)
        heads.append((acc_ref[:, h * TILE:(h + 1) * TILE] / l_row).T)
    pairs = [jnp.where(lo, heads[2 * mp], heads[2 * mp + 1]) for mp in range(nh // 2)]
    o_ref[...] = (jnp.concatenate(pairs, axis=1) * _silu(gate_ref[...])).astype(o_ref.dtype)


def _dsa(p3, q_norm_g, k_norm_g):
    b, l, _ = p3.shape
    top_k = min(DSA_TOPK_MAX, l // 4)
    nb = l // TILE
    nc = l // KEY_CHUNK
    sl = np.tril(np.ones((KEY_CHUNK, KEY_CHUNK), np.float32), -1)
    dup = lambda g: jnp.concatenate([g, g]).reshape(1, LANE).astype(F32)
    consts = [dup(q_norm_g), dup(k_norm_g), jnp.asarray(sl, BF16)]
    qblk = lambda name: pl.BlockSpec((None, TILE, BRANCH_W), lambda i, j: (i, j, P_OFF[name] // BRANCH_W))
    seq = lambda name: pl.BlockSpec((None, l, LANE), lambda i, j: (i, 0, P_OFF[name] // LANE),
                                    pipeline_mode=pl.Buffered(1))
    chunks = lambda dt: pltpu.VMEM((nc, KEY_CHUNK, LANE), dt)
    return pl.pallas_call(
        functools.partial(_dsa_kernel, top_k=top_k),
        grid=(b, nb),
        in_specs=[qblk("c_q"), qblk("c_iq"),
                  pl.BlockSpec((None, TILE, LANE), lambda i, j: (i, j, P_OFF["small"] // LANE)),
                  qblk("c_gate"), seq("c_kk"), seq("c_vv"), seq("c_ikik")]
        + [_full_spec(a.shape) for a in consts],
        out_specs=pl.BlockSpec((None, TILE, BRANCH_W), lambda i, j: (i, j, 0)),
        out_shape=jax.ShapeDtypeStruct((b, l, BRANCH_W), BF16),
        scratch_shapes=[chunks(BF16), pltpu.VMEM((nc, LANE, KEY_CHUNK), BF16), chunks(BF16), chunks(I32),
                        pltpu.VMEM((nc, KEY_CHUNK, DSA_HEADS * TILE), F32), pltpu.VMEM((8, LANE), I32),
                        pltpu.VMEM((LANE, DSA_HEADS * TILE), F32)],
        compiler_params=pltpu.CompilerParams(dimension_semantics=("arbitrary", "arbitrary"),
                                             vmem_limit_bytes=VMEM_LIMIT),
        name="dsa",
    )(p3, p3, p3, p3, p3, p3, p3, *consts)


def _ssd_kernel(z_ref, xbc_ref, small_ref, cw_ref, cbias_ref, dtb_ref, apad_ref, dexp_ref, ng_ref, tril_ref,
                e64_ref, e128_ref, o_ref, st_ref, halo_ref, buf_ref):
    @pl.when(pl.program_id(1) == 0)
    def _():
        st_ref[...] = jnp.zeros_like(st_ref)
        halo_ref[...] = jnp.zeros_like(halo_ref)

    nhp = SSD_HEADS * SSD_HD
    gs = SSD_STATE
    act = _silu(_causal_conv(xbc_ref[...], cw_ref, halo_ref, buf_ref) + cbias_ref[...])
    xs = act[:, :nhp]
    bm = act[:, nhp:nhp + SSD_GROUPS * gs]
    cm = act[:, nhp + SSD_GROUPS * gs:]
    dt_full = _softplus(small_ref[...] + dtb_ref[...])
    cs_full = _dot(tril_ref[...], dt_full * apad_ref[...], HIGHEST)
    dt_exp = _dot(dt_full, e64_ref[...], HIGHEST)
    cs_exp = _dot(cs_full, e64_ref[...], HIGHEST)
    cs_lb = _dot(cs_full, e128_ref[...], HIGHEST)
    xdt = xs * dt_exp
    cs_last = cs_exp[TILE - 1:TILE, :]
    xds = (xdt * jnp.exp(cs_last - cs_exp)).astype(BF16)
    ecs = jnp.exp(cs_exp)
    xdt_b = xdt.astype(BF16)

    r = lax.broadcasted_iota(I32, (TILE, TILE), 0)
    c = lax.broadcasted_iota(I32, (TILE, TILE), 1)
    incl = r >= c
    lo = c < HALF
    hpg = SSD_HEADS // SSD_GROUPS
    gw = hpg * SSD_HD
    y_parts = []
    for g in range(SSD_GROUPS):
        bmg = bm[:, g * gs:(g + 1) * gs]
        cmg = cm[:, g * gs:(g + 1) * gs].astype(BF16)
        cbg = _dot_nt(cmg, bmg.astype(BF16))
        for mp in range(hpg // 2):
            pair = (g * hpg) // 2 + mp
            xpair = xdt_b[:, pair * LANE:(pair + 1) * LANE]
            ys = []
            for e in range(2):
                h = 2 * pair + e
                cb = cs_lb[:, h * LANE:(h + 1) * LANE]
                lm = jnp.where(incl, jnp.exp(jnp.where(incl, cb - cb.T, 0.0)), 0.0)
                ys.append(_dot((cbg * lm).astype(BF16), xpair))
            y_parts.append(jnp.where(lo, ys[0], ys[1]))
        st = st_ref[g]
        y_off = _dot(cmg, st.astype(BF16)) * ecs[:, g * gw:(g + 1) * gw]
        y_parts[-(hpg // 2):] = [yp + y_off[:, k * LANE:(k + 1) * LANE]
                                 for k, yp in enumerate(y_parts[-(hpg // 2):])]
        new = _dot(bmg.T.astype(BF16), xds[:, g * gw:(g + 1) * gw])
        st_ref[g] = jnp.exp(cs_last[:, g * gw:(g + 1) * gw]) * st + new
    y = jnp.concatenate(y_parts, axis=1) + xs * dexp_ref[...]
    yz = y * _silu(z_ref[...])
    o_ref[...] = _rms(yz, ng_ref[...]).astype(o_ref.dtype)


def _ssd(p3, conv_w, conv_b, a_log, dt_bias, d_skip, norm_g):
    b, l, _ = p3.shape
    consts = [conv_w.astype(F32), conv_b.reshape(1, -1).astype(F32), _pad_lanes(dt_bias, SM_DT),
              _pad_lanes(-jnp.exp(a_log.astype(F32)), SM_DT),
              jnp.repeat(d_skip.astype(F32), SSD_HD).reshape(1, -1), norm_g.reshape(1, -1).astype(F32),
              jnp.asarray(np.tril(np.ones((TILE, TILE), np.float32))),
              jnp.asarray(_np_lane_expand(SM_DT, SSD_HEADS, SSD_HD)), jnp.asarray(_np_lane_expand(SM_DT, SSD_HEADS, LANE))]
    return pl.pallas_call(
        _ssd_kernel,
        grid=(b, l // TILE),
        in_specs=[pl.BlockSpec((None, TILE, BRANCH_W), lambda i, j: (i, j, P_OFF["d_z"] // BRANCH_W)),
                  pl.BlockSpec((None, TILE, SSD_XBC), lambda i, j: (i, j, P_OFF["d_xbc"] // SSD_XBC)),
                  pl.BlockSpec((None, TILE, LANE), lambda i, j: (i, j, P_OFF["small"] // LANE))]
        + [_full_spec(a.shape) for a in consts],
        out_specs=pl.BlockSpec((None, TILE, BRANCH_W), lambda i, j: (i, j, 0)),
        out_shape=jax.ShapeDtypeStruct((b, l, BRANCH_W), BF16),
        scratch_shapes=[pltpu.VMEM((SSD_GROUPS, SSD_STATE, (SSD_HEADS // SSD_GROUPS) * SSD_HD), F32),
                        pltpu.VMEM((8, SSD_XBC), F32), pltpu.VMEM((TILE + 8, SSD_XBC), F32)],
        compiler_params=pltpu.CompilerParams(dimension_semantics=("arbitrary", "arbitrary"),
                                             vmem_limit_bytes=VMEM_LIMIT),
        name="ssd",
    )(p3, p3, p3, *consts)


def _mem_kv_kernel(mem_ref, g_ref, w_ref, kg_ref, k_ref, v_ref):
    kv = _dot(_rms(mem_ref[...], g_ref[...]).astype(BF16), w_ref[...])
    for h in range(MEM_HEADS):
        hs = slice(h * MEM_HD, (h + 1) * MEM_HD)
        k_ref[:, hs] = _rms(kv[:, hs], kg_ref[...]).astype(k_ref.dtype)
    v_ref[...] = kv[:, BRANCH_W:].astype(v_ref.dtype)


def _mem_kv(mem, mem_norm_g, w_kv_bf16, k_norm_g):
    b, m, _ = mem.shape
    consts = [mem_norm_g.reshape(1, -1).astype(F32), w_kv_bf16, k_norm_g.reshape(1, -1).astype(F32)]
    out = jax.ShapeDtypeStruct((b, m, BRANCH_W), BF16)
    return pl.pallas_call(
        _mem_kv_kernel,
        grid=(b,),
        in_specs=[pl.BlockSpec((None, m, D_MODEL), lambda i: (i, 0, 0))] + [_full_spec(a.shape) for a in consts],
        out_specs=[pl.BlockSpec((None, m, BRANCH_W), lambda i: (i, 0, 0))] * 2,
        out_shape=[out, out],
        compiler_params=pltpu.CompilerParams(dimension_semantics=("arbitrary",), vmem_limit_bytes=VMEM_LIMIT),
        name="mem_kv",
    )(mem, *consts)


def _mem_attn_kernel(q_ref, gate_ref, k_ref, v_ref, qg_ref, o_ref):
    q = q_ref[...]
    outs = []
    for h in range(MEM_HEADS):
        hs = slice(h * MEM_HD, (h + 1) * MEM_HD)
        qn = _rms(q[:, hs], qg_ref[...]).astype(BF16)
        lg = _dot_nt(qn, k_ref[:, hs]) * (MEM_HD ** -0.5)
        p = jnp.exp(lg - jnp.max(lg, axis=1, keepdims=True))
        outs.append(_dot(p.astype(BF16), v_ref[:, hs]) / jnp.sum(p, axis=1, keepdims=True))
    o_ref[...] = (jnp.concatenate(outs, axis=1) * _silu(gate_ref[...])).astype(o_ref.dtype)


def _mem_attn(p3, k, v, q_norm_g):
    b, l, _ = p3.shape
    m = k.shape[1]
    tm = min(512, l)
    qg = q_norm_g.reshape(1, -1).astype(F32)
    blk = lambda name: pl.BlockSpec((None, tm, BRANCH_W), lambda i, j: (i, j, P_OFF[name] // BRANCH_W))
    kv = pl.BlockSpec((None, m, BRANCH_W), lambda i, j: (i, 0, 0))
    return pl.pallas_call(
        _mem_attn_kernel,
        grid=(b, l // tm),
        in_specs=[blk("m_q"), blk("m_gate"), kv, kv, _full_spec(qg.shape)],
        out_specs=pl.BlockSpec((None, tm, BRANCH_W), lambda i, j: (i, j, 0)),
        out_shape=jax.ShapeDtypeStruct((b, l, BRANCH_W), BF16),
        compiler_params=pltpu.CompilerParams(dimension_semantics=("arbitrary", "arbitrary"),
                                             vmem_limit_bytes=VMEM_LIMIT),
        name="mem_attn",
    )(p3, p3, k, v, qg)


def _reordered_w_in(w):
    cols = jnp.asarray(np.maximum(_P_COLUMNS, 0))
    keep = jnp.asarray((_P_COLUMNS >= 0).astype(np.float32))
    return (jnp.take(w, cols, axis=1) * keep).astype(BF16)


def kernel(x, mem, norm_g, w_in, gdn_conv_w, gdn_a_log, gdn_dt_bias, gdn_norm_g, sg_ln_g, sg_ln_b, sg_w, sg_b, dsa_q_norm_g, dsa_k_norm_g, ssd_conv_w, ssd_conv_b, ssd_a_log, ssd_dt_bias, ssd_d, ssd_norm_g, mem_norm_g, w_mem_kv, mem_q_norm_g, mem_k_norm_g, w_gate, w_branch, w_out):
    b, l, d = x.shape
    depth = norm_g.shape[0]
    x2d = x.reshape(b * l, d)
    for i in range(depth):
        g = norm_g[i].reshape(1, d).astype(F32)
        p3 = _in_proj(x2d, g, _reordered_w_in(w_in[i])).reshape(b, l, P_COLS)
        mk, mv = _mem_kv(mem, mem_norm_g[i], w_mem_kv[i].astype(BF16), mem_k_norm_g[i])
        ys = (
            _gdn(p3, gdn_conv_w[i], gdn_a_log[i], gdn_dt_bias[i], gdn_norm_g[i]),
            _sg(p3, sg_ln_g[i], sg_ln_b[i], sg_w[i], sg_b[i]),
            _dsa(p3, dsa_q_norm_g[i], dsa_k_norm_g[i]),
            _ssd(p3, ssd_conv_w[i], ssd_conv_b[i], ssd_a_log[i], ssd_dt_bias[i], ssd_d[i], ssd_norm_g[i]),
            _mem_attn(p3, mk, mv, mem_q_norm_g[i]),
        )
        ys = [y.reshape(b * l, BRANCH_W) for y in ys]
        x2d = _merge(x2d, g, ys, w_gate[i].astype(BF16), w_branch[i].astype(BF16), w_out[i].astype(BF16))
    return x2d.reshape(b, l, d)
```

```python
import functools

import numpy as np
import jax
import jax.numpy as jnp
from jax import lax
from jax.experimental import pallas as pl
from jax.experimental.pallas import tpu as pltpu

F32 = jnp.float32
BF16 = jnp.bfloat16
I32 = jnp.int32
HIGHEST = lax.Precision.HIGHEST

D_MODEL = 1024
N_BRANCH = 5
BRANCH_W = D_MODEL // 2
CONV_K = 4
EPS = 1e-6
GDN_HEADS = 4
GDN_DK = BRANCH_W // GDN_HEADS
GDN_CHUNK = 64
GDN_QKV = 3 * BRANCH_W
SG_CHUNK = 128
SG_GROUPS = 4
DSA_HEADS = 8
DSA_HD = BRANCH_W // DSA_HEADS
IDX_HEADS = 8
IDX_HD = 64
DSA_TOPK_MAX = 256
Q_BLOCK = 128
SSD_HEADS = 8
SSD_HD = BRANCH_W // SSD_HEADS
SSD_GROUPS = 2
SSD_STATE = 128
SSD_CHUNK = 128
SSD_XBC = SSD_HEADS * SSD_HD + 2 * SSD_GROUPS * SSD_STATE
MEM_HEADS = 4
MEM_HD = BRANCH_W // MEM_HEADS

IN_SPLITS = (
    GDN_QKV, GDN_HEADS, GDN_HEADS, BRANCH_W,
    BRANCH_W, BRANCH_W, BRANCH_W,
    DSA_HEADS * DSA_HD, DSA_HD, DSA_HD, IDX_HEADS * IDX_HD, IDX_HD, IDX_HEADS, BRANCH_W,
    BRANCH_W, SSD_XBC, SSD_HEADS,
    MEM_HEADS * MEM_HD, BRANCH_W,
)
_SEG_NAMES = ("a_qkv", "a_a", "a_b", "a_gate", "b_u", "b_v", "b_gate", "c_q", "c_k", "c_v", "c_iq", "c_ik",
              "c_iw", "c_gate", "d_z", "d_xbc", "d_dt", "m_q", "m_gate")
_SEG_START = dict(zip(_SEG_NAMES, np.concatenate([[0], np.cumsum(IN_SPLITS)[:-1]]).tolist()))
_SEG_WIDTH = dict(zip(_SEG_NAMES, IN_SPLITS))

LANE = 128
HALF = LANE // 2
TILE = 128
VMEM_LIMIT = 56 * 1024 * 1024

_P_LAYOUT = (("a_qkv", 1536), ("b_u", 512), ("d_xbc", 1024), ("a_gate", 512), ("b_v", 512), ("b_gate", 512),
             ("c_q", 512), ("c_iq", 512), ("c_gate", 512), ("d_z", 512), ("m_q", 512), ("m_gate", 512),
             ("c_kk", 128), ("c_vv", 128), ("c_ikik", 128), ("small", 128))
P_OFF = {}
_o = 0
for _n, _w in _P_LAYOUT:
    P_OFF[_n] = _o
    _o += _w
P_COLS = _o
P_MAIN = P_OFF["small"]
SM_A, SM_B, SM_DT, SM_IW = 0, 4, 8, 16


def _p_columns():
    cols = []
    for name, width in _P_LAYOUT:
        if name in _SEG_START:
            cols += list(range(_SEG_START[name], _SEG_START[name] + width))
        elif name in ("c_kk", "c_vv", "c_ikik"):
            src = {"c_kk": "c_k", "c_vv": "c_v", "c_ikik": "c_ik"}[name]
            one = list(range(_SEG_START[src], _SEG_START[src] + _SEG_WIDTH[src]))
            cols += one + one
        else:
            sm = [-1] * LANE
            for seg, at in (("a_a", SM_A), ("a_b", SM_B), ("d_dt", SM_DT), ("c_iw", SM_IW)):
                for j in range(_SEG_WIDTH[seg]):
                    sm[at + j] = _SEG_START[seg] + j
            cols += sm
    return np.asarray(cols, np.int32)


_P_COLUMNS = _p_columns()


def _dot(a, b, precision=None):
    return jnp.dot(a, b, preferred_element_type=F32, precision=precision)


def _dot_nt(a, b):
    return lax.dot_general(a, b, (((1,), (1,)), ((), ())), preferred_element_type=F32)


def _sigmoid(x):
    return 1.0 / (1.0 + jnp.exp(-x))


def _silu(x):
    return x * _sigmoid(x)


def _softplus(x):
    return jnp.maximum(x, 0.0) + jnp.log1p(jnp.exp(-jnp.abs(x)))


def _gelu_tanh(x):
    return 0.5 * x * (1.0 + jnp.tanh(np.sqrt(2.0 / np.pi).astype(np.float32) * (x + 0.044715 * (x * x * x))))


def _rms(x, g, eps=EPS):
    return x * lax.rsqrt(jnp.mean(x * x, axis=-1, keepdims=True) + eps) * g


def _full_spec(shape):
    nd = len(shape)
    return pl.BlockSpec(shape, lambda *_: (0,) * nd)


def _in_proj_kernel(x_ref, g_ref, w_ref, o_ref, s_ref):
    h = _rms(x_ref[...], g_ref[...]).astype(BF16)
    step = 1024
    for c0 in range(0, P_MAIN, step):
        c1 = min(c0 + step, P_MAIN)
        o_ref[:, c0:c1] = _dot(h, w_ref[:, c0:c1]).astype(o_ref.dtype)
    s_ref[...] = _dot(h, w_ref[:, P_MAIN:])


def _in_proj(x2d, g, w_bf16):
    m = x2d.shape[0]
    tm = min(512, m)
    return pl.pallas_call(
        _in_proj_kernel,
        grid=(m // tm,),
        in_specs=[pl.BlockSpec((tm, D_MODEL), lambda i: (i, 0)),
                  pl.BlockSpec((1, D_MODEL), lambda i: (0, 0)),
                  pl.BlockSpec((D_MODEL, P_COLS), lambda i: (0, 0), pipeline_mode=pl.Buffered(1))],
        out_specs=[pl.BlockSpec((tm, P_MAIN), lambda i: (i, 0)), pl.BlockSpec((tm, LANE), lambda i: (i, 0))],
        out_shape=[jax.ShapeDtypeStruct((m, P_MAIN), BF16), jax.ShapeDtypeStruct((m, LANE), F32)],
        compiler_params=pltpu.CompilerParams(dimension_semantics=("arbitrary",), vmem_limit_bytes=VMEM_LIMIT),
        name="in_proj",
    )(x2d, g, w_bf16)


def _merge_kernel(x_ref, g_ref, y0, y1, y2, y3, y4, wg_ref, wb_ref, wo_ref, o_ref):
    x = x_ref[...]
    h = _rms(x, g_ref[...]).astype(BF16)
    acc = None
    for p, y in enumerate((y0, y1, y2, y3, y4)):
        term = _sigmoid(_dot(h, wg_ref[p])) * _dot(y[...], wb_ref[p])
        acc = term if acc is None else acc + term
    o_ref[...] = x + _dot(acc.astype(BF16), wo_ref[...])


def _merge(x2d, g, ys, wg, wb, wo):
    m = x2d.shape[0]
    tm = min(512, m)
    row = lambda i: (i, 0)
    const = pl.Buffered(1)
    return pl.pallas_call(
        _merge_kernel,
        grid=(m // tm,),
        in_specs=[pl.BlockSpec((tm, D_MODEL), row), pl.BlockSpec((1, D_MODEL), lambda i: (0, 0))]
        + [pl.BlockSpec((tm, BRANCH_W), row)] * N_BRANCH
        + [pl.BlockSpec((N_BRANCH, D_MODEL, D_MODEL), lambda i: (0, 0, 0), pipeline_mode=const),
           pl.BlockSpec((N_BRANCH, BRANCH_W, D_MODEL), lambda i: (0, 0, 0), pipeline_mode=const),
           pl.BlockSpec((D_MODEL, D_MODEL), lambda i: (0, 0), pipeline_mode=const)],
        out_specs=pl.BlockSpec((tm, D_MODEL), row),
        out_shape=jax.ShapeDtypeStruct((m, D_MODEL), F32),
        compiler_params=pltpu.CompilerParams(dimension_semantics=("arbitrary",), vmem_limit_bytes=VMEM_LIMIT),
        name="merge_out",
    )(x2d, g, *ys, wg, wb, wo)


def _causal_conv(x, w_ref, halo_ref, buf_ref):
    buf_ref[0:8, :] = halo_ref[...]
    buf_ref[8:8 + TILE, :] = x
    y = w_ref[CONV_K - 1:CONV_K, :] * x
    for j in range(CONV_K - 1):
        y = y + w_ref[j:j + 1, :] * buf_ref[8 - (CONV_K - 1) + j:8 - (CONV_K - 1) + j + TILE, :]
    halo_ref[...] = x[TILE - 8:TILE, :]
    return y


def _np_lane_expand(first_row, n_heads, width):
    e = np.zeros((LANE, n_heads * width), np.float32)
    for h in range(n_heads):
        e[first_row + h, h * width:(h + 1) * width] = 1.0
    return e


def _pad_lanes(vec, at):
    out = jnp.zeros((1, LANE), F32)
    return lax.dynamic_update_slice(out, vec.reshape(1, -1).astype(F32), (0, at))


def _gdn_kernel(qkv_ref, small_ref, gate_ref, cw_ref, apad_ref, dtb_ref, ng_ref, btril_ref, eg_ref, eb_ref,
                o_ref, st_ref, halo_ref, buf_ref):
    @pl.when(pl.program_id(1) == 0)
    def _():
        st_ref[...] = jnp.zeros_like(st_ref)
        halo_ref[...] = jnp.zeros_like(halo_ref)

    act = _silu(_causal_conv(qkv_ref[...].astype(F32), cw_ref, halo_ref, buf_ref))
    small = small_ref[...]
    g_full = -apad_ref[...] * _softplus(small + dtb_ref[...])
    beta_full = _sigmoid(small)
    gc_full = _dot(btril_ref[...], g_full, HIGHEST)
    g_lb = _dot(gc_full, eg_ref[...], HIGHEST)
    b_lb = _dot(beta_full, eb_ref[...], HIGHEST)

    r = lax.broadcasted_iota(I32, (TILE, TILE), 0)
    c = lax.broadcasted_iota(I32, (TILE, TILE), 1)
    same = (r >> 6) == (c >> 6)
    incl = same & (r >= c)
    strict = same & (r > c)
    eye = (r == c).astype(F32)
    n_sub = TILE // GDN_CHUNK
    ng = ng_ref[...]
    heads = range(GDN_HEADS)
    q, k, v, cb, bb = [], [], [], [], []
    for h in heads:
        qh = act[:, h * GDN_DK:(h + 1) * GDN_DK]
        kh = act[:, BRANCH_W + h * GDN_DK:BRANCH_W + (h + 1) * GDN_DK]
        q.append(qh * lax.rsqrt(jnp.sum(qh * qh, axis=-1, keepdims=True) + EPS) * (GDN_DK ** -0.5))
        k.append(kh * lax.rsqrt(jnp.sum(kh * kh, axis=-1, keepdims=True) + EPS))
        v.append(act[:, 2 * BRANCH_W + h * GDN_DK:2 * BRANCH_W + (h + 1) * GDN_DK])
        cb.append(g_lb[:, h * LANE:(h + 1) * LANE])
        bb.append(b_lb[:, h * LANE:(h + 1) * LANE])
    dm = [jnp.where(incl, jnp.exp(jnp.where(incl, cb[h] - cb[h].T, 0.0)), 0.0) for h in heads]
    kb = [k[h].astype(BF16) for h in heads]
    kk = [_dot_nt(kb[h], kb[h]) for h in heads]
    qkb = [(_dot_nt(q[h].astype(BF16), kb[h]) * dm[h]).astype(BF16) for h in heads]
    mp = [jnp.where(strict, -(bb[h] * kk[h] * dm[h]), 0.0) for h in heads]
    inv = [eye + mp[h] for h in heads]
    for _ in range(int(np.log2(GDN_CHUNK)) - 1):
        mpb = [mp[h].astype(BF16) for h in heads]
        mp = [_dot(mpb[h], mpb[h]) for h in heads]
        inv = [inv[h] + _dot(inv[h].astype(BF16), mp[h].astype(BF16)) for h in heads]
    eg = [jnp.exp(cb[h]) for h in heads]
    sol = [_dot(inv[h].astype(BF16),
                jnp.concatenate([bb[h] * eg[h] * k[h], bb[h] * v[h]], axis=1).astype(BF16)) for h in heads]
    wk = [sol[h][:, :GDN_DK].astype(BF16) for h in heads]
    u0 = [sol[h][:, GDN_DK:] for h in heads]
    qd = [(q[h] * eg[h]).astype(BF16) for h in heads]
    zero = jnp.zeros((GDN_CHUNK, LANE), F32)
    st = [st_ref[h] for h in heads]
    o_parts = [[] for _ in heads]
    for cc in range(n_sub):
        rows = slice(cc * GDN_CHUNK, (cc + 1) * GDN_CHUNK)
        in_chunk = (r >> 6) == cc
        glast = [cb[h][(cc + 1) * GDN_CHUNK - 1:(cc + 1) * GDN_CHUNK, :] for h in heads]
        stb = [st[h].astype(BF16) for h in heads]
        u = [u0[h][rows] - _dot(wk[h][rows], stb[h]) for h in heads]
        upad = [jnp.concatenate([u[h] if j == cc else zero for j in range(n_sub)], axis=0).astype(BF16)
                for h in heads]
        for h in heads:
            o_parts[h].append(_dot(qd[h][rows], stb[h]) + _dot(qkb[h][rows], upad[h]))
        kd = [jnp.where(in_chunk, k[h] * jnp.exp(jnp.where(in_chunk, glast[h] - cb[h], 0.0)), 0.0) for h in heads]
        st = [jnp.exp(glast[h]) * st[h] + _dot(kd[h].T.astype(BF16), upad[h]) for h in heads]
    for h in heads:
        st_ref[h] = st[h]
    outs = [_rms(jnp.concatenate(o_parts[h], axis=0), ng) for h in heads]
    y = jnp.concatenate(outs, axis=1) * _silu(gate_ref[...].astype(F32))
    o_ref[...] = y.astype(o_ref.dtype)


def _gdn(p3, sm3, conv_w, a_log, dt_bias, norm_g):
    b, l, _ = p3.shape
    btril = np.zeros((TILE, TILE), np.float32)
    for i in range(TILE):
        lo = (i // GDN_CHUNK) * GDN_CHUNK
        btril[i, lo:i + 1] = 1.0
    consts = [conv_w.astype(F32), _pad_lanes(jnp.exp(a_log.astype(F32)), SM_A), _pad_lanes(dt_bias, SM_A),
              norm_g.reshape(1, -1).astype(F32), jnp.asarray(btril),
              jnp.asarray(_np_lane_expand(SM_A, GDN_HEADS, LANE)), jnp.asarray(_np_lane_expand(SM_B, GDN_HEADS, LANE))]
    return pl.pallas_call(
        _gdn_kernel,
        grid=(b, l // TILE),
        in_specs=[pl.BlockSpec((None, TILE, GDN_QKV), lambda i, j: (i, j, P_OFF["a_qkv"] // GDN_QKV)),
                  pl.BlockSpec((None, TILE, LANE), lambda i, j: (i, j, 0)),
                  pl.BlockSpec((None, TILE, BRANCH_W), lambda i, j: (i, j, P_OFF["a_gate"] // BRANCH_W))]
        + [_full_spec(a.shape) for a in consts],
        out_specs=pl.BlockSpec((None, TILE, BRANCH_W), lambda i, j: (i, j, 0)),
        out_shape=jax.ShapeDtypeStruct((b, l, BRANCH_W), BF16),
        scratch_shapes=[pltpu.VMEM((GDN_HEADS, GDN_DK, GDN_DK), F32), pltpu.VMEM((8, GDN_QKV), F32),
                        pltpu.VMEM((TILE + 8, GDN_QKV), F32)],
        compiler_params=pltpu.CompilerParams(dimension_semantics=("arbitrary", "arbitrary"),
                                             vmem_limit_bytes=VMEM_LIMIT),
        name="gdn",
    )(p3, sm3, p3, *consts)


def _sg_kernel(u_ref, v_ref, gate_ref, lng_ref, lnb_ref, w_ref, bs_ref, o_ref):
    u = _gelu_tanh(u_ref[...].astype(F32))
    v = _gelu_tanh(v_ref[...].astype(F32))
    mu = jnp.mean(v, axis=-1, keepdims=True)
    var = jnp.mean(jnp.square(v - mu), axis=-1, keepdims=True)
    vn = ((v - mu) * lax.rsqrt(var + 1e-5) * lng_ref[...] + lnb_ref[...]).astype(BF16)
    r = lax.broadcasted_iota(I32, (SG_CHUNK, SG_CHUNK), 0)
    c = lax.broadcasted_iota(I32, (SG_CHUNK, SG_CHUNK), 1)
    gw = BRANCH_W // SG_GROUPS
    mixed = []
    for g in range(SG_GROUPS):
        wc = jnp.where(r >= c, w_ref[g], 0.0).astype(BF16)
        mixed.append(_dot(wc, vn[:, g * gw:(g + 1) * gw]))
    mixed = jnp.concatenate(mixed, axis=1) + bs_ref[...]
    o_ref[...] = (u * mixed * _silu(gate_ref[...].astype(F32))).astype(o_ref.dtype)


def _sg(p3, ln_g, ln_b, w_s, b_s):
    b, l, _ = p3.shape
    gw = BRANCH_W // SG_GROUPS
    bs_full = jnp.repeat(jnp.swapaxes(b_s, 0, 1).astype(F32), gw, axis=1)
    consts = [ln_g.reshape(1, -1).astype(F32), ln_b.reshape(1, -1).astype(F32), w_s.astype(F32), bs_full]
    blk = lambda name: pl.BlockSpec((None, SG_CHUNK, BRANCH_W), lambda i, j: (i, j, P_OFF[name] // BRANCH_W))
    return pl.pallas_call(
        _sg_kernel,
        grid=(b, l // SG_CHUNK),
        in_specs=[blk("b_u"), blk("b_v"), blk("b_gate")] + [_full_spec(a.shape) for a in consts],
        out_specs=pl.BlockSpec((None, SG_CHUNK, BRANCH_W), lambda i, j: (i, j, 0)),
        out_shape=jax.ShapeDtypeStruct((b, l, BRANCH_W), BF16),
        compiler_params=pltpu.CompilerParams(dimension_semantics=("arbitrary", "arbitrary"),
                                             vmem_limit_bytes=VMEM_LIMIT),
        name="spatial_gating",
    )(p3, p3, p3, *consts)


_NEG = -1e30
_INT_MIN = -2 ** 31


def _fold8(x, op):
    parts = [x[r:r + 8] for r in range(0, x.shape[0], 8)]
    while len(parts) > 1:
        parts = [op(parts[k], parts[k + 1]) for k in range(0, len(parts) - 1, 2)] + (
            [parts[-1]] if len(parts) % 2 else [])
    return parts[0]


KEY_CHUNK = 2 * TILE


def _dsa_kernel(q_ref, iq_ref, small_ref, gate_ref, kk_ref, vv_ref, ik_ref, qg_ref, kg_ref, sl_ref,
                o_ref, kn_ref, vt_ref, ikb_ref, key_ref, lg_ref, tau_ref, acc_ref, *, top_k):
    i = pl.program_id(1)
    nh = DSA_HEADS
    nc = key_ref.shape[0]
    kc = KEY_CHUNK

    srow = lax.broadcasted_iota(I32, (kc, LANE), 0)
    lane = lax.broadcasted_iota(I32, (kc, LANE), 1)
    qpos = i * TILE + lane
    pos_lanes = (lane == HALF) | (lane == HALF + 1)

    @pl.when(i == 0)
    def _():
        def prep(jc, carry):
            rows = pl.ds(pl.multiple_of(jc * kc, kc), kc)
            kpos = jnp.where(lane == HALF, srow, jc * kc).astype(F32)
            kn = jnp.where(lane < HALF, _rms(kk_ref[rows, :].astype(F32), kg_ref[...]), jnp.where(pos_lanes, kpos, 0.0))
            kn_ref[jc] = kn.astype(BF16)
            v = vv_ref[rows, :].astype(F32)
            vt_ref[jc] = jnp.concatenate([v[r:r + TILE].T for r in range(0, kc, TILE)], axis=1).astype(BF16)
            ikb_ref[jc] = ik_ref[rows, :]
            return carry
        lax.fori_loop(0, nc, prep, 0)

    n_ch = (i + 2) // 2
    n_ch2 = (n_ch + 1) // 2

    @pl.when(n_ch < nc)
    def _():
        key_ref[n_ch] = jnp.full((kc, LANE), _INT_MIN, I32)

    iq = iq_ref[...].astype(F32)
    lane_q = lax.broadcasted_iota(I32, (TILE, LANE), 1)
    lo = lane_q < HALF
    pos_lanes_q = (lane_q == HALF) | (lane_q == HALF + 1)
    iq_st = jnp.concatenate(
        [jnp.where(lo if h % 2 == 0 else ~lo, iq[:, (h // 2) * LANE:(h // 2 + 1) * LANE], 0.0) for h in range(nh)],
        axis=0).astype(BF16)
    small_t = small_ref[...].T * (IDX_HEADS ** -0.5 * IDX_HD ** -0.5)
    iw_row = jnp.concatenate([small_t[SM_IW + h:SM_IW + h + 1, :] for h in range(nh)], axis=1)

    def score_pair(j2, carry):
        jcs = (2 * j2, 2 * j2 + 1)
        raw = [_dot_nt(ikb_ref[jc], iq_st) for jc in jcs]
        for jc, s_raw in zip(jcs, raw):
            s_h = jnp.maximum(s_raw, 0.0) * iw_row
            s = s_h[:, 0:TILE]
            for h in range(1, nh):
                s = s + s_h[:, h * TILE:(h + 1) * TILE]
            bits = pltpu.bitcast(s, I32)
            key = jnp.where(bits < 0, bits ^ 0x7FFFFFFF, bits)
            key_ref[jc] = jnp.where(jc * kc + srow <= qpos, key, _INT_MIN)
        return carry

    lax.fori_loop(0, n_ch2, score_pair, 0)

    def count(pred):
        def body(j2, acc):
            for u in range(2):
                acc = acc + _fold8(jnp.where(pred(key_ref[2 * j2 + u]), 1.0, 0.0), jnp.add)
            return acc
        acc = lax.fori_loop(0, n_ch2, body, jnp.zeros((8, LANE), F32))
        return jnp.sum(acc, axis=0, keepdims=True)

    tau_ref[...] = jnp.full(tau_ref.shape, _INT_MIN + 1, I32)

    @pl.when((i + 1) * TILE > top_k)
    def _():
        zero = jnp.zeros((1, LANE), I32)
        base = jnp.where(count(lambda kx: kx >= zero) >= top_k, zero, _INT_MIN)

        def bit_body(bi, base):
            cand = base | lax.shift_left(jnp.int32(1), 30 - bi)
            return jnp.where(count(lambda kx: kx >= cand) >= top_k, cand, base)

        tau = lax.fori_loop(0, 31, bit_body, base)
        tau = jnp.where(qpos[0:1, :] + 1 <= top_k, _INT_MIN + 1, tau)
        tau_ref[...] = jnp.broadcast_to(tau, tau_ref.shape)

    tau = tau_ref[0:1, :]
    need = top_k - count(lambda kx: kx > tau)

    q = q_ref[...].astype(F32)
    q_parts = []
    for mpair in range(nh // 2):
        qp = q[:, mpair * LANE:(mpair + 1) * LANE]
        sq = qp * qp
        s_lo = jnp.sum(jnp.where(lo, sq, 0.0), axis=1, keepdims=True)
        s_hi = jnp.sum(jnp.where(lo, 0.0, sq), axis=1, keepdims=True)
        qn = qp * lax.rsqrt(jnp.where(lo, s_lo, s_hi) * (1.0 / DSA_HD) + EPS) * qg_ref[...] * (DSA_HD ** -0.5)
        for e, qh in enumerate((qn, pltpu.roll(qn, HALF, 1))):
            slope = 2.0 ** (-8.0 * (2 * mpair + e + 1) / nh)
            q_parts.append(jnp.where(lo, qh, jnp.where(pos_lanes_q, slope, 0.0)))
    q_st = jnp.concatenate(q_parts, axis=0).astype(BF16)

    def mask_chunk(jc, carry):
        run, mred = carry
        key = key_ref[jc]
        eq = key == tau
        eqf = jnp.where(eq, 1.0, 0.0)
        prefix = _dot(sl_ref[...], eqf.astype(BF16)) + run
        sel = (key > tau) | (eq & (prefix < need))
        lg = _dot_nt(kn_ref[jc], q_st)
        new = []
        for h in range(nh):
            x = jnp.where(sel, lg[:, h * TILE:(h + 1) * TILE], _NEG)
            lg_ref[jc, :, h * TILE:(h + 1) * TILE] = x
            new.append(jnp.maximum(mred[h], _fold8(x, jnp.maximum)))
        return run + jnp.sum(_fold8(eqf, jnp.add), axis=0, keepdims=True), tuple(new)

    _, mred = lax.fori_loop(0, 2 * n_ch2, mask_chunk,
                            (jnp.zeros((1, LANE), F32), tuple(jnp.full((8, LANE), _NEG, F32) for _ in range(nh))))
    m_row = [jnp.max(mh, axis=0, keepdims=True) for mh in mred]

    acc_ref[...] = jnp.zeros_like(acc_ref)

    def pv_pair(j2, lred):
        lred = list(lred)
        pv = None
        for jc in (2 * j2, 2 * j2 + 1):
            ps = []
            for h in range(nh):
                p = jnp.exp(lg_ref[jc, :, h * TILE:(h + 1) * TILE] - m_row[h])
                lred[h] = lred[h] + _fold8(p, jnp.add)
                ps.append(p.astype(BF16))
            term = _dot(vt_ref[jc], jnp.concatenate(ps, axis=1))
            pv = term if pv is None else pv + term
        acc_ref[...] += pv
        return tuple(lred)

    lred = lax.fori_loop(0, n_ch2, pv_pair, tuple(jnp.zeros((8, LANE), F32) for _ in range(nh)))

    heads = []
    for h in range(nh):
        l_row = jnp.sum(lred[h], axis=0, keepdims=True)
        heads.append((acc_ref[:, h * TILE:(h + 1) * TILE] / l_row).T)
    pairs = [jnp.where(lo, heads[2 * mp], heads[2 * mp + 1]) for mp in range(nh // 2)]
    o_ref[...] = (jnp.concatenate(pairs, axis=1) * _silu(gate_ref[...].astype(F32))).astype(o_ref.dtype)


def _dsa(p3, sm3, q_norm_g, k_norm_g):
    b, l, _ = p3.shape
    top_k = min(DSA_TOPK_MAX, l // 4)
    nb = l // TILE
    nc = l // KEY_CHUNK
    sl = np.tril(np.ones((KEY_CHUNK, KEY_CHUNK), np.float32), -1)
    dup = lambda g: jnp.concatenate([g, g]).reshape(1, LANE).astype(F32)
    consts = [dup(q_norm_g), dup(k_norm_g), jnp.asarray(sl, BF16)]
    qblk = lambda name: pl.BlockSpec((None, TILE, BRANCH_W), lambda i, j: (i, j, P_OFF[name] // BRANCH_W))
    seq = lambda name: pl.BlockSpec((None, l, LANE), lambda i, j: (i, 0, P_OFF[name] // LANE),
                                    pipeline_mode=pl.Buffered(1))
    chunks = lambda dt: pltpu.VMEM((nc, KEY_CHUNK, LANE), dt)
    return pl.pallas_call(
        functools.partial(_dsa_kernel, top_k=top_k),
        grid=(b, nb),
        in_specs=[qblk("c_q"), qblk("c_iq"),
                  pl.BlockSpec((None, TILE, LANE), lambda i, j: (i, j, 0)),
                  qblk("c_gate"), seq("c_kk"), seq("c_vv"), seq("c_ikik")]
        + [_full_spec(a.shape) for a in consts],
        out_specs=pl.BlockSpec((None, TILE, BRANCH_W), lambda i, j: (i, j, 0)),
        out_shape=jax.ShapeDtypeStruct((b, l, BRANCH_W), BF16),
        scratch_shapes=[chunks(BF16), pltpu.VMEM((nc, LANE, KEY_CHUNK), BF16), chunks(BF16), chunks(I32),
                        pltpu.VMEM((nc, KEY_CHUNK, DSA_HEADS * TILE), F32), pltpu.VMEM((8, LANE), I32),
                        pltpu.VMEM((LANE, DSA_HEADS * TILE), F32)],
        compiler_params=pltpu.CompilerParams(dimension_semantics=("arbitrary", "arbitrary"),
                                             vmem_limit_bytes=VMEM_LIMIT),
        name="dsa",
    )(p3, p3, sm3, p3, p3, p3, p3, *consts)


def _ssd_kernel(z_ref, xbc_ref, small_ref, cw_ref, cbias_ref, dtb_ref, apad_ref, dexp_ref, ng_ref, tril_ref,
                e64_ref, e128_ref, o_ref, st_ref, halo_ref, buf_ref):
    @pl.when(pl.program_id(1) == 0)
    def _():
        st_ref[...] = jnp.zeros_like(st_ref)
        halo_ref[...] = jnp.zeros_like(halo_ref)

    nhp = SSD_HEADS * SSD_HD
    gs = SSD_STATE
    act = _silu(_causal_conv(xbc_ref[...].astype(F32), cw_ref, halo_ref, buf_ref) + cbias_ref[...])
    xs = act[:, :nhp]
    bm = act[:, nhp:nhp + SSD_GROUPS * gs]
    cm = act[:, nhp + SSD_GROUPS * gs:]
    dt_full = _softplus(small_ref[...] + dtb_ref[...])
    cs_full = _dot(tril_ref[...], dt_full * apad_ref[...], HIGHEST)
    dt_exp = _dot(dt_full, e64_ref[...], HIGHEST)
    cs_exp = _dot(cs_full, e64_ref[...], HIGHEST)
    cs_lb = _dot(cs_full, e128_ref[...], HIGHEST)
    xdt = xs * dt_exp
    cs_last = cs_exp[TILE - 1:TILE, :]
    xds = (xdt * jnp.exp(cs_last - cs_exp)).astype(BF16)
    ecs = jnp.exp(cs_exp)
    xdt_b = xdt.astype(BF16)

    r = lax.broadcasted_iota(I32, (TILE, TILE), 0)
    c = lax.broadcasted_iota(I32, (TILE, TILE), 1)
    incl = r >= c
    lo = c < HALF
    hpg = SSD_HEADS // SSD_GROUPS
    gw = hpg * SSD_HD
    y_parts = []
    for g in range(SSD_GROUPS):
        bmg = bm[:, g * gs:(g + 1) * gs]
        cmg = cm[:, g * gs:(g + 1) * gs].astype(BF16)
        cbg = _dot_nt(cmg, bmg.astype(BF16))
        for mp in range(hpg // 2):
            pair = (g * hpg) // 2 + mp
            xpair = xdt_b[:, pair * LANE:(pair + 1) * LANE]
            ys = []
            for e in range(2):
                h = 2 * pair + e
                cb = cs_lb[:, h * LANE:(h + 1) * LANE]
                lm = jnp.where(incl, jnp.exp(jnp.where(incl, cb - cb.T, 0.0)), 0.0)
                ys.append(_dot((cbg * lm).astype(BF16), xpair))
            y_parts.append(jnp.where(lo, ys[0], ys[1]))
        st = st_ref[g]
        y_off = _dot(cmg, st.astype(BF16)) * ecs[:, g * gw:(g + 1) * gw]
        y_parts[-(hpg // 2):] = [yp + y_off[:, k * LANE:(k + 1) * LANE]
                                 for k, yp in enumerate(y_parts[-(hpg // 2):])]
        new = _dot(bmg.T.astype(BF16), xds[:, g * gw:(g + 1) * gw])
        st_ref[g] = jnp.exp(cs_last[:, g * gw:(g + 1) * gw]) * st + new
    y = jnp.concatenate(y_parts, axis=1) + xs * dexp_ref[...]
    yz = y * _silu(z_ref[...].astype(F32))
    o_ref[...] = _rms(yz, ng_ref[...]).astype(o_ref.dtype)


def _ssd(p3, sm3, conv_w, conv_b, a_log, dt_bias, d_skip, norm_g):
    b, l, _ = p3.shape
    consts = [conv_w.astype(F32), conv_b.reshape(1, -1).astype(F32), _pad_lanes(dt_bias, SM_DT),
              _pad_lanes(-jnp.exp(a_log.astype(F32)), SM_DT),
              jnp.repeat(d_skip.astype(F32), SSD_HD).reshape(1, -1), norm_g.reshape(1, -1).astype(F32),
              jnp.asarray(np.tril(np.ones((TILE, TILE), np.float32))),
              jnp.asarray(_np_lane_expand(SM_DT, SSD_HEADS, SSD_HD)), jnp.asarray(_np_lane_expand(SM_DT, SSD_HEADS, LANE))]
    return pl.pallas_call(
        _ssd_kernel,
        grid=(b, l // TILE),
        in_specs=[pl.BlockSpec((None, TILE, BRANCH_W), lambda i, j: (i, j, P_OFF["d_z"] // BRANCH_W)),
                  pl.BlockSpec((None, TILE, SSD_XBC), lambda i, j: (i, j, P_OFF["d_xbc"] // SSD_XBC)),
                  pl.BlockSpec((None, TILE, LANE), lambda i, j: (i, j, 0))]
        + [_full_spec(a.shape) for a in consts],
        out_specs=pl.BlockSpec((None, TILE, BRANCH_W), lambda i, j: (i, j, 0)),
        out_shape=jax.ShapeDtypeStruct((b, l, BRANCH_W), BF16),
        scratch_shapes=[pltpu.VMEM((SSD_GROUPS, SSD_STATE, (SSD_HEADS // SSD_GROUPS) * SSD_HD), F32),
                        pltpu.VMEM((8, SSD_XBC), F32), pltpu.VMEM((TILE + 8, SSD_XBC), F32)],
        compiler_params=pltpu.CompilerParams(dimension_semantics=("arbitrary", "arbitrary"),
                                             vmem_limit_bytes=VMEM_LIMIT),
        name="ssd",
    )(p3, p3, sm3, *consts)


def _mem_kv_kernel(mem_ref, g_ref, w_ref, kg_ref, k_ref, v_ref):
    kv = _dot(_rms(mem_ref[...], g_ref[...]).astype(BF16), w_ref[...])
    for h in range(MEM_HEADS):
        hs = slice(h * MEM_HD, (h + 1) * MEM_HD)
        k_ref[:, hs] = _rms(kv[:, hs], kg_ref[...]).astype(k_ref.dtype)
    v_ref[...] = kv[:, BRANCH_W:].astype(v_ref.dtype)


def _mem_kv(mem, mem_norm_g, w_kv_bf16, k_norm_g):
    b, m, _ = mem.shape
    consts = [mem_norm_g.reshape(1, -1).astype(F32), w_kv_bf16, k_norm_g.reshape(1, -1).astype(F32)]
    out = jax.ShapeDtypeStruct((b, m, BRANCH_W), BF16)
    return pl.pallas_call(
        _mem_kv_kernel,
        grid=(b,),
        in_specs=[pl.BlockSpec((None, m, D_MODEL), lambda i: (i, 0, 0))] + [_full_spec(a.shape) for a in consts],
        out_specs=[pl.BlockSpec((None, m, BRANCH_W), lambda i: (i, 0, 0))] * 2,
        out_shape=[out, out],
        compiler_params=pltpu.CompilerParams(dimension_semantics=("arbitrary",), vmem_limit_bytes=VMEM_LIMIT),
        name="mem_kv",
    )(mem, *consts)


def _mem_attn_kernel(q_ref, gate_ref, k_ref, v_ref, qg_ref, o_ref):
    q = q_ref[...].astype(F32)
    outs = []
    for h in range(MEM_HEADS):
        hs = slice(h * MEM_HD, (h + 1) * MEM_HD)
        qn = _rms(q[:, hs], qg_ref[...]).astype(BF16)
        lg = _dot_nt(qn, k_ref[:, hs]) * (MEM_HD ** -0.5)
        p = jnp.exp(lg - jnp.max(lg, axis=1, keepdims=True))
        outs.append(_dot(p.astype(BF16), v_ref[:, hs]) / jnp.sum(p, axis=1, keepdims=True))
    o_ref[...] = (jnp.concatenate(outs, axis=1) * _silu(gate_ref[...].astype(F32))).astype(o_ref.dtype)


def _mem_attn(p3, k, v, q_norm_g):
    b, l, _ = p3.shape
    m = k.shape[1]
    tm = min(512, l)
    qg = q_norm_g.reshape(1, -1).astype(F32)
    blk = lambda name: pl.BlockSpec((None, tm, BRANCH_W), lambda i, j: (i, j, P_OFF[name] // BRANCH_W))
    kv = pl.BlockSpec((None, m, BRANCH_W), lambda i, j: (i, 0, 0))
    return pl.pallas_call(
        _mem_attn_kernel,
        grid=(b, l // tm),
        in_specs=[blk("m_q"), blk("m_gate"), kv, kv, _full_spec(qg.shape)],
        out_specs=pl.BlockSpec((None, tm, BRANCH_W), lambda i, j: (i, j, 0)),
        out_shape=jax.ShapeDtypeStruct((b, l, BRANCH_W), BF16),
        compiler_params=pltpu.CompilerParams(dimension_semantics=("arbitrary", "arbitrary"),
                                             vmem_limit_bytes=VMEM_LIMIT),
        name="mem_attn",
    )(p3, p3, k, v, qg)


def _reordered_w_in(w):
    cols = jnp.asarray(np.maximum(_P_COLUMNS, 0))
    keep = jnp.asarray((_P_COLUMNS >= 0).astype(np.float32))
    return (jnp.take(w, cols, axis=1) * keep).astype(BF16)


def kernel(x, mem, norm_g, w_in, gdn_conv_w, gdn_a_log, gdn_dt_bias, gdn_norm_g, sg_ln_g, sg_ln_b, sg_w, sg_b, dsa_q_norm_g, dsa_k_norm_g, ssd_conv_w, ssd_conv_b, ssd_a_log, ssd_dt_bias, ssd_d, ssd_norm_g, mem_norm_g, w_mem_kv, mem_q_norm_g, mem_k_norm_g, w_gate, w_branch, w_out):
    b, l, d = x.shape
    depth = norm_g.shape[0]
    x2d = x.reshape(b * l, d)
    for i in range(depth):
        g = norm_g[i].reshape(1, d).astype(F32)
        p2d, sm2d = _in_proj(x2d, g, _reordered_w_in(w_in[i]))
        p3 = p2d.reshape(b, l, P_MAIN)
        sm3 = sm2d.reshape(b, l, LANE)
        mk, mv = _mem_kv(mem, mem_norm_g[i], w_mem_kv[i].astype(BF16), mem_k_norm_g[i])
        ys = (
            _gdn(p3, sm3, gdn_conv_w[i], gdn_a_log[i], gdn_dt_bias[i], gdn_norm_g[i]),
            _sg(p3, sg_ln_g[i], sg_ln_b[i], sg_w[i], sg_b[i]),
            _dsa(p3, sm3, dsa_q_norm_g[i], dsa_k_norm_g[i]),
            _ssd(p3, sm3, ssd_conv_w[i], ssd_conv_b[i], ssd_a_log[i], ssd_dt_bias[i], ssd_d[i], ssd_norm_g[i]),
            _mem_attn(p3, mk, mv, mem_q_norm_g[i]),
        )
        ys = [y.reshape(b * l, BRANCH_W) for y in ys]
        x2d = _merge(x2d, g, ys, w_gate[i].astype(BF16), w_branch[i].astype(BF16), w_out[i].astype(BF16))
    return x2d.reshape(b, l, d)
```

```python
import functools

import numpy as np
import jax
import jax.numpy as jnp
from jax import lax
from jax.experimental import pallas as pl
from jax.experimental.pallas import tpu as pltpu

F32 = jnp.float32
BF16 = jnp.bfloat16
I32 = jnp.int32
I16 = jnp.int16

D_MODEL = 1024
N_BRANCH = 5
BRANCH_W = D_MODEL // 2
CONV_K = 4
EPS = 1e-6
GDN_HEADS = 4
GDN_DK = BRANCH_W // GDN_HEADS
GDN_CHUNK = 64
GDN_QKV = 3 * BRANCH_W
SG_CHUNK = 128
SG_GROUPS = 4
DSA_HEADS = 8
DSA_HD = BRANCH_W // DSA_HEADS
IDX_HEADS = 8
IDX_HD = 64
DSA_TOPK_MAX = 256
Q_BLOCK = 128
SSD_HEADS = 8
SSD_HD = BRANCH_W // SSD_HEADS
SSD_GROUPS = 2
SSD_STATE = 128
SSD_CHUNK = 128
SSD_XBC = SSD_HEADS * SSD_HD + 2 * SSD_GROUPS * SSD_STATE
MEM_HEADS = 4
MEM_HD = BRANCH_W // MEM_HEADS

IN_SPLITS = (
    GDN_QKV, GDN_HEADS, GDN_HEADS, BRANCH_W,
    BRANCH_W, BRANCH_W, BRANCH_W,
    DSA_HEADS * DSA_HD, DSA_HD, DSA_HD, IDX_HEADS * IDX_HD, IDX_HD, IDX_HEADS, BRANCH_W,
    BRANCH_W, SSD_XBC, SSD_HEADS,
    MEM_HEADS * MEM_HD, BRANCH_W,
)
_SEG_NAMES = ("a_qkv", "a_a", "a_b", "a_gate", "b_u", "b_v", "b_gate", "c_q", "c_k", "c_v", "c_iq", "c_ik",
              "c_iw", "c_gate", "d_z", "d_xbc", "d_dt", "m_q", "m_gate")
_SEG_START = dict(zip(_SEG_NAMES, np.concatenate([[0], np.cumsum(IN_SPLITS)[:-1]]).tolist()))
_SEG_WIDTH = dict(zip(_SEG_NAMES, IN_SPLITS))

LANE = 128
HALF = LANE // 2
TILE = 128
VMEM_LIMIT = 56 * 1024 * 1024

_P_LAYOUT = (("a_qkv", 1536), ("b_u", 512), ("d_xbc", 1024), ("a_gate", 512), ("b_v", 512), ("b_gate", 512),
             ("c_q", 512), ("c_iq", 512), ("c_gate", 512), ("d_z", 512), ("m_q", 512), ("m_gate", 512),
             ("c_kk", 128), ("c_vv", 128), ("c_ikik", 128), ("small", 128))
P_OFF = {}
_o = 0
for _n, _w in _P_LAYOUT:
    P_OFF[_n] = _o
    _o += _w
P_COLS = _o
P_MAIN = P_OFF["small"]
SM_A, SM_B, SM_DT, SM_IW = 0, 4, 8, 16


def _p_columns():
    cols = []
    for name, width in _P_LAYOUT:
        if name in _SEG_START:
            cols += list(range(_SEG_START[name], _SEG_START[name] + width))
        elif name in ("c_kk", "c_vv", "c_ikik"):
            src = {"c_kk": "c_k", "c_vv": "c_v", "c_ikik": "c_ik"}[name]
            one = list(range(_SEG_START[src], _SEG_START[src] + _SEG_WIDTH[src]))
            cols += one + one
        else:
            sm = [-1] * LANE
            for seg, at in (("a_a", SM_A), ("a_b", SM_B), ("d_dt", SM_DT), ("c_iw", SM_IW)):
                for j in range(_SEG_WIDTH[seg]):
                    sm[at + j] = _SEG_START[seg] + j
            cols += sm
    return np.asarray(cols, np.int32)


_P_COLUMNS = _p_columns()


def _dot(a, b, precision=None):
    return jnp.dot(a, b, preferred_element_type=F32, precision=precision)


def _dot_nt(a, b):
    return lax.dot_general(a, b, (((1,), (1,)), ((), ())), preferred_element_type=F32)


def _split3(x):
    hi = x.astype(BF16)
    r1 = x - hi.astype(F32)
    mid = r1.astype(BF16)
    lo = (r1 - mid.astype(F32)).astype(BF16)
    return hi, mid, lo


def _dot01_r(x, e):
    hi, mid, lo = _split3(x)
    return _dot(hi, e) + _dot(mid, e) + _dot(lo, e)


def _dot01_l(e, x):
    hi, mid, lo = _split3(x)
    return _dot(e, hi) + _dot(e, mid) + _dot(e, lo)


def _sigmoid(x):
    return 1.0 / (1.0 + jnp.exp(-x))


def _silu(x):
    return x * _sigmoid(x)


def _softplus(x):
    return jnp.maximum(x, 0.0) + jnp.log1p(jnp.exp(-jnp.abs(x)))


def _gelu_tanh(x):
    return 0.5 * x * (1.0 + jnp.tanh(np.sqrt(2.0 / np.pi).astype(np.float32) * (x + 0.044715 * (x * x * x))))


def _rms(x, g, eps=EPS):
    return x * lax.rsqrt(jnp.mean(x * x, axis=-1, keepdims=True) + eps) * g


def _full_spec(shape):
    nd = len(shape)
    return pl.BlockSpec(shape, lambda *_: (0,) * nd)


def _in_proj_kernel(x_ref, g_ref, w_ref, o_ref, s_ref):
    h = _rms(x_ref[...], g_ref[...]).astype(BF16)
    step = 1024
    for c0 in range(0, P_MAIN, step):
        c1 = min(c0 + step, P_MAIN)
        o_ref[:, c0:c1] = _dot(h, w_ref[:, c0:c1]).astype(o_ref.dtype)
    s_ref[...] = _dot(h, w_ref[:, P_MAIN:])


def _in_proj(x2d, g, w_bf16):
    m = x2d.shape[0]
    tm = min(512, m)
    return pl.pallas_call(
        _in_proj_kernel,
        grid=(m // tm,),
        in_specs=[pl.BlockSpec((tm, D_MODEL), lambda i: (i, 0)),
                  pl.BlockSpec((1, D_MODEL), lambda i: (0, 0)),
                  pl.BlockSpec((D_MODEL, P_COLS), lambda i: (0, 0), pipeline_mode=pl.Buffered(1))],
        out_specs=[pl.BlockSpec((tm, P_MAIN), lambda i: (i, 0)), pl.BlockSpec((tm, LANE), lambda i: (i, 0))],
        out_shape=[jax.ShapeDtypeStruct((m, P_MAIN), BF16), jax.ShapeDtypeStruct((m, LANE), F32)],
        compiler_params=pltpu.CompilerParams(dimension_semantics=("arbitrary",), vmem_limit_bytes=VMEM_LIMIT),
        name="in_proj",
    )(x2d, g, w_bf16)


def _merge_kernel(x_ref, g_ref, y0, y1, y2, y3, y4, wg_ref, wb_ref, wo_ref, o_ref):
    x = x_ref[...]
    h = _rms(x, g_ref[...]).astype(BF16)
    acc = None
    for p, y in enumerate((y0, y1, y2, y3, y4)):
        term = _sigmoid(_dot(h, wg_ref[p])) * _dot(y[...], wb_ref[p])
        acc = term if acc is None else acc + term
    o_ref[...] = x + _dot(acc.astype(BF16), wo_ref[...])


def _merge(x2d, g, ys, wg, wb, wo):
    m = x2d.shape[0]
    tm = min(512, m)
    row = lambda i: (i, 0)
    const = pl.Buffered(1)
    return pl.pallas_call(
        _merge_kernel,
        grid=(m // tm,),
        in_specs=[pl.BlockSpec((tm, D_MODEL), row), pl.BlockSpec((1, D_MODEL), lambda i: (0, 0))]
        + [pl.BlockSpec((tm, BRANCH_W), row)] * N_BRANCH
        + [pl.BlockSpec((N_BRANCH, D_MODEL, D_MODEL), lambda i: (0, 0, 0), pipeline_mode=const),
           pl.BlockSpec((N_BRANCH, BRANCH_W, D_MODEL), lambda i: (0, 0, 0), pipeline_mode=const),
           pl.BlockSpec((D_MODEL, D_MODEL), lambda i: (0, 0), pipeline_mode=const)],
        out_specs=pl.BlockSpec((tm, D_MODEL), row),
        out_shape=jax.ShapeDtypeStruct((m, D_MODEL), F32),
        compiler_params=pltpu.CompilerParams(dimension_semantics=("arbitrary",), vmem_limit_bytes=VMEM_LIMIT),
        name="merge_out",
    )(x2d, g, *ys, wg, wb, wo)


def _causal_conv(x, w_ref, halo_ref, buf_ref):
    buf_ref[0:8, :] = halo_ref[...]
    buf_ref[8:8 + TILE, :] = x
    y = w_ref[CONV_K - 1:CONV_K, :] * x
    for j in range(CONV_K - 1):
        y = y + w_ref[j:j + 1, :] * buf_ref[8 - (CONV_K - 1) + j:8 - (CONV_K - 1) + j + TILE, :]
    halo_ref[...] = x[TILE - 8:TILE, :]
    return y


def _np_lane_expand(first_row, n_heads, width):
    e = np.zeros((LANE, n_heads * width), np.float32)
    for h in range(n_heads):
        e[first_row + h, h * width:(h + 1) * width] = 1.0
    return e


def _pad_lanes(vec, at):
    out = jnp.zeros((1, LANE), F32)
    return lax.dynamic_update_slice(out, vec.reshape(1, -1).astype(F32), (0, at))


def _gdn_kernel(qkv_ref, small_ref, gate_ref, cw_ref, apad_ref, dtb_ref, ng_ref, btril_ref, eg_ref, eb_ref,
                o_ref, st_ref, halo_ref, buf_ref):
    @pl.when(pl.program_id(1) == 0)
    def _():
        st_ref[...] = jnp.zeros_like(st_ref)
        halo_ref[...] = jnp.zeros_like(halo_ref)

    act = _silu(_causal_conv(qkv_ref[...].astype(F32), cw_ref, halo_ref, buf_ref))
    small = small_ref[...]
    g_full = -apad_ref[...] * _softplus(small + dtb_ref[...])
    beta_full = _sigmoid(small)
    gc_full = _dot01_l(btril_ref[...], g_full)
    g_lb = _dot01_r(gc_full, eg_ref[...])
    b_lb = _dot01_r(beta_full, eb_ref[...])

    r = lax.broadcasted_iota(I32, (TILE, TILE), 0)
    c = lax.broadcasted_iota(I32, (TILE, TILE), 1)
    same = (r >> 6) == (c >> 6)
    incl = same & (r >= c)
    strict = same & (r > c)
    eye = (r == c).astype(F32)
    n_sub = TILE // GDN_CHUNK
    ng = ng_ref[...]
    heads = range(GDN_HEADS)
    q, k, v, cb, bb = [], [], [], [], []
    for h in heads:
        qh = act[:, h * GDN_DK:(h + 1) * GDN_DK]
        kh = act[:, BRANCH_W + h * GDN_DK:BRANCH_W + (h + 1) * GDN_DK]
        q.append(qh * lax.rsqrt(jnp.sum(qh * qh, axis=-1, keepdims=True) + EPS) * (GDN_DK ** -0.5))
        k.append(kh * lax.rsqrt(jnp.sum(kh * kh, axis=-1, keepdims=True) + EPS))
        v.append(act[:, 2 * BRANCH_W + h * GDN_DK:2 * BRANCH_W + (h + 1) * GDN_DK])
        cb.append(g_lb[:, h * LANE:(h + 1) * LANE])
        bb.append(b_lb[:, h * LANE:(h + 1) * LANE])
    dm = [jnp.where(incl, jnp.exp(jnp.where(incl, cb[h] - cb[h].T, 0.0)), 0.0) for h in heads]
    kb = [k[h].astype(BF16) for h in heads]
    kk = [_dot_nt(kb[h], kb[h]) for h in heads]
    qkb = [(_dot_nt(q[h].astype(BF16), kb[h]) * dm[h]).astype(BF16) for h in heads]
    mp = [jnp.where(strict, -(bb[h] * kk[h] * dm[h]), 0.0) for h in heads]
    inv = [eye + mp[h] for h in heads]
    for _ in range(int(np.log2(GDN_CHUNK)) - 1):
        mpb = [mp[h].astype(BF16) for h in heads]
        mp = [_dot(mpb[h], mpb[h]) for h in heads]
        inv = [inv[h] + _dot(inv[h].astype(BF16), mp[h].astype(BF16)) for h in heads]
    eg = [jnp.exp(cb[h]) for h in heads]
    sol = [_dot(inv[h].astype(BF16),
                jnp.concatenate([bb[h] * eg[h] * k[h], bb[h] * v[h]], axis=1).astype(BF16)) for h in heads]
    wk = [sol[h][:, :GDN_DK].astype(BF16) for h in heads]
    u0 = [sol[h][:, GDN_DK:] for h in heads]
    qd = [(q[h] * eg[h]).astype(BF16) for h in heads]
    zero = jnp.zeros((GDN_CHUNK, LANE), F32)
    st = [st_ref[h] for h in heads]
    o_parts = [[] for _ in heads]
    for cc in range(n_sub):
        rows = slice(cc * GDN_CHUNK, (cc + 1) * GDN_CHUNK)
        in_chunk = (r >> 6) == cc
        glast = [cb[h][(cc + 1) * GDN_CHUNK - 1:(cc + 1) * GDN_CHUNK, :] for h in heads]
        stb = [st[h].astype(BF16) for h in heads]
        u = [u0[h][rows] - _dot(wk[h][rows], stb[h]) for h in heads]
        upad = [jnp.concatenate([u[h] if j == cc else zero for j in range(n_sub)], axis=0).astype(BF16)
                for h in heads]
        for h in heads:
            o_parts[h].append(_dot(qd[h][rows], stb[h]) + _dot(qkb[h][rows], upad[h]))
        kd = [jnp.where(in_chunk, k[h] * jnp.exp(jnp.where(in_chunk, glast[h] - cb[h], 0.0)), 0.0) for h in heads]
        st = [jnp.exp(glast[h]) * st[h] + _dot(kd[h].T.astype(BF16), upad[h]) for h in heads]
    for h in heads:
        st_ref[h] = st[h]
    outs = [_rms(jnp.concatenate(o_parts[h], axis=0), ng) for h in heads]
    y = jnp.concatenate(outs, axis=1) * _silu(gate_ref[...].astype(F32))
    o_ref[...] = y.astype(o_ref.dtype)


def _gdn(p3, sm3, conv_w, a_log, dt_bias, norm_g):
    b, l, _ = p3.shape
    btril = np.zeros((TILE, TILE), np.float32)
    for i in range(TILE):
        lo = (i // GDN_CHUNK) * GDN_CHUNK
        btril[i, lo:i + 1] = 1.0
    consts = [conv_w.astype(F32), _pad_lanes(jnp.exp(a_log.astype(F32)), SM_A), _pad_lanes(dt_bias, SM_A),
              norm_g.reshape(1, -1).astype(F32), jnp.asarray(btril, BF16),
              jnp.asarray(_np_lane_expand(SM_A, GDN_HEADS, LANE), BF16),
              jnp.asarray(_np_lane_expand(SM_B, GDN_HEADS, LANE), BF16)]
    return pl.pallas_call(
        _gdn_kernel,
        grid=(b, l // TILE),
        in_specs=[pl.BlockSpec((None, TILE, GDN_QKV), lambda i, j: (i, j, P_OFF["a_qkv"] // GDN_QKV)),
                  pl.BlockSpec((None, TILE, LANE), lambda i, j: (i, j, 0)),
                  pl.BlockSpec((None, TILE, BRANCH_W), lambda i, j: (i, j, P_OFF["a_gate"] // BRANCH_W))]
        + [_full_spec(a.shape) for a in consts],
        out_specs=pl.BlockSpec((None, TILE, BRANCH_W), lambda i, j: (i, j, 0)),
        out_shape=jax.ShapeDtypeStruct((b, l, BRANCH_W), BF16),
        scratch_shapes=[pltpu.VMEM((GDN_HEADS, GDN_DK, GDN_DK), F32), pltpu.VMEM((8, GDN_QKV), F32),
                        pltpu.VMEM((TILE + 8, GDN_QKV), F32)],
        compiler_params=pltpu.CompilerParams(dimension_semantics=("arbitrary", "arbitrary"),
                                             vmem_limit_bytes=VMEM_LIMIT),
        name="gdn",
    )(p3, sm3, p3, *consts)


def _sg_kernel(u_ref, v_ref, gate_ref, lng_ref, lnb_ref, w_ref, bs_ref, o_ref):
    u = _gelu_tanh(u_ref[...].astype(F32))
    v = _gelu_tanh(v_ref[...].astype(F32))
    mu = jnp.mean(v, axis=-1, keepdims=True)
    var = jnp.mean(jnp.square(v - mu), axis=-1, keepdims=True)
    vn = ((v - mu) * lax.rsqrt(var + 1e-5) * lng_ref[...] + lnb_ref[...]).astype(BF16)
    r = lax.broadcasted_iota(I32, (SG_CHUNK, SG_CHUNK), 0)
    c = lax.broadcasted_iota(I32, (SG_CHUNK, SG_CHUNK), 1)
    gw = BRANCH_W // SG_GROUPS
    mixed = []
    for g in range(SG_GROUPS):
        wc = jnp.where(r >= c, w_ref[g], 0.0).astype(BF16)
        mixed.append(_dot(wc, vn[:, g * gw:(g + 1) * gw]))
    mixed = jnp.concatenate(mixed, axis=1) + bs_ref[...]
    o_ref[...] = (u * mixed * _silu(gate_ref[...].astype(F32))).astype(o_ref.dtype)


def _sg(p3, ln_g, ln_b, w_s, b_s):
    b, l, _ = p3.shape
    gw = BRANCH_W // SG_GROUPS
    bs_full = jnp.repeat(jnp.swapaxes(b_s, 0, 1).astype(F32), gw, axis=1)
    consts = [ln_g.reshape(1, -1).astype(F32), ln_b.reshape(1, -1).astype(F32), w_s.astype(F32), bs_full]
    blk = lambda name: pl.BlockSpec((None, SG_CHUNK, BRANCH_W), lambda i, j: (i, j, P_OFF[name] // BRANCH_W))
    return pl.pallas_call(
        _sg_kernel,
        grid=(b, l // SG_CHUNK),
        in_specs=[blk("b_u"), blk("b_v"), blk("b_gate")] + [_full_spec(a.shape) for a in consts],
        out_specs=pl.BlockSpec((None, SG_CHUNK, BRANCH_W), lambda i, j: (i, j, 0)),
        out_shape=jax.ShapeDtypeStruct((b, l, BRANCH_W), BF16),
        compiler_params=pltpu.CompilerParams(dimension_semantics=("arbitrary", "arbitrary"),
                                             vmem_limit_bytes=VMEM_LIMIT),
        name="spatial_gating",
    )(p3, p3, p3, *consts)


_NEG = -1e30
_INT_MIN = -2 ** 31
_I16_MIN = -2 ** 15
_I16_ROWS = 16


def _fold8(x, op):
    parts = [x[r:r + 8] for r in range(0, x.shape[0], 8)]
    while len(parts) > 1:
        parts = [op(parts[k], parts[k + 1]) for k in range(0, len(parts) - 1, 2)] + (
            [parts[-1]] if len(parts) % 2 else [])
    return parts[0]


KEY_CHUNK = 2 * TILE


def _dsa_kernel(q_ref, iq_ref, small_ref, gate_ref, kk_ref, vv_ref, ik_ref, qg_ref, kg_ref, sl_ref,
                o_ref, kn_ref, vt_ref, ikb_ref, key_ref, hi_ref, lo_ref, lg_ref, tau_ref, acc_ref, *, top_k):
    i = pl.program_id(1)
    nh = DSA_HEADS
    nc = key_ref.shape[0]
    kc = KEY_CHUNK

    srow = lax.broadcasted_iota(I32, (kc, LANE), 0)
    lane = lax.broadcasted_iota(I32, (kc, LANE), 1)
    qpos = i * TILE + lane
    pos_lanes = (lane == HALF) | (lane == HALF + 1)

    @pl.when(i == 0)
    def _():
        def prep(jc, carry):
            rows = pl.ds(pl.multiple_of(jc * kc, kc), kc)
            kpos = jnp.where(lane == HALF, srow, jc * kc).astype(F32)
            kn = jnp.where(lane < HALF, _rms(kk_ref[rows, :].astype(F32), kg_ref[...]), jnp.where(pos_lanes, kpos, 0.0))
            kn_ref[jc] = kn.astype(BF16)
            v = vv_ref[rows, :].astype(F32)
            vt_ref[jc] = jnp.concatenate([v[r:r + TILE].T for r in range(0, kc, TILE)], axis=1).astype(BF16)
            ikb_ref[jc] = ik_ref[rows, :]
            return carry
        lax.fori_loop(0, nc, prep, 0)

    n_ch = (i + 2) // 2
    n_ch2 = (n_ch + 1) // 2

    @pl.when(n_ch < nc)
    def _():
        key_ref[n_ch] = jnp.full((kc, LANE), _INT_MIN, I32)
        hi_ref[n_ch] = jnp.full((kc, LANE), _I16_MIN, I16)
        lo_ref[n_ch] = jnp.full((kc, LANE), _I16_MIN, I16)

    iq = iq_ref[...].astype(F32)
    lane_q = lax.broadcasted_iota(I32, (TILE, LANE), 1)
    lo = lane_q < HALF
    pos_lanes_q = (lane_q == HALF) | (lane_q == HALF + 1)
    iq_st = jnp.concatenate(
        [jnp.where(lo if h % 2 == 0 else ~lo, iq[:, (h // 2) * LANE:(h // 2 + 1) * LANE], 0.0) for h in range(nh)],
        axis=0).astype(BF16)
    small_t = small_ref[...].T * (IDX_HEADS ** -0.5 * IDX_HD ** -0.5)
    iw_row = jnp.concatenate([small_t[SM_IW + h:SM_IW + h + 1, :] for h in range(nh)], axis=1)

    def score_pair(j2, carry):
        jcs = (2 * j2, 2 * j2 + 1)
        raw = [_dot_nt(ikb_ref[jc], iq_st) for jc in jcs]
        for jc, s_raw in zip(jcs, raw):
            s_h = jnp.maximum(s_raw, 0.0) * iw_row
            s = s_h[:, 0:TILE]
            for h in range(1, nh):
                s = s + s_h[:, h * TILE:(h + 1) * TILE]
            bits = pltpu.bitcast(s, I32)
            key = jnp.where(bits < 0, bits ^ 0x7FFFFFFF, bits)
            key = jnp.where(jc * kc + srow <= qpos, key, _INT_MIN)
            key_ref[jc] = key
            hi_ref[jc] = lax.shift_right_arithmetic(key, 16).astype(I16)
            lo_ref[jc] = ((key & 0xFFFF) + _I16_MIN).astype(I16)
        return carry

    lax.fori_loop(0, n_ch2, score_pair, 0)

    def count(pred):
        def body(j2, acc):
            for u in range(2):
                acc = acc + _fold8(jnp.where(pred(key_ref[2 * j2 + u]), 1.0, 0.0), jnp.add)
            return acc
        acc = lax.fori_loop(0, n_ch2, body, jnp.zeros((8, LANE), F32))
        return jnp.sum(acc, axis=0, keepdims=True)

    one16 = jnp.ones((_I16_ROWS, LANE), I16)
    zero16 = jnp.zeros((_I16_ROWS, LANE), I16)

    def tile16(row):
        return jnp.broadcast_to(row, (_I16_ROWS, LANE)).astype(I16)

    def count16(ref, cmp, thr_row):
        thr = tile16(thr_row)

        def body(j2, acc):
            for u in range(2):
                x = ref[2 * j2 + u]
                ones = [jnp.where(cmp(x[r:r + _I16_ROWS], thr), one16, zero16) for r in range(0, kc, _I16_ROWS)]
                while len(ones) > 1:
                    ones = [ones[k] + ones[k + 1] for k in range(0, len(ones), 2)]
                acc = acc + ones[0]
            return acc
        acc = lax.fori_loop(0, n_ch2, body, zero16)
        return jnp.sum(acc.astype(I32), axis=0, keepdims=True)

    ge = lambda a, b: a >= b
    tau_ref[...] = jnp.full(tau_ref.shape, _INT_MIN + 1, I32)

    @pl.when((i + 1) * TILE > top_k)
    def _():
        zero = jnp.zeros((1, LANE), I32)
        base = jnp.where(count16(hi_ref, ge, zero) >= top_k, zero, _I16_MIN)

        def hi_body(bi, base):
            cand = base | lax.shift_left(jnp.int32(1), 14 - bi)
            return jnp.where(count16(hi_ref, ge, cand) >= top_k, cand, base)

        hi_tau = lax.fori_loop(0, 15, hi_body, base)
        left = top_k - count16(hi_ref, lambda a, b: a > b, hi_tau)
        hi_tile = tile16(hi_tau)

        def keep_group(jc, carry):
            hi = hi_ref[jc]
            lo16 = lo_ref[jc]
            lo_ref[jc] = jnp.concatenate(
                [jnp.where(hi[r:r + _I16_ROWS] == hi_tile, lo16[r:r + _I16_ROWS], jnp.full_like(one16, _I16_MIN))
                 for r in range(0, kc, _I16_ROWS)], axis=0)
            return carry

        lax.fori_loop(0, 2 * n_ch2, keep_group, 0)

        def lo_body(bi, base_u):
            cand_u = base_u | lax.shift_left(jnp.int32(1), 15 - bi)
            return jnp.where(count16(lo_ref, ge, cand_u + _I16_MIN) >= left, cand_u, base_u)

        lo_tau = lax.fori_loop(0, 16, lo_body, zero)
        tau = lax.shift_left(hi_tau, 16) | lo_tau
        tau = jnp.where(qpos[0:1, :] + 1 <= top_k, _INT_MIN + 1, tau)
        tau_ref[...] = jnp.broadcast_to(tau, tau_ref.shape)

    tau = tau_ref[0:1, :]
    need = top_k - count(lambda kx: kx > tau)

    q = q_ref[...].astype(F32)
    q_parts = []
    for mpair in range(nh // 2):
        qp = q[:, mpair * LANE:(mpair + 1) * LANE]
        sq = qp * qp
        s_lo = jnp.sum(jnp.where(lo, sq, 0.0), axis=1, keepdims=True)
        s_hi = jnp.sum(jnp.where(lo, 0.0, sq), axis=1, keepdims=True)
        qn = qp * lax.rsqrt(jnp.where(lo, s_lo, s_hi) * (1.0 / DSA_HD) + EPS) * qg_ref[...] * (DSA_HD ** -0.5)
        for e, qh in enumerate((qn, pltpu.roll(qn, HALF, 1))):
            slope = 2.0 ** (-8.0 * (2 * mpair + e + 1) / nh)
            q_parts.append(jnp.where(lo, qh, jnp.where(pos_lanes_q, slope, 0.0)))
    q_st = jnp.concatenate(q_parts, axis=0).astype(BF16)

    def mask_chunk(jc, carry):
        run, mred = carry
        key = key_ref[jc]
        eq = key == tau
        eqf = jnp.where(eq, 1.0, 0.0)
        prefix = _dot(sl_ref[...], eqf.astype(BF16)) + run
        sel = (key > tau) | (eq & (prefix < need))
        lg = _dot_nt(kn_ref[jc], q_st)
        new = []
        for h in range(nh):
            x = jnp.where(sel, lg[:, h * TILE:(h + 1) * TILE], _NEG)
            lg_ref[jc, :, h * TILE:(h + 1) * TILE] = x
            new.append(jnp.maximum(mred[h], _fold8(x, jnp.maximum)))
        return run + jnp.sum(_fold8(eqf, jnp.add), axis=0, keepdims=True), tuple(new)

    def mask_chunk_no_ties(jc, mred):
        sel = key_ref[jc] >= tau
        lg = _dot_nt(kn_ref[jc], q_st)
        new = []
        for h in range(nh):
            x = jnp.where(sel, lg[:, h * TILE:(h + 1) * TILE], _NEG)
            lg_ref[jc, :, h * TILE:(h + 1) * TILE] = x
            new.append(jnp.maximum(mred[h], _fold8(x, jnp.maximum)))
        return tuple(new)

    mred0 = tuple(jnp.full((8, LANE), _NEG, F32) for _ in range(nh))
    no_ties = jnp.all((count(lambda kx: kx >= tau) == top_k) | (qpos[0:1, :] + 1 <= top_k))
    mred = lax.cond(
        no_ties,
        lambda: lax.fori_loop(0, 2 * n_ch2, mask_chunk_no_ties, mred0),
        lambda: lax.fori_loop(0, 2 * n_ch2, mask_chunk, (jnp.zeros((1, LANE), F32), mred0))[1])
    m_row = [jnp.max(mh, axis=0, keepdims=True) for mh in mred]

    acc_ref[...] = jnp.zeros_like(acc_ref)

    def pv_pair(j2, lred):
        lred = list(lred)
        pv = None
        for jc in (2 * j2, 2 * j2 + 1):
            ps = []
            for h in range(nh):
                p = jnp.exp(lg_ref[jc, :, h * TILE:(h + 1) * TILE] - m_row[h])
                lred[h] = lred[h] + _fold8(p, jnp.add)
                ps.append(p.astype(BF16))
            term = _dot(vt_ref[jc], jnp.concatenate(ps, axis=1))
            pv = term if pv is None else pv + term
        acc_ref[...] += pv
        return tuple(lred)

    lred = lax.fori_loop(0, n_ch2, pv_pair, tuple(jnp.zeros((8, LANE), F32) for _ in range(nh)))

    heads = []
    for h in range(nh):
        l_row = jnp.sum(lred[h], axis=0, keepdims=True)
        heads.append((acc_ref[:, h * TILE:(h + 1) * TILE] / l_row).T)
    pairs = [jnp.where(lo, heads[2 * mp], heads[2 * mp + 1]) for mp in range(nh // 2)]
    o_ref[...] = (jnp.concatenate(pairs, axis=1) * _silu(gate_ref[...].astype(F32))).astype(o_ref.dtype)


def _dsa(p3, sm3, q_norm_g, k_norm_g):
    b, l, _ = p3.shape
    top_k = min(DSA_TOPK_MAX, l // 4)
    nb = l // TILE
    nc = l // KEY_CHUNK
    sl = np.tril(np.ones((KEY_CHUNK, KEY_CHUNK), np.float32), -1)
    dup = lambda g: jnp.concatenate([g, g]).reshape(1, LANE).astype(F32)
    consts = [dup(q_norm_g), dup(k_norm_g), jnp.asarray(sl, BF16)]
    qblk = lambda name: pl.BlockSpec((None, TILE, BRANCH_W), lambda i, j: (i, j, P_OFF[name] // BRANCH_W))
    seq = lambda name: pl.BlockSpec((None, l, LANE), lambda i, j: (i, 0, P_OFF[name] // LANE),
                                    pipeline_mode=pl.Buffered(1))
    chunks = lambda dt: pltpu.VMEM((nc, KEY_CHUNK, LANE), dt)
    return pl.pallas_call(
        functools.partial(_dsa_kernel, top_k=top_k),
        grid=(b, nb),
        in_specs=[qblk("c_q"), qblk("c_iq"),
                  pl.BlockSpec((None, TILE, LANE), lambda i, j: (i, j, 0)),
                  qblk("c_gate"), seq("c_kk"), seq("c_vv"), seq("c_ikik")]
        + [_full_spec(a.shape) for a in consts],
        out_specs=pl.BlockSpec((None, TILE, BRANCH_W), lambda i, j: (i, j, 0)),
        out_shape=jax.ShapeDtypeStruct((b, l, BRANCH_W), BF16),
        scratch_shapes=[chunks(BF16), pltpu.VMEM((nc, LANE, KEY_CHUNK), BF16), chunks(BF16), chunks(I32),
                        chunks(I16), chunks(I16),
                        pltpu.VMEM((nc, KEY_CHUNK, DSA_HEADS * TILE), F32), pltpu.VMEM((8, LANE), I32),
                        pltpu.VMEM((LANE, DSA_HEADS * TILE), F32)],
        compiler_params=pltpu.CompilerParams(dimension_semantics=("arbitrary", "arbitrary"),
                                             vmem_limit_bytes=VMEM_LIMIT),
        name="dsa",
    )(p3, p3, sm3, p3, p3, p3, p3, *consts)


def _ssd_kernel(z_ref, xbc_ref, small_ref, cw_ref, cbias_ref, dtb_ref, apad_ref, dexp_ref, ng_ref, tril_ref,
                e64_ref, e128_ref, o_ref, st_ref, halo_ref, buf_ref):
    @pl.when(pl.program_id(1) == 0)
    def _():
        st_ref[...] = jnp.zeros_like(st_ref)
        halo_ref[...] = jnp.zeros_like(halo_ref)

    nhp = SSD_HEADS * SSD_HD
    gs = SSD_STATE
    act = _silu(_causal_conv(xbc_ref[...].astype(F32), cw_ref, halo_ref, buf_ref) + cbias_ref[...])
    xs = act[:, :nhp]
    bm = act[:, nhp:nhp + SSD_GROUPS * gs]
    cm = act[:, nhp + SSD_GROUPS * gs:]
    dt_full = _softplus(small_ref[...] + dtb_ref[...])
    cs_full = _dot01_l(tril_ref[...], dt_full * apad_ref[...])
    dt_exp = _dot01_r(dt_full, e64_ref[...])
    cs_exp = _dot01_r(cs_full, e64_ref[...])
    cs_lb = _dot01_r(cs_full, e128_ref[...])
    xdt = xs * dt_exp
    cs_last = cs_exp[TILE - 1:TILE, :]
    xds = (xdt * jnp.exp(cs_last - cs_exp)).astype(BF16)
    ecs = jnp.exp(cs_exp)
    xdt_b = xdt.astype(BF16)

    r = lax.broadcasted_iota(I32, (TILE, TILE), 0)
    c = lax.broadcasted_iota(I32, (TILE, TILE), 1)
    incl = r >= c
    lo = c < HALF
    hpg = SSD_HEADS // SSD_GROUPS
    gw = hpg * SSD_HD
    y_parts = []
    for g in range(SSD_GROUPS):
        bmg = bm[:, g * gs:(g + 1) * gs]
        cmg = cm[:, g * gs:(g + 1) * gs].astype(BF16)
        cbg = _dot_nt(cmg, bmg.astype(BF16))
        for mp in range(hpg // 2):
            pair = (g * hpg) // 2 + mp
            xpair = xdt_b[:, pair * LANE:(pair + 1) * LANE]
            ys = []
            for e in range(2):
                h = 2 * pair + e
                cb = cs_lb[:, h * LANE:(h + 1) * LANE]
                lm = jnp.where(incl, jnp.exp(jnp.where(incl, cb - cb.T, 0.0)), 0.0)
                ys.append(_dot((cbg * lm).astype(BF16), xpair))
            y_parts.append(jnp.where(lo, ys[0], ys[1]))
        st = st_ref[g]
        y_off = _dot(cmg, st.astype(BF16)) * ecs[:, g * gw:(g + 1) * gw]
        y_parts[-(hpg // 2):] = [yp + y_off[:, k * LANE:(k + 1) * LANE]
                                 for k, yp in enumerate(y_parts[-(hpg // 2):])]
        new = _dot(bmg.T.astype(BF16), xds[:, g * gw:(g + 1) * gw])
        st_ref[g] = jnp.exp(cs_last[:, g * gw:(g + 1) * gw]) * st + new
    y = jnp.concatenate(y_parts, axis=1) + xs * dexp_ref[...]
    yz = y * _silu(z_ref[...].astype(F32))
    o_ref[...] = _rms(yz, ng_ref[...]).astype(o_ref.dtype)


def _ssd(p3, sm3, conv_w, conv_b, a_log, dt_bias, d_skip, norm_g):
    b, l, _ = p3.shape
    consts = [conv_w.astype(F32), conv_b.reshape(1, -1).astype(F32), _pad_lanes(dt_bias, SM_DT),
              _pad_lanes(-jnp.exp(a_log.astype(F32)), SM_DT),
              jnp.repeat(d_skip.astype(F32), SSD_HD).reshape(1, -1), norm_g.reshape(1, -1).astype(F32),
              jnp.asarray(np.tril(np.ones((TILE, TILE), np.float32)), BF16),
              jnp.asarray(_np_lane_expand(SM_DT, SSD_HEADS, SSD_HD), BF16),
              jnp.asarray(_np_lane_expand(SM_DT, SSD_HEADS, LANE), BF16)]
    return pl.pallas_call(
        _ssd_kernel,
        grid=(b, l // TILE),
        in_specs=[pl.BlockSpec((None, TILE, BRANCH_W), lambda i, j: (i, j, P_OFF["d_z"] // BRANCH_W)),
                  pl.BlockSpec((None, TILE, SSD_XBC), lambda i, j: (i, j, P_OFF["d_xbc"] // SSD_XBC)),
                  pl.BlockSpec((None, TILE, LANE), lambda i, j: (i, j, 0))]
        + [_full_spec(a.shape) for a in consts],
        out_specs=pl.BlockSpec((None, TILE, BRANCH_W), lambda i, j: (i, j, 0)),
        out_shape=jax.ShapeDtypeStruct((b, l, BRANCH_W), BF16),
        scratch_shapes=[pltpu.VMEM((SSD_GROUPS, SSD_STATE, (SSD_HEADS // SSD_GROUPS) * SSD_HD), F32),
                        pltpu.VMEM((8, SSD_XBC), F32), pltpu.VMEM((TILE + 8, SSD_XBC), F32)],
        compiler_params=pltpu.CompilerParams(dimension_semantics=("arbitrary", "arbitrary"),
                                             vmem_limit_bytes=VMEM_LIMIT),
        name="ssd",
    )(p3, p3, sm3, *consts)


def _mem_kv_kernel(mem_ref, g_ref, w_ref, kg_ref, k_ref, v_ref):
    kv = _dot(_rms(mem_ref[...], g_ref[...]).astype(BF16), w_ref[...])
    for h in range(MEM_HEADS):
        hs = slice(h * MEM_HD, (h + 1) * MEM_HD)
        k_ref[:, hs] = _rms(kv[:, hs], kg_ref[...]).astype(k_ref.dtype)
    v_ref[...] = kv[:, BRANCH_W:].astype(v_ref.dtype)


def _mem_kv(mem, mem_norm_g, w_kv_bf16, k_norm_g):
    b, m, _ = mem.shape
    consts = [mem_norm_g.reshape(1, -1).astype(F32), w_kv_bf16, k_norm_g.reshape(1, -1).astype(F32)]
    out = jax.ShapeDtypeStruct((b, m, BRANCH_W), BF16)
    return pl.pallas_call(
        _mem_kv_kernel,
        grid=(b,),
        in_specs=[pl.BlockSpec((None, m, D_MODEL), lambda i: (i, 0, 0))] + [_full_spec(a.shape) for a in consts],
        out_specs=[pl.BlockSpec((None, m, BRANCH_W), lambda i: (i, 0, 0))] * 2,
        out_shape=[out, out],
        compiler_params=pltpu.CompilerParams(dimension_semantics=("arbitrary",), vmem_limit_bytes=VMEM_LIMIT),
        name="mem_kv",
    )(mem, *consts)


def _mem_attn_kernel(q_ref, gate_ref, k_ref, v_ref, qg_ref, o_ref):
    q = q_ref[...].astype(F32)
    outs = []
    for h in range(MEM_HEADS):
        hs = slice(h * MEM_HD, (h + 1) * MEM_HD)
        qn = _rms(q[:, hs], qg_ref[...]).astype(BF16)
        lg = _dot_nt(qn, k_ref[:, hs]) * (MEM_HD ** -0.5)
        p = jnp.exp(lg - jnp.max(lg, axis=1, keepdims=True))
        outs.append(_dot(p.astype(BF16), v_ref[:, hs]) / jnp.sum(p, axis=1, keepdims=True))
    o_ref[...] = (jnp.concatenate(outs, axis=1) * _silu(gate_ref[...].astype(F32))).astype(o_ref.dtype)


def _mem_attn(p3, k, v, q_norm_g):
    b, l, _ = p3.shape
    m = k.shape[1]
    tm = min(512, l)
    qg = q_norm_g.reshape(1, -1).astype(F32)
    blk = lambda name: pl.BlockSpec((None, tm, BRANCH_W), lambda i, j: (i, j, P_OFF[name] // BRANCH_W))
    kv = pl.BlockSpec((None, m, BRANCH_W), lambda i, j: (i, 0, 0))
    return pl.pallas_call(
        _mem_attn_kernel,
        grid=(b, l // tm),
        in_specs=[blk("m_q"), blk("m_gate"), kv, kv, _full_spec(qg.shape)],
        out_specs=pl.BlockSpec((None, tm, BRANCH_W), lambda i, j: (i, j, 0)),
        out_shape=jax.ShapeDtypeStruct((b, l, BRANCH_W), BF16),
        compiler_params=pltpu.CompilerParams(dimension_semantics=("arbitrary", "arbitrary"),
                                             vmem_limit_bytes=VMEM_LIMIT),
        name="mem_attn",
    )(p3, p3, k, v, qg)


def _reordered_w_in(w):
    cols = jnp.asarray(np.maximum(_P_COLUMNS, 0))
    keep = jnp.asarray((_P_COLUMNS >= 0).astype(np.float32))
    return (jnp.take(w, cols, axis=1) * keep).astype(BF16)


def kernel(x, mem, norm_g, w_in, gdn_conv_w, gdn_a_log, gdn_dt_bias, gdn_norm_g, sg_ln_g, sg_ln_b, sg_w, sg_b, dsa_q_norm_g, dsa_k_norm_g, ssd_conv_w, ssd_conv_b, ssd_a_log, ssd_dt_bias, ssd_d, ssd_norm_g, mem_norm_g, w_mem_kv, mem_q_norm_g, mem_k_norm_g, w_gate, w_branch, w_out):
    b, l, d = x.shape
    depth = norm_g.shape[0]
    x2d = x.reshape(b * l, d)
    for i in range(depth):
        g = norm_g[i].reshape(1, d).astype(F32)
        p2d, sm2d = _in_proj(x2d, g, _reordered_w_in(w_in[i]))
        p3 = p2d.reshape(b, l, P_MAIN)
        sm3 = sm2d.reshape(b, l, LANE)
        mk, mv = _mem_kv(mem, mem_norm_g[i], w_mem_kv[i].astype(BF16), mem_k_norm_g[i])
        ys = (
            _gdn(p3, sm3, gdn_conv_w[i], gdn_a_log[i], gdn_dt_bias[i], gdn_norm_g[i]),
            _sg(p3, sg_ln_g[i], sg_ln_b[i], sg_w[i], sg_b[i]),
            _dsa(p3, sm3, dsa_q_norm_g[i], dsa_k_norm_g[i]),
            _ssd(p3, sm3, ssd_conv_w[i], ssd_conv_b[i], ssd_a_log[i], ssd_dt_bias[i], ssd_d[i], ssd_norm_g[i]),
            _mem_attn(p3, mk, mv, mem_q_norm_g[i]),
        )
        ys = [y.reshape(b * l, BRANCH_W) for y in ys]
        x2d = _merge(x2d, g, ys, w_gate[i].astype(BF16), w_branch[i].astype(BF16), w_out[i].astype(BF16))
    return x2d.reshape(b, l, d)
```

```python
import functools

import numpy as np
import jax
import jax.numpy as jnp
from jax import lax
from jax.experimental import pallas as pl
from jax.experimental.pallas import tpu as pltpu

F32 = jnp.float32
BF16 = jnp.bfloat16
I32 = jnp.int32

D_MODEL = 1024
N_BRANCH = 5
BRANCH_W = D_MODEL // 2
CONV_K = 4
EPS = 1e-6
GDN_HEADS = 4
GDN_DK = BRANCH_W // GDN_HEADS
GDN_CHUNK = 64
GDN_QKV = 3 * BRANCH_W
SG_CHUNK = 128
SG_GROUPS = 4
DSA_HEADS = 8
DSA_HD = BRANCH_W // DSA_HEADS
IDX_HEADS = 8
IDX_HD = 64
DSA_TOPK_MAX = 256
Q_BLOCK = 128
SSD_HEADS = 8
SSD_HD = BRANCH_W // SSD_HEADS
SSD_GROUPS = 2
SSD_STATE = 128
SSD_CHUNK = 128
SSD_XBC = SSD_HEADS * SSD_HD + 2 * SSD_GROUPS * SSD_STATE
MEM_HEADS = 4
MEM_HD = BRANCH_W // MEM_HEADS

IN_SPLITS = (
    GDN_QKV, GDN_HEADS, GDN_HEADS, BRANCH_W,
    BRANCH_W, BRANCH_W, BRANCH_W,
    DSA_HEADS * DSA_HD, DSA_HD, DSA_HD, IDX_HEADS * IDX_HD, IDX_HD, IDX_HEADS, BRANCH_W,
    BRANCH_W, SSD_XBC, SSD_HEADS,
    MEM_HEADS * MEM_HD, BRANCH_W,
)
_SEG_NAMES = ("a_qkv", "a_a", "a_b", "a_gate", "b_u", "b_v", "b_gate", "c_q", "c_k", "c_v", "c_iq", "c_ik",
              "c_iw", "c_gate", "d_z", "d_xbc", "d_dt", "m_q", "m_gate")
_SEG_START = dict(zip(_SEG_NAMES, np.concatenate([[0], np.cumsum(IN_SPLITS)[:-1]]).tolist()))
_SEG_WIDTH = dict(zip(_SEG_NAMES, IN_SPLITS))

LANE = 128
HALF = LANE // 2
TILE = 128
VMEM_LIMIT = 56 * 1024 * 1024

_P_LAYOUT = (("a_qkv", 1536), ("b_u", 512), ("d_xbc", 1024), ("a_gate", 512), ("b_v", 512), ("b_gate", 512),
             ("c_q", 512), ("c_iq", 512), ("c_gate", 512), ("d_z", 512), ("m_q", 512), ("m_gate", 512),
             ("c_kk", 128), ("c_vv", 128), ("c_ikik", 128), ("small", 128))
P_OFF = {}
_o = 0
for _n, _w in _P_LAYOUT:
    P_OFF[_n] = _o
    _o += _w
P_COLS = _o
P_MAIN = P_OFF["small"]
SM_A, SM_B, SM_DT, SM_IW = 0, 4, 8, 16


def _p_columns():
    cols = []
    for name, width in _P_LAYOUT:
        if name in _SEG_START:
            cols += list(range(_SEG_START[name], _SEG_START[name] + width))
        elif name in ("c_kk", "c_vv", "c_ikik"):
            src = {"c_kk": "c_k", "c_vv": "c_v", "c_ikik": "c_ik"}[name]
            one = list(range(_SEG_START[src], _SEG_START[src] + _SEG_WIDTH[src]))
            cols += one + one
        else:
            sm = [-1] * LANE
            for seg, at in (("a_a", SM_A), ("a_b", SM_B), ("d_dt", SM_DT), ("c_iw", SM_IW)):
                for j in range(_SEG_WIDTH[seg]):
                    sm[at + j] = _SEG_START[seg] + j
            cols += sm
    return np.asarray(cols, np.int32)


_P_COLUMNS = _p_columns()


def _dot(a, b, precision=None):
    return jnp.dot(a, b, preferred_element_type=F32, precision=precision)


def _dot_nt(a, b):
    return lax.dot_general(a, b, (((1,), (1,)), ((), ())), preferred_element_type=F32)


def _split3(x):
    hi = x.astype(BF16)
    r1 = x - hi.astype(F32)
    mid = r1.astype(BF16)
    lo = (r1 - mid.astype(F32)).astype(BF16)
    return hi, mid, lo


def _dot01_r(x, e):
    hi, mid, lo = _split3(x)
    return _dot(hi, e) + _dot(mid, e) + _dot(lo, e)


def _dot01_l(e, x):
    hi, mid, lo = _split3(x)
    return _dot(e, hi) + _dot(e, mid) + _dot(e, lo)


def _sigmoid(x):
    return 1.0 / (1.0 + jnp.exp(-x))


def _silu(x):
    return x * _sigmoid(x)


def _softplus(x):
    return jnp.maximum(x, 0.0) + jnp.log1p(jnp.exp(-jnp.abs(x)))


def _gelu_tanh(x):
    return 0.5 * x * (1.0 + jnp.tanh(np.sqrt(2.0 / np.pi).astype(np.float32) * (x + 0.044715 * (x * x * x))))


def _rms(x, g, eps=EPS):
    return x * lax.rsqrt(jnp.mean(x * x, axis=-1, keepdims=True) + eps) * g


def _full_spec(shape):
    nd = len(shape)
    return pl.BlockSpec(shape, lambda *_: (0,) * nd)


def _in_proj_kernel(x_ref, g_ref, w_ref, o_ref, s_ref):
    h = _rms(x_ref[...], g_ref[...]).astype(BF16)
    step = 1024
    for c0 in range(0, P_MAIN, step):
        c1 = min(c0 + step, P_MAIN)
        o_ref[:, c0:c1] = _dot(h, w_ref[:, c0:c1]).astype(o_ref.dtype)
    s_ref[...] = _dot(h, w_ref[:, P_MAIN:])


def _in_proj(x2d, g, w_bf16):
    m = x2d.shape[0]
    tm = min(512, m)
    return pl.pallas_call(
        _in_proj_kernel,
        grid=(m // tm,),
        in_specs=[pl.BlockSpec((tm, D_MODEL), lambda i: (i, 0)),
                  pl.BlockSpec((1, D_MODEL), lambda i: (0, 0)),
                  pl.BlockSpec((D_MODEL, P_COLS), lambda i: (0, 0), pipeline_mode=pl.Buffered(1))],
        out_specs=[pl.BlockSpec((tm, P_MAIN), lambda i: (i, 0)), pl.BlockSpec((tm, LANE), lambda i: (i, 0))],
        out_shape=[jax.ShapeDtypeStruct((m, P_MAIN), BF16), jax.ShapeDtypeStruct((m, LANE), F32)],
        compiler_params=pltpu.CompilerParams(dimension_semantics=("arbitrary",), vmem_limit_bytes=VMEM_LIMIT),
        name="in_proj",
    )(x2d, g, w_bf16)


def _merge_kernel(x_ref, g_ref, y0, y1, y2, y3, y4, wg_ref, wb_ref, wo_ref, o_ref):
    x = x_ref[...]
    h = _rms(x, g_ref[...]).astype(BF16)
    acc = None
    for p, y in enumerate((y0, y1, y2, y3, y4)):
        term = _sigmoid(_dot(h, wg_ref[p])) * _dot(y[...], wb_ref[p])
        acc = term if acc is None else acc + term
    o_ref[...] = x + _dot(acc.astype(BF16), wo_ref[...])


def _merge(x2d, g, ys, wg, wb, wo):
    m = x2d.shape[0]
    tm = min(512, m)
    row = lambda i: (i, 0)
    const = pl.Buffered(1)
    return pl.pallas_call(
        _merge_kernel,
        grid=(m // tm,),
        in_specs=[pl.BlockSpec((tm, D_MODEL), row), pl.BlockSpec((1, D_MODEL), lambda i: (0, 0))]
        + [pl.BlockSpec((tm, BRANCH_W), row)] * N_BRANCH
        + [pl.BlockSpec((N_BRANCH, D_MODEL, D_MODEL), lambda i: (0, 0, 0), pipeline_mode=const),
           pl.BlockSpec((N_BRANCH, BRANCH_W, D_MODEL), lambda i: (0, 0, 0), pipeline_mode=const),
           pl.BlockSpec((D_MODEL, D_MODEL), lambda i: (0, 0), pipeline_mode=const)],
        out_specs=pl.BlockSpec((tm, D_MODEL), row),
        out_shape=jax.ShapeDtypeStruct((m, D_MODEL), F32),
        compiler_params=pltpu.CompilerParams(dimension_semantics=("arbitrary",), vmem_limit_bytes=VMEM_LIMIT),
        name="merge_out",
    )(x2d, g, *ys, wg, wb, wo)


def _causal_conv(x, w_ref, halo_ref, buf_ref):
    buf_ref[0:8, :] = halo_ref[...]
    buf_ref[8:8 + TILE, :] = x
    y = w_ref[CONV_K - 1:CONV_K, :] * x
    for j in range(CONV_K - 1):
        y = y + w_ref[j:j + 1, :] * buf_ref[8 - (CONV_K - 1) + j:8 - (CONV_K - 1) + j + TILE, :]
    halo_ref[...] = x[TILE - 8:TILE, :]
    return y


def _np_lane_expand(first_row, n_heads, width):
    e = np.zeros((LANE, n_heads * width), np.float32)
    for h in range(n_heads):
        e[first_row + h, h * width:(h + 1) * width] = 1.0
    return e


def _pad_lanes(vec, at):
    out = jnp.zeros((1, LANE), F32)
    return lax.dynamic_update_slice(out, vec.reshape(1, -1).astype(F32), (0, at))


def _gdn_kernel(qkv_ref, small_ref, gate_ref, cw_ref, apad_ref, dtb_ref, ng_ref, btril_ref, eg_ref, eb_ref,
                o_ref, st_ref, halo_ref, buf_ref):
    @pl.when(pl.program_id(1) == 0)
    def _():
        st_ref[...] = jnp.zeros_like(st_ref)
        halo_ref[...] = jnp.zeros_like(halo_ref)

    act = _silu(_causal_conv(qkv_ref[...].astype(F32), cw_ref, halo_ref, buf_ref))
    small = small_ref[...]
    g_full = -apad_ref[...] * _softplus(small + dtb_ref[...])
    beta_full = _sigmoid(small)
    gc_full = _dot01_l(btril_ref[...], g_full)
    g_lb = _dot01_r(gc_full, eg_ref[...])
    b_lb = _dot01_r(beta_full, eb_ref[...])

    r = lax.broadcasted_iota(I32, (TILE, TILE), 0)
    c = lax.broadcasted_iota(I32, (TILE, TILE), 1)
    same = (r >> 6) == (c >> 6)
    incl = same & (r >= c)
    strict = same & (r > c)
    eye = (r == c).astype(F32)
    n_sub = TILE // GDN_CHUNK
    ng = ng_ref[...]
    heads = range(GDN_HEADS)
    q, k, v, cb, bb = [], [], [], [], []
    for h in heads:
        qh = act[:, h * GDN_DK:(h + 1) * GDN_DK]
        kh = act[:, BRANCH_W + h * GDN_DK:BRANCH_W + (h + 1) * GDN_DK]
        q.append(qh * lax.rsqrt(jnp.sum(qh * qh, axis=-1, keepdims=True) + EPS) * (GDN_DK ** -0.5))
        k.append(kh * lax.rsqrt(jnp.sum(kh * kh, axis=-1, keepdims=True) + EPS))
        v.append(act[:, 2 * BRANCH_W + h * GDN_DK:2 * BRANCH_W + (h + 1) * GDN_DK])
        cb.append(g_lb[:, h * LANE:(h + 1) * LANE])
        bb.append(b_lb[:, h * LANE:(h + 1) * LANE])
    dm = [jnp.where(incl, jnp.exp(jnp.where(incl, cb[h] - cb[h].T, 0.0)), 0.0) for h in heads]
    kb = [k[h].astype(BF16) for h in heads]
    kk = [_dot_nt(kb[h], kb[h]) for h in heads]
    qkb = [(_dot_nt(q[h].astype(BF16), kb[h]) * dm[h]).astype(BF16) for h in heads]
    mp = [jnp.where(strict, -(bb[h] * kk[h] * dm[h]), 0.0) for h in heads]
    inv = [eye + mp[h] for h in heads]
    for _ in range(int(np.log2(GDN_CHUNK)) - 1):
        mpb = [mp[h].astype(BF16) for h in heads]
        mp = [_dot(mpb[h], mpb[h]) for h in heads]
        inv = [inv[h] + _dot(inv[h].astype(BF16), mp[h].astype(BF16)) for h in heads]
    eg = [jnp.exp(cb[h]) for h in heads]
    sol = [_dot(inv[h].astype(BF16),
                jnp.concatenate([bb[h] * eg[h] * k[h], bb[h] * v[h]], axis=1).astype(BF16)) for h in heads]
    wk = [sol[h][:, :GDN_DK].astype(BF16) for h in heads]
    u0 = [sol[h][:, GDN_DK:] for h in heads]
    qd = [(q[h] * eg[h]).astype(BF16) for h in heads]
    zero = jnp.zeros((GDN_CHUNK, LANE), F32)
    st = [st_ref[h] for h in heads]
    o_parts = [[] for _ in heads]
    for cc in range(n_sub):
        rows = slice(cc * GDN_CHUNK, (cc + 1) * GDN_CHUNK)
        in_chunk = (r >> 6) == cc
        glast = [cb[h][(cc + 1) * GDN_CHUNK - 1:(cc + 1) * GDN_CHUNK, :] for h in heads]
        stb = [st[h].astype(BF16) for h in heads]
        u = [u0[h][rows] - _dot(wk[h][rows], stb[h]) for h in heads]
        upad = [jnp.concatenate([u[h] if j == cc else zero for j in range(n_sub)], axis=0).astype(BF16)
                for h in heads]
        for h in heads:
            o_parts[h].append(_dot(qd[h][rows], stb[h]) + _dot(qkb[h][rows], upad[h]))
        kd = [jnp.where(in_chunk, k[h] * jnp.exp(jnp.where(in_chunk, glast[h] - cb[h], 0.0)), 0.0) for h in heads]
        st = [jnp.exp(glast[h]) * st[h] + _dot(kd[h].T.astype(BF16), upad[h]) for h in heads]
    for h in heads:
        st_ref[h] = st[h]
    outs = [_rms(jnp.concatenate(o_parts[h], axis=0), ng) for h in heads]
    y = jnp.concatenate(outs, axis=1) * _silu(gate_ref[...].astype(F32))
    o_ref[...] = y.astype(o_ref.dtype)


def _gdn(p3, sm3, conv_w, a_log, dt_bias, norm_g):
    b, l, _ = p3.shape
    btril = np.zeros((TILE, TILE), np.float32)
    for i in range(TILE):
        lo = (i // GDN_CHUNK) * GDN_CHUNK
        btril[i, lo:i + 1] = 1.0
    consts = [conv_w.astype(F32), _pad_lanes(jnp.exp(a_log.astype(F32)), SM_A), _pad_lanes(dt_bias, SM_A),
              norm_g.reshape(1, -1).astype(F32), jnp.asarray(btril, BF16),
              jnp.asarray(_np_lane_expand(SM_A, GDN_HEADS, LANE), BF16),
              jnp.asarray(_np_lane_expand(SM_B, GDN_HEADS, LANE), BF16)]
    return pl.pallas_call(
        _gdn_kernel,
        grid=(b, l // TILE),
        in_specs=[pl.BlockSpec((None, TILE, GDN_QKV), lambda i, j: (i, j, P_OFF["a_qkv"] // GDN_QKV)),
                  pl.BlockSpec((None, TILE, LANE), lambda i, j: (i, j, 0)),
                  pl.BlockSpec((None, TILE, BRANCH_W), lambda i, j: (i, j, P_OFF["a_gate"] // BRANCH_W))]
        + [_full_spec(a.shape) for a in consts],
        out_specs=pl.BlockSpec((None, TILE, BRANCH_W), lambda i, j: (i, j, 0)),
        out_shape=jax.ShapeDtypeStruct((b, l, BRANCH_W), BF16),
        scratch_shapes=[pltpu.VMEM((GDN_HEADS, GDN_DK, GDN_DK), F32), pltpu.VMEM((8, GDN_QKV), F32),
                        pltpu.VMEM((TILE + 8, GDN_QKV), F32)],
        compiler_params=pltpu.CompilerParams(dimension_semantics=("arbitrary", "arbitrary"),
                                             vmem_limit_bytes=VMEM_LIMIT),
        name="gdn",
    )(p3, sm3, p3, *consts)


def _sg_kernel(u_ref, v_ref, gate_ref, lng_ref, lnb_ref, w_ref, bs_ref, o_ref):
    u = _gelu_tanh(u_ref[...].astype(F32))
    v = _gelu_tanh(v_ref[...].astype(F32))
    mu = jnp.mean(v, axis=-1, keepdims=True)
    var = jnp.mean(jnp.square(v - mu), axis=-1, keepdims=True)
    vn = ((v - mu) * lax.rsqrt(var + 1e-5) * lng_ref[...] + lnb_ref[...]).astype(BF16)
    r = lax.broadcasted_iota(I32, (SG_CHUNK, SG_CHUNK), 0)
    c = lax.broadcasted_iota(I32, (SG_CHUNK, SG_CHUNK), 1)
    gw = BRANCH_W // SG_GROUPS
    mixed = []
    for g in range(SG_GROUPS):
        wc = jnp.where(r >= c, w_ref[g], 0.0).astype(BF16)
        mixed.append(_dot(wc, vn[:, g * gw:(g + 1) * gw]))
    mixed = jnp.concatenate(mixed, axis=1) + bs_ref[...]
    o_ref[...] = (u * mixed * _silu(gate_ref[...].astype(F32))).astype(o_ref.dtype)


def _sg(p3, ln_g, ln_b, w_s, b_s):
    b, l, _ = p3.shape
    gw = BRANCH_W // SG_GROUPS
    bs_full = jnp.repeat(jnp.swapaxes(b_s, 0, 1).astype(F32), gw, axis=1)
    consts = [ln_g.reshape(1, -1).astype(F32), ln_b.reshape(1, -1).astype(F32), w_s.astype(F32), bs_full]
    blk = lambda name: pl.BlockSpec((None, SG_CHUNK, BRANCH_W), lambda i, j: (i, j, P_OFF[name] // BRANCH_W))
    return pl.pallas_call(
        _sg_kernel,
        grid=(b, l // SG_CHUNK),
        in_specs=[blk("b_u"), blk("b_v"), blk("b_gate")] + [_full_spec(a.shape) for a in consts],
        out_specs=pl.BlockSpec((None, SG_CHUNK, BRANCH_W), lambda i, j: (i, j, 0)),
        out_shape=jax.ShapeDtypeStruct((b, l, BRANCH_W), BF16),
        compiler_params=pltpu.CompilerParams(dimension_semantics=("arbitrary", "arbitrary"),
                                             vmem_limit_bytes=VMEM_LIMIT),
        name="spatial_gating",
    )(p3, p3, p3, *consts)


_NEG = -1e30
_INT_MIN = -2 ** 31


def _fold8(x, op):
    parts = [x[r:r + 8] for r in range(0, x.shape[0], 8)]
    while len(parts) > 1:
        parts = [op(parts[k], parts[k + 1]) for k in range(0, len(parts) - 1, 2)] + (
            [parts[-1]] if len(parts) % 2 else [])
    return parts[0]


KEY_CHUNK = 2 * TILE


def _bit_planes(u):
    a = [u[8 * g:8 * g + 8] for g in range(32)]
    j, m = 16, 0x0000FFFF
    while j:
        k = 0
        while k < 32:
            t = (a[k] ^ (a[k + j] >> j)) & m
            a[k] = a[k] ^ t
            a[k + j] = a[k + j] ^ (t << j)
            k = (k + j + 1) & ~j
        j >>= 1
        m = m ^ ((m << j) & 0xFFFFFFFF)
    return [a[31 - b] for b in range(32)]


def _dsa_kernel(q_ref, iq_ref, small_ref, gate_ref, kk_ref, vv_ref, ik_ref, qg_ref, kg_ref, sl_ref,
                o_ref, kn_ref, vt_ref, ikb_ref, key_ref, plane_ref, lg_ref, tau_ref, acc_ref, *, top_k):
    i = pl.program_id(1)
    nh = DSA_HEADS
    nc = key_ref.shape[0]
    kc = KEY_CHUNK

    srow = lax.broadcasted_iota(I32, (kc, LANE), 0)
    lane = lax.broadcasted_iota(I32, (kc, LANE), 1)
    qpos = i * TILE + lane
    pos_lanes = (lane == HALF) | (lane == HALF + 1)

    @pl.when(i == 0)
    def _():
        def prep(jc, carry):
            rows = pl.ds(pl.multiple_of(jc * kc, kc), kc)
            kpos = jnp.where(lane == HALF, srow, jc * kc).astype(F32)
            kn = jnp.where(lane < HALF, _rms(kk_ref[rows, :].astype(F32), kg_ref[...]), jnp.where(pos_lanes, kpos, 0.0))
            kn_ref[jc] = kn.astype(BF16)
            v = vv_ref[rows, :].astype(F32)
            vt_ref[jc] = jnp.concatenate([v[r:r + TILE].T for r in range(0, kc, TILE)], axis=1).astype(BF16)
            ikb_ref[jc] = ik_ref[rows, :]
            return carry
        lax.fori_loop(0, nc, prep, 0)

    n_ch = (i + 2) // 2
    n_ch2 = (n_ch + 1) // 2

    @pl.when(n_ch < nc)
    def _():
        key_ref[n_ch] = jnp.full((kc, LANE), _INT_MIN, I32)

    iq = iq_ref[...].astype(F32)
    lane_q = lax.broadcasted_iota(I32, (TILE, LANE), 1)
    lo = lane_q < HALF
    pos_lanes_q = (lane_q == HALF) | (lane_q == HALF + 1)
    iq_st = jnp.concatenate(
        [jnp.where(lo if h % 2 == 0 else ~lo, iq[:, (h // 2) * LANE:(h // 2 + 1) * LANE], 0.0) for h in range(nh)],
        axis=0).astype(BF16)
    small_t = small_ref[...].T * (IDX_HEADS ** -0.5 * IDX_HD ** -0.5)
    iw_row = jnp.concatenate([small_t[SM_IW + h:SM_IW + h + 1, :] for h in range(nh)], axis=1)

    def score_pair(j2, carry):
        jcs = (2 * j2, 2 * j2 + 1)
        raw = [_dot_nt(ikb_ref[jc], iq_st) for jc in jcs]
        for jc, s_raw in zip(jcs, raw):
            s_h = jnp.maximum(s_raw, 0.0) * iw_row
            s = s_h[:, 0:TILE]
            for h in range(1, nh):
                s = s + s_h[:, h * TILE:(h + 1) * TILE]
            bits = pltpu.bitcast(s, I32)
            key = jnp.where(bits < 0, bits ^ 0x7FFFFFFF, bits)
            key = jnp.where(jc * kc + srow <= qpos, key, _INT_MIN)
            key_ref[jc] = key
            planes = _bit_planes(key ^ _INT_MIN)
            for b in range(32):
                plane_ref[jc, b] = planes[b]
        return carry

    lax.fori_loop(0, n_ch2, score_pair, 0)

    def count(pred):
        def body(j2, acc):
            for u in range(2):
                acc = acc + _fold8(jnp.where(pred(key_ref[2 * j2 + u]), 1.0, 0.0), jnp.add)
            return acc
        acc = lax.fori_loop(0, n_ch2, body, jnp.zeros((8, LANE), F32))
        return jnp.sum(acc, axis=0, keepdims=True)

    tau_ref[...] = jnp.full(tau_ref.shape, _INT_MIN + 1, I32)

    @pl.when((i + 1) * TILE > top_k)
    def _():
        def ones_at(live, b):
            acc = jnp.zeros((8, LANE), I32)
            for jc in range(nc):
                acc = acc + lax.population_count(live[jc] & plane_ref[jc, b])
            return jnp.sum(acc.astype(F32), axis=0, keepdims=True)

        live0 = tuple(jnp.where(jc < 2 * n_ch2, jnp.full((8, LANE), -1, I32), 0) for jc in range(nc))

        def bit_body(bi, carry):
            live, cnt, left, tau_u = carry
            b = 31 - bi
            take = cnt >= left
            left = jnp.where(take, left, left - cnt)
            tau_u = jnp.where(take, tau_u | lax.shift_left(jnp.int32(1), b), tau_u)
            b_next = jnp.maximum(b - 1, 0)
            acc = jnp.zeros((8, LANE), I32)
            new_live = []
            for jc in range(nc):
                with_bit = live[jc] & plane_ref[jc, b]
                keep = jnp.where(take, with_bit, live[jc] ^ with_bit)
                new_live.append(keep)
                acc = acc + lax.population_count(keep & plane_ref[jc, b_next])
            return tuple(new_live), jnp.sum(acc.astype(F32), axis=0, keepdims=True), left, tau_u

        init = (live0, ones_at(live0, 31), jnp.full((1, LANE), float(top_k), F32), jnp.zeros((1, LANE), I32))
        tau = lax.fori_loop(0, 32, bit_body, init)[3] ^ _INT_MIN
        tau = jnp.where(qpos[0:1, :] + 1 <= top_k, _INT_MIN + 1, tau)
        tau_ref[...] = jnp.broadcast_to(tau, tau_ref.shape)

    tau = tau_ref[0:1, :]
    need = top_k - count(lambda kx: kx > tau)

    q = q_ref[...].astype(F32)
    q_parts = []
    for mpair in range(nh // 2):
        qp = q[:, mpair * LANE:(mpair + 1) * LANE]
        sq = qp * qp
        s_lo = jnp.sum(jnp.where(lo, sq, 0.0), axis=1, keepdims=True)
        s_hi = jnp.sum(jnp.where(lo, 0.0, sq), axis=1, keepdims=True)
        qn = qp * lax.rsqrt(jnp.where(lo, s_lo, s_hi) * (1.0 / DSA_HD) + EPS) * qg_ref[...] * (DSA_HD ** -0.5)
        for e, qh in enumerate((qn, pltpu.roll(qn, HALF, 1))):
            slope = 2.0 ** (-8.0 * (2 * mpair + e + 1) / nh)
            q_parts.append(jnp.where(lo, qh, jnp.where(pos_lanes_q, slope, 0.0)))
    q_st = jnp.concatenate(q_parts, axis=0).astype(BF16)

    def mask_chunk(jc, carry):
        run, mred = carry
        key = key_ref[jc]
        eq = key == tau
        eqf = jnp.where(eq, 1.0, 0.0)
        prefix = _dot(sl_ref[...], eqf.astype(BF16)) + run
        sel = (key > tau) | (eq & (prefix < need))
        lg = _dot_nt(kn_ref[jc], q_st)
        new = []
        for h in range(nh):
            x = jnp.where(sel, lg[:, h * TILE:(h + 1) * TILE], _NEG)
            lg_ref[jc, :, h * TILE:(h + 1) * TILE] = x
            new.append(jnp.maximum(mred[h], _fold8(x, jnp.maximum)))
        return run + jnp.sum(_fold8(eqf, jnp.add), axis=0, keepdims=True), tuple(new)

    _, mred = lax.fori_loop(0, 2 * n_ch2, mask_chunk,
                            (jnp.zeros((1, LANE), F32), tuple(jnp.full((8, LANE), _NEG, F32) for _ in range(nh))))
    m_row = [jnp.max(mh, axis=0, keepdims=True) for mh in mred]

    acc_ref[...] = jnp.zeros_like(acc_ref)

    def pv_pair(j2, lred):
        lred = list(lred)
        pv = None
        for jc in (2 * j2, 2 * j2 + 1):
            ps = []
            for h in range(nh):
                p = jnp.exp(lg_ref[jc, :, h * TILE:(h + 1) * TILE] - m_row[h])
                lred[h] = lred[h] + _fold8(p, jnp.add)
                ps.append(p.astype(BF16))
            term = _dot(vt_ref[jc], jnp.concatenate(ps, axis=1))
            pv = term if pv is None else pv + term
        acc_ref[...] += pv
        return tuple(lred)

    lred = lax.fori_loop(0, n_ch2, pv_pair, tuple(jnp.zeros((8, LANE), F32) for _ in range(nh)))

    heads = []
    for h in range(nh):
        l_row = jnp.sum(lred[h], axis=0, keepdims=True)
        heads.append((acc_ref[:, h * TILE:(h + 1) * TILE] / l_row).T)
    pairs = [jnp.where(lo, heads[2 * mp], heads[2 * mp + 1]) for mp in range(nh // 2)]
    o_ref[...] = (jnp.concatenate(pairs, axis=1) * _silu(gate_ref[...].astype(F32))).astype(o_ref.dtype)


def _dsa(p3, sm3, q_norm_g, k_norm_g):
    b, l, _ = p3.shape
    top_k = min(DSA_TOPK_MAX, l // 4)
    nb = l // TILE
    nc = l // KEY_CHUNK
    sl = np.tril(np.ones((KEY_CHUNK, KEY_CHUNK), np.float32), -1)
    dup = lambda g: jnp.concatenate([g, g]).reshape(1, LANE).astype(F32)
    consts = [dup(q_norm_g), dup(k_norm_g), jnp.asarray(sl, BF16)]
    qblk = lambda name: pl.BlockSpec((None, TILE, BRANCH_W), lambda i, j: (i, j, P_OFF[name] // BRANCH_W))
    seq = lambda name: pl.BlockSpec((None, l, LANE), lambda i, j: (i, 0, P_OFF[name] // LANE),
                                    pipeline_mode=pl.Buffered(1))
    chunks = lambda dt: pltpu.VMEM((nc, KEY_CHUNK, LANE), dt)
    return pl.pallas_call(
        functools.partial(_dsa_kernel, top_k=top_k),
        grid=(b, nb),
        in_specs=[qblk("c_q"), qblk("c_iq"),
                  pl.BlockSpec((None, TILE, LANE), lambda i, j: (i, j, 0)),
                  qblk("c_gate"), seq("c_kk"), seq("c_vv"), seq("c_ikik")]
        + [_full_spec(a.shape) for a in consts],
        out_specs=pl.BlockSpec((None, TILE, BRANCH_W), lambda i, j: (i, j, 0)),
        out_shape=jax.ShapeDtypeStruct((b, l, BRANCH_W), BF16),
        scratch_shapes=[chunks(BF16), pltpu.VMEM((nc, LANE, KEY_CHUNK), BF16), chunks(BF16), chunks(I32),
                        pltpu.VMEM((nc, 32, 8, LANE), I32),
                        pltpu.VMEM((nc, KEY_CHUNK, DSA_HEADS * TILE), F32), pltpu.VMEM((8, LANE), I32),
                        pltpu.VMEM((LANE, DSA_HEADS * TILE), F32)],
        compiler_params=pltpu.CompilerParams(dimension_semantics=("arbitrary", "arbitrary"),
                                             vmem_limit_bytes=VMEM_LIMIT),
        name="dsa",
    )(p3, p3, sm3, p3, p3, p3, p3, *consts)


def _ssd_kernel(z_ref, xbc_ref, small_ref, cw_ref, cbias_ref, dtb_ref, apad_ref, dexp_ref, ng_ref, tril_ref,
                e64_ref, e128_ref, o_ref, st_ref, halo_ref, buf_ref):
    @pl.when(pl.program_id(1) == 0)
    def _():
        st_ref[...] = jnp.zeros_like(st_ref)
        halo_ref[...] = jnp.zeros_like(halo_ref)

    nhp = SSD_HEADS * SSD_HD
    gs = SSD_STATE
    act = _silu(_causal_conv(xbc_ref[...].astype(F32), cw_ref, halo_ref, buf_ref) + cbias_ref[...])
    xs = act[:, :nhp]
    bm = act[:, nhp:nhp + SSD_GROUPS * gs]
    cm = act[:, nhp + SSD_GROUPS * gs:]
    dt_full = _softplus(small_ref[...] + dtb_ref[...])
    cs_full = _dot01_l(tril_ref[...], dt_full * apad_ref[...])
    dt_exp = _dot01_r(dt_full, e64_ref[...])
    cs_exp = _dot01_r(cs_full, e64_ref[...])
    cs_lb = _dot01_r(cs_full, e128_ref[...])
    xdt = xs * dt_exp
    cs_last = cs_exp[TILE - 1:TILE, :]
    xds = (xdt * jnp.exp(cs_last - cs_exp)).astype(BF16)
    ecs = jnp.exp(cs_exp)
    xdt_b = xdt.astype(BF16)

    r = lax.broadcasted_iota(I32, (TILE, TILE), 0)
    c = lax.broadcasted_iota(I32, (TILE, TILE), 1)
    incl = r >= c
    lo = c < HALF
    hpg = SSD_HEADS // SSD_GROUPS
    gw = hpg * SSD_HD
    y_parts = []
    for g in range(SSD_GROUPS):
        bmg = bm[:, g * gs:(g + 1) * gs]
        cmg = cm[:, g * gs:(g + 1) * gs].astype(BF16)
        cbg = _dot_nt(cmg, bmg.astype(BF16))
        for mp in range(hpg // 2):
            pair = (g * hpg) // 2 + mp
            xpair = xdt_b[:, pair * LANE:(pair + 1) * LANE]
            ys = []
            for e in range(2):
                h = 2 * pair + e
                cb = cs_lb[:, h * LANE:(h + 1) * LANE]
                lm = jnp.where(incl, jnp.exp(jnp.where(incl, cb - cb.T, 0.0)), 0.0)
                ys.append(_dot((cbg * lm).astype(BF16), xpair))
            y_parts.append(jnp.where(lo, ys[0], ys[1]))
        st = st_ref[g]
        y_off = _dot(cmg, st.astype(BF16)) * ecs[:, g * gw:(g + 1) * gw]
        y_parts[-(hpg // 2):] = [yp + y_off[:, k * LANE:(k + 1) * LANE]
                                 for k, yp in enumerate(y_parts[-(hpg // 2):])]
        new = _dot(bmg.T.astype(BF16), xds[:, g * gw:(g + 1) * gw])
        st_ref[g] = jnp.exp(cs_last[:, g * gw:(g + 1) * gw]) * st + new
    y = jnp.concatenate(y_parts, axis=1) + xs * dexp_ref[...]
    yz = y * _silu(z_ref[...].astype(F32))
    o_ref[...] = _rms(yz, ng_ref[...]).astype(o_ref.dtype)


def _ssd(p3, sm3, conv_w, conv_b, a_log, dt_bias, d_skip, norm_g):
    b, l, _ = p3.shape
    consts = [conv_w.astype(F32), conv_b.reshape(1, -1).astype(F32), _pad_lanes(dt_bias, SM_DT),
              _pad_lanes(-jnp.exp(a_log.astype(F32)), SM_DT),
              jnp.repeat(d_skip.astype(F32), SSD_HD).reshape(1, -1), norm_g.reshape(1, -1).astype(F32),
              jnp.asarray(np.tril(np.ones((TILE, TILE), np.float32)), BF16),
              jnp.asarray(_np_lane_expand(SM_DT, SSD_HEADS, SSD_HD), BF16),
              jnp.asarray(_np_lane_expand(SM_DT, SSD_HEADS, LANE), BF16)]
    return pl.pallas_call(
        _ssd_kernel,
        grid=(b, l // TILE),
        in_specs=[pl.BlockSpec((None, TILE, BRANCH_W), lambda i, j: (i, j, P_OFF["d_z"] // BRANCH_W)),
                  pl.BlockSpec((None, TILE, SSD_XBC), lambda i, j: (i, j, P_OFF["d_xbc"] // SSD_XBC)),
                  pl.BlockSpec((None, TILE, LANE), lambda i, j: (i, j, 0))]
        + [_full_spec(a.shape) for a in consts],
        out_specs=pl.BlockSpec((None, TILE, BRANCH_W), lambda i, j: (i, j, 0)),
        out_shape=jax.ShapeDtypeStruct((b, l, BRANCH_W), BF16),
        scratch_shapes=[pltpu.VMEM((SSD_GROUPS, SSD_STATE, (SSD_HEADS // SSD_GROUPS) * SSD_HD), F32),
                        pltpu.VMEM((8, SSD_XBC), F32), pltpu.VMEM((TILE + 8, SSD_XBC), F32)],
        compiler_params=pltpu.CompilerParams(dimension_semantics=("arbitrary", "arbitrary"),
                                             vmem_limit_bytes=VMEM_LIMIT),
        name="ssd",
    )(p3, p3, sm3, *consts)


def _mem_kv_kernel(mem_ref, g_ref, w_ref, kg_ref, k_ref, v_ref):
    kv = _dot(_rms(mem_ref[...], g_ref[...]).astype(BF16), w_ref[...])
    for h in range(MEM_HEADS):
        hs = slice(h * MEM_HD, (h + 1) * MEM_HD)
        k_ref[:, hs] = _rms(kv[:, hs], kg_ref[...]).astype(k_ref.dtype)
    v_ref[...] = kv[:, BRANCH_W:].astype(v_ref.dtype)


def _mem_kv(mem, mem_norm_g, w_kv_bf16, k_norm_g):
    b, m, _ = mem.shape
    consts = [mem_norm_g.reshape(1, -1).astype(F32), w_kv_bf16, k_norm_g.reshape(1, -1).astype(F32)]
    out = jax.ShapeDtypeStruct((b, m, BRANCH_W), BF16)
    return pl.pallas_call(
        _mem_kv_kernel,
        grid=(b,),
        in_specs=[pl.BlockSpec((None, m, D_MODEL), lambda i: (i, 0, 0))] + [_full_spec(a.shape) for a in consts],
        out_specs=[pl.BlockSpec((None, m, BRANCH_W), lambda i: (i, 0, 0))] * 2,
        out_shape=[out, out],
        compiler_params=pltpu.CompilerParams(dimension_semantics=("arbitrary",), vmem_limit_bytes=VMEM_LIMIT),
        name="mem_kv",
    )(mem, *consts)


def _mem_attn_kernel(q_ref, gate_ref, k_ref, v_ref, qg_ref, o_ref):
    q = q_ref[...].astype(F32)
    outs = []
    for h in range(MEM_HEADS):
        hs = slice(h * MEM_HD, (h + 1) * MEM_HD)
        qn = _rms(q[:, hs], qg_ref[...]).astype(BF16)
        lg = _dot_nt(qn, k_ref[:, hs]) * (MEM_HD ** -0.5)
        p = jnp.exp(lg - jnp.max(lg, axis=1, keepdims=True))
        outs.append(_dot(p.astype(BF16), v_ref[:, hs]) / jnp.sum(p, axis=1, keepdims=True))
    o_ref[...] = (jnp.concatenate(outs, axis=1) * _silu(gate_ref[...].astype(F32))).astype(o_ref.dtype)


def _mem_attn(p3, k, v, q_norm_g):
    b, l, _ = p3.shape
    m = k.shape[1]
    tm = min(512, l)
    qg = q_norm_g.reshape(1, -1).astype(F32)
    blk = lambda name: pl.BlockSpec((None, tm, BRANCH_W), lambda i, j: (i, j, P_OFF[name] // BRANCH_W))
    kv = pl.BlockSpec((None, m, BRANCH_W), lambda i, j: (i, 0, 0))
    return pl.pallas_call(
        _mem_attn_kernel,
        grid=(b, l // tm),
        in_specs=[blk("m_q"), blk("m_gate"), kv, kv, _full_spec(qg.shape)],
        out_specs=pl.BlockSpec((None, tm, BRANCH_W), lambda i, j: (i, j, 0)),
        out_shape=jax.ShapeDtypeStruct((b, l, BRANCH_W), BF16),
        compiler_params=pltpu.CompilerParams(dimension_semantics=("arbitrary", "arbitrary"),
                                             vmem_limit_bytes=VMEM_LIMIT),
        name="mem_attn",
    )(p3, p3, k, v, qg)


def _reordered_w_in(w):
    cols = jnp.asarray(np.maximum(_P_COLUMNS, 0))
    keep = jnp.asarray((_P_COLUMNS >= 0).astype(np.float32))
    return (jnp.take(w, cols, axis=1) * keep).astype(BF16)


def kernel(x, mem, norm_g, w_in, gdn_conv_w, gdn_a_log, gdn_dt_bias, gdn_norm_g, sg_ln_g, sg_ln_b, sg_w, sg_b, dsa_q_norm_g, dsa_k_norm_g, ssd_conv_w, ssd_conv_b, ssd_a_log, ssd_dt_bias, ssd_d, ssd_norm_g, mem_norm_g, w_mem_kv, mem_q_norm_g, mem_k_norm_g, w_gate, w_branch, w_out):
    b, l, d = x.shape
    depth = norm_g.shape[0]
    x2d = x.reshape(b * l, d)
    for i in range(depth):
        g = norm_g[i].reshape(1, d).astype(F32)
        p2d, sm2d = _in_proj(x2d, g, _reordered_w_in(w_in[i]))
        p3 = p2d.reshape(b, l, P_MAIN)
        sm3 = sm2d.reshape(b, l, LANE)
        mk, mv = _mem_kv(mem, mem_norm_g[i], w_mem_kv[i].astype(BF16), mem_k_norm_g[i])
        ys = (
            _gdn(p3, sm3, gdn_conv_w[i], gdn_a_log[i], gdn_dt_bias[i], gdn_norm_g[i]),
            _sg(p3, sg_ln_g[i], sg_ln_b[i], sg_w[i], sg_b[i]),
            _dsa(p3, sm3, dsa_q_norm_g[i], dsa_k_norm_g[i]),
            _ssd(p3, sm3, ssd_conv_w[i], ssd_conv_b[i], ssd_a_log[i], ssd_dt_bias[i], ssd_d[i], ssd_norm_g[i]),
            _mem_attn(p3, mk, mv, mem_q_norm_g[i]),
        )
        ys = [y.reshape(b * l, BRANCH_W) for y in ys]
        x2d = _merge(x2d, g, ys, w_gate[i].astype(BF16), w_branch[i].astype(BF16), w_out[i].astype(BF16))
    return x2d.reshape(b, l, d)
```

```python
import functools

import numpy as np
import jax
import jax.numpy as jnp
from jax import lax
from jax.experimental import pallas as pl
from jax.experimental.pallas import tpu as pltpu

F32 = jnp.float32
BF16 = jnp.bfloat16
I32 = jnp.int32

D_MODEL = 1024
N_BRANCH = 5
BRANCH_W = D_MODEL // 2
CONV_K = 4
EPS = 1e-6
GDN_HEADS = 4
GDN_DK = BRANCH_W // GDN_HEADS
GDN_CHUNK = 64
GDN_QKV = 3 * BRANCH_W
SG_CHUNK = 128
SG_GROUPS = 4
DSA_HEADS = 8
DSA_HD = BRANCH_W // DSA_HEADS
IDX_HEADS = 8
IDX_HD = 64
DSA_TOPK_MAX = 256
Q_BLOCK = 128
SSD_HEADS = 8
SSD_HD = BRANCH_W // SSD_HEADS
SSD_GROUPS = 2
SSD_STATE = 128
SSD_CHUNK = 128
SSD_XBC = SSD_HEADS * SSD_HD + 2 * SSD_GROUPS * SSD_STATE
MEM_HEADS = 4
MEM_HD = BRANCH_W // MEM_HEADS

IN_SPLITS = (
    GDN_QKV, GDN_HEADS, GDN_HEADS, BRANCH_W,
    BRANCH_W, BRANCH_W, BRANCH_W,
    DSA_HEADS * DSA_HD, DSA_HD, DSA_HD, IDX_HEADS * IDX_HD, IDX_HD, IDX_HEADS, BRANCH_W,
    BRANCH_W, SSD_XBC, SSD_HEADS,
    MEM_HEADS * MEM_HD, BRANCH_W,
)
_SEG_NAMES = ("a_qkv", "a_a", "a_b", "a_gate", "b_u", "b_v", "b_gate", "c_q", "c_k", "c_v", "c_iq", "c_ik",
              "c_iw", "c_gate", "d_z", "d_xbc", "d_dt", "m_q", "m_gate")
_SEG_START = dict(zip(_SEG_NAMES, np.concatenate([[0], np.cumsum(IN_SPLITS)[:-1]]).tolist()))
_SEG_WIDTH = dict(zip(_SEG_NAMES, IN_SPLITS))

LANE = 128
HALF = LANE // 2
TILE = 128
VMEM_LIMIT = 56 * 1024 * 1024

_P_LAYOUT = (("a_qkv", 1536), ("b_u", 512), ("d_xbc", 1024), ("a_gate", 512), ("b_v", 512), ("b_gate", 512),
             ("c_q", 512), ("c_iq", 512), ("c_gate", 512), ("d_z", 512), ("m_q", 512), ("m_gate", 512),
             ("c_kk", 128), ("c_vv", 128), ("c_ikik", 128), ("small", 128))
P_OFF = {}
_o = 0
for _n, _w in _P_LAYOUT:
    P_OFF[_n] = _o
    _o += _w
P_COLS = _o
P_MAIN = P_OFF["small"]
SM_A, SM_B, SM_DT, SM_IW = 0, 4, 8, 16


def _p_columns():
    cols = []
    for name, width in _P_LAYOUT:
        if name in _SEG_START:
            cols += list(range(_SEG_START[name], _SEG_START[name] + width))
        elif name in ("c_kk", "c_vv", "c_ikik"):
            src = {"c_kk": "c_k", "c_vv": "c_v", "c_ikik": "c_ik"}[name]
            one = list(range(_SEG_START[src], _SEG_START[src] + _SEG_WIDTH[src]))
            cols += one + one
        else:
            sm = [-1] * LANE
            for seg, at in (("a_a", SM_A), ("a_b", SM_B), ("d_dt", SM_DT), ("c_iw", SM_IW)):
                for j in range(_SEG_WIDTH[seg]):
                    sm[at + j] = _SEG_START[seg] + j
            cols += sm
    return np.asarray(cols, np.int32)


_P_COLUMNS = _p_columns()


def _dot(a, b, precision=None):
    return jnp.dot(a, b, preferred_element_type=F32, precision=precision)


def _dot_nt(a, b):
    return lax.dot_general(a, b, (((1,), (1,)), ((), ())), preferred_element_type=F32)


def _split3(x):
    hi = x.astype(BF16)
    r1 = x - hi.astype(F32)
    mid = r1.astype(BF16)
    lo = (r1 - mid.astype(F32)).astype(BF16)
    return hi, mid, lo


def _dot01_r(x, e):
    hi, mid, lo = _split3(x)
    return _dot(hi, e) + _dot(mid, e) + _dot(lo, e)


def _dot01_l(e, x):
    hi, mid, lo = _split3(x)
    return _dot(e, hi) + _dot(e, mid) + _dot(e, lo)


def _sigmoid(x):
    return 1.0 / (1.0 + jnp.exp(-x))


def _silu(x):
    return x * _sigmoid(x)


def _softplus(x):
    return jnp.maximum(x, 0.0) + jnp.log1p(jnp.exp(-jnp.abs(x)))


def _gelu_tanh(x):
    return 0.5 * x * (1.0 + jnp.tanh(np.sqrt(2.0 / np.pi).astype(np.float32) * (x + 0.044715 * (x * x * x))))


def _rms(x, g, eps=EPS):
    return x * lax.rsqrt(jnp.mean(x * x, axis=-1, keepdims=True) + eps) * g


def _full_spec(shape):
    nd = len(shape)
    return pl.BlockSpec(shape, lambda *_: (0,) * nd)


def _in_proj_kernel(x_ref, g_ref, w_ref, o_ref, s_ref):
    h = _rms(x_ref[...], g_ref[...]).astype(BF16)
    step = 1024
    for c0 in range(0, P_MAIN, step):
        c1 = min(c0 + step, P_MAIN)
        o_ref[:, c0:c1] = _dot(h, w_ref[:, c0:c1]).astype(o_ref.dtype)
    s_ref[...] = _dot(h, w_ref[:, P_MAIN:])


def _in_proj(x2d, g, w_bf16):
    m = x2d.shape[0]
    tm = min(512, m)
    return pl.pallas_call(
        _in_proj_kernel,
        grid=(m // tm,),
        in_specs=[pl.BlockSpec((tm, D_MODEL), lambda i: (i, 0)),
                  pl.BlockSpec((1, D_MODEL), lambda i: (0, 0)),
                  pl.BlockSpec((D_MODEL, P_COLS), lambda i: (0, 0), pipeline_mode=pl.Buffered(1))],
        out_specs=[pl.BlockSpec((tm, P_MAIN), lambda i: (i, 0)), pl.BlockSpec((tm, LANE), lambda i: (i, 0))],
        out_shape=[jax.ShapeDtypeStruct((m, P_MAIN), BF16), jax.ShapeDtypeStruct((m, LANE), F32)],
        compiler_params=pltpu.CompilerParams(dimension_semantics=("arbitrary",), vmem_limit_bytes=VMEM_LIMIT),
        name="in_proj",
    )(x2d, g, w_bf16)


def _merge_kernel(x_ref, g_ref, y0, y1, y2, y3, y4, wg_ref, wb_ref, wo_ref, o_ref):
    x = x_ref[...]
    h = _rms(x, g_ref[...]).astype(BF16)
    acc = None
    for p, y in enumerate((y0, y1, y2, y3, y4)):
        term = _sigmoid(_dot(h, wg_ref[p])) * _dot(y[...], wb_ref[p])
        acc = term if acc is None else acc + term
    o_ref[...] = x + _dot(acc.astype(BF16), wo_ref[...])


def _merge(x2d, g, ys, wg, wb, wo):
    m = x2d.shape[0]
    tm = min(512, m)
    row = lambda i: (i, 0)
    const = pl.Buffered(1)
    return pl.pallas_call(
        _merge_kernel,
        grid=(m // tm,),
        in_specs=[pl.BlockSpec((tm, D_MODEL), row), pl.BlockSpec((1, D_MODEL), lambda i: (0, 0))]
        + [pl.BlockSpec((tm, BRANCH_W), row)] * N_BRANCH
        + [pl.BlockSpec((N_BRANCH, D_MODEL, D_MODEL), lambda i: (0, 0, 0), pipeline_mode=const),
           pl.BlockSpec((N_BRANCH, BRANCH_W, D_MODEL), lambda i: (0, 0, 0), pipeline_mode=const),
           pl.BlockSpec((D_MODEL, D_MODEL), lambda i: (0, 0), pipeline_mode=const)],
        out_specs=pl.BlockSpec((tm, D_MODEL), row),
        out_shape=jax.ShapeDtypeStruct((m, D_MODEL), F32),
        compiler_params=pltpu.CompilerParams(dimension_semantics=("arbitrary",), vmem_limit_bytes=VMEM_LIMIT),
        name="merge_out",
    )(x2d, g, *ys, wg, wb, wo)


def _causal_conv(x, w_ref, halo_ref, buf_ref):
    buf_ref[0:8, :] = halo_ref[...]
    buf_ref[8:8 + TILE, :] = x
    y = w_ref[CONV_K - 1:CONV_K, :] * x
    for j in range(CONV_K - 1):
        y = y + w_ref[j:j + 1, :] * buf_ref[8 - (CONV_K - 1) + j:8 - (CONV_K - 1) + j + TILE, :]
    halo_ref[...] = x[TILE - 8:TILE, :]
    return y


def _np_lane_expand(first_row, n_heads, width):
    e = np.zeros((LANE, n_heads * width), np.float32)
    for h in range(n_heads):
        e[first_row + h, h * width:(h + 1) * width] = 1.0
    return e


def _pad_lanes(vec, at):
    out = jnp.zeros((1, LANE), F32)
    return lax.dynamic_update_slice(out, vec.reshape(1, -1).astype(F32), (0, at))


def _gdn_kernel(qkv_ref, small_ref, gate_ref, cw_ref, apad_ref, dtb_ref, ng_ref, btril_ref, eg_ref, eb_ref,
                o_ref, st_ref, halo_ref, buf_ref):
    @pl.when(pl.program_id(1) == 0)
    def _():
        st_ref[...] = jnp.zeros_like(st_ref)
        halo_ref[...] = jnp.zeros_like(halo_ref)

    act = _silu(_causal_conv(qkv_ref[...].astype(F32), cw_ref, halo_ref, buf_ref))
    small = small_ref[...]
    g_full = -apad_ref[...] * _softplus(small + dtb_ref[...])
    beta_full = _sigmoid(small)
    gc_full = _dot01_l(btril_ref[...], g_full)
    g_lb = _dot01_r(gc_full, eg_ref[...])
    b_lb = _dot01_r(beta_full, eb_ref[...])

    r = lax.broadcasted_iota(I32, (TILE, TILE), 0)
    c = lax.broadcasted_iota(I32, (TILE, TILE), 1)
    same = (r >> 6) == (c >> 6)
    incl = same & (r >= c)
    strict = same & (r > c)
    eye = (r == c).astype(F32)
    n_sub = TILE // GDN_CHUNK
    ng = ng_ref[...]
    heads = range(GDN_HEADS)
    q, k, v, cb, bb = [], [], [], [], []
    for h in heads:
        qh = act[:, h * GDN_DK:(h + 1) * GDN_DK]
        kh = act[:, BRANCH_W + h * GDN_DK:BRANCH_W + (h + 1) * GDN_DK]
        q.append(qh * lax.rsqrt(jnp.sum(qh * qh, axis=-1, keepdims=True) + EPS) * (GDN_DK ** -0.5))
        k.append(kh * lax.rsqrt(jnp.sum(kh * kh, axis=-1, keepdims=True) + EPS))
        v.append(act[:, 2 * BRANCH_W + h * GDN_DK:2 * BRANCH_W + (h + 1) * GDN_DK])
        cb.append(g_lb[:, h * LANE:(h + 1) * LANE])
        bb.append(b_lb[:, h * LANE:(h + 1) * LANE])
    dm = [jnp.where(incl, jnp.exp(jnp.where(incl, cb[h] - cb[h].T, 0.0)), 0.0) for h in heads]
    kb = [k[h].astype(BF16) for h in heads]
    kk = [_dot_nt(kb[h], kb[h]) for h in heads]
    qkb = [(_dot_nt(q[h].astype(BF16), kb[h]) * dm[h]).astype(BF16) for h in heads]
    mp = [jnp.where(strict, -(bb[h] * kk[h] * dm[h]), 0.0) for h in heads]
    inv = [eye + mp[h] for h in heads]
    for _ in range(int(np.log2(GDN_CHUNK)) - 1):
        mpb = [mp[h].astype(BF16) for h in heads]
        mp = [_dot(mpb[h], mpb[h]) for h in heads]
        inv = [inv[h] + _dot(inv[h].astype(BF16), mp[h].astype(BF16)) for h in heads]
    eg = [jnp.exp(cb[h]) for h in heads]
    sol = [_dot(inv[h].astype(BF16),
                jnp.concatenate([bb[h] * eg[h] * k[h], bb[h] * v[h]], axis=1).astype(BF16)) for h in heads]
    wk = [sol[h][:, :GDN_DK].astype(BF16) for h in heads]
    u0 = [sol[h][:, GDN_DK:] for h in heads]
    qd = [(q[h] * eg[h]).astype(BF16) for h in heads]
    zero = jnp.zeros((GDN_CHUNK, LANE), F32)
    st = [st_ref[h] for h in heads]
    o_parts = [[] for _ in heads]
    for cc in range(n_sub):
        rows = slice(cc * GDN_CHUNK, (cc + 1) * GDN_CHUNK)
        in_chunk = (r >> 6) == cc
        glast = [cb[h][(cc + 1) * GDN_CHUNK - 1:(cc + 1) * GDN_CHUNK, :] for h in heads]
        stb = [st[h].astype(BF16) for h in heads]
        u = [u0[h][rows] - _dot(wk[h][rows], stb[h]) for h in heads]
        upad = [jnp.concatenate([u[h] if j == cc else zero for j in range(n_sub)], axis=0).astype(BF16)
                for h in heads]
        for h in heads:
            o_parts[h].append(_dot(qd[h][rows], stb[h]) + _dot(qkb[h][rows], upad[h]))
        kd = [jnp.where(in_chunk, k[h] * jnp.exp(jnp.where(in_chunk, glast[h] - cb[h], 0.0)), 0.0) for h in heads]
        st = [jnp.exp(glast[h]) * st[h] + _dot(kd[h].T.astype(BF16), upad[h]) for h in heads]
    for h in heads:
        st_ref[h] = st[h]
    outs = [_rms(jnp.concatenate(o_parts[h], axis=0), ng) for h in heads]
    y = jnp.concatenate(outs, axis=1) * _silu(gate_ref[...].astype(F32))
    o_ref[...] = y.astype(o_ref.dtype)


def _gdn(p3, sm3, conv_w, a_log, dt_bias, norm_g):
    b, l, _ = p3.shape
    btril = np.zeros((TILE, TILE), np.float32)
    for i in range(TILE):
        lo = (i // GDN_CHUNK) * GDN_CHUNK
        btril[i, lo:i + 1] = 1.0
    consts = [conv_w.astype(F32), _pad_lanes(jnp.exp(a_log.astype(F32)), SM_A), _pad_lanes(dt_bias, SM_A),
              norm_g.reshape(1, -1).astype(F32), jnp.asarray(btril, BF16),
              jnp.asarray(_np_lane_expand(SM_A, GDN_HEADS, LANE), BF16),
              jnp.asarray(_np_lane_expand(SM_B, GDN_HEADS, LANE), BF16)]
    return pl.pallas_call(
        _gdn_kernel,
        grid=(b, l // TILE),
        in_specs=[pl.BlockSpec((None, TILE, GDN_QKV), lambda i, j: (i, j, P_OFF["a_qkv"] // GDN_QKV)),
                  pl.BlockSpec((None, TILE, LANE), lambda i, j: (i, j, 0)),
                  pl.BlockSpec((None, TILE, BRANCH_W), lambda i, j: (i, j, P_OFF["a_gate"] // BRANCH_W))]
        + [_full_spec(a.shape) for a in consts],
        out_specs=pl.BlockSpec((None, TILE, BRANCH_W), lambda i, j: (i, j, 0)),
        out_shape=jax.ShapeDtypeStruct((b, l, BRANCH_W), BF16),
        scratch_shapes=[pltpu.VMEM((GDN_HEADS, GDN_DK, GDN_DK), F32), pltpu.VMEM((8, GDN_QKV), F32),
                        pltpu.VMEM((TILE + 8, GDN_QKV), F32)],
        compiler_params=pltpu.CompilerParams(dimension_semantics=("arbitrary", "arbitrary"),
                                             vmem_limit_bytes=VMEM_LIMIT),
        name="gdn",
    )(p3, sm3, p3, *consts)


def _sg_kernel(u_ref, v_ref, gate_ref, lng_ref, lnb_ref, w_ref, bs_ref, o_ref):
    u = _gelu_tanh(u_ref[...].astype(F32))
    v = _gelu_tanh(v_ref[...].astype(F32))
    mu = jnp.mean(v, axis=-1, keepdims=True)
    var = jnp.mean(jnp.square(v - mu), axis=-1, keepdims=True)
    vn = ((v - mu) * lax.rsqrt(var + 1e-5) * lng_ref[...] + lnb_ref[...]).astype(BF16)
    r = lax.broadcasted_iota(I32, (SG_CHUNK, SG_CHUNK), 0)
    c = lax.broadcasted_iota(I32, (SG_CHUNK, SG_CHUNK), 1)
    gw = BRANCH_W // SG_GROUPS
    mixed = []
    for g in range(SG_GROUPS):
        wc = jnp.where(r >= c, w_ref[g], 0.0).astype(BF16)
        mixed.append(_dot(wc, vn[:, g * gw:(g + 1) * gw]))
    mixed = jnp.concatenate(mixed, axis=1) + bs_ref[...]
    o_ref[...] = (u * mixed * _silu(gate_ref[...].astype(F32))).astype(o_ref.dtype)


def _sg(p3, ln_g, ln_b, w_s, b_s):
    b, l, _ = p3.shape
    gw = BRANCH_W // SG_GROUPS
    bs_full = jnp.repeat(jnp.swapaxes(b_s, 0, 1).astype(F32), gw, axis=1)
    consts = [ln_g.reshape(1, -1).astype(F32), ln_b.reshape(1, -1).astype(F32), w_s.astype(F32), bs_full]
    blk = lambda name: pl.BlockSpec((None, SG_CHUNK, BRANCH_W), lambda i, j: (i, j, P_OFF[name] // BRANCH_W))
    return pl.pallas_call(
        _sg_kernel,
        grid=(b, l // SG_CHUNK),
        in_specs=[blk("b_u"), blk("b_v"), blk("b_gate")] + [_full_spec(a.shape) for a in consts],
        out_specs=pl.BlockSpec((None, SG_CHUNK, BRANCH_W), lambda i, j: (i, j, 0)),
        out_shape=jax.ShapeDtypeStruct((b, l, BRANCH_W), BF16),
        compiler_params=pltpu.CompilerParams(dimension_semantics=("arbitrary", "arbitrary"),
                                             vmem_limit_bytes=VMEM_LIMIT),
        name="spatial_gating",
    )(p3, p3, p3, *consts)


_NEG = -1e30
_INT_MIN = -2 ** 31


def _fold8(x, op):
    parts = [x[r:r + 8] for r in range(0, x.shape[0], 8)]
    while len(parts) > 1:
        parts = [op(parts[k], parts[k + 1]) for k in range(0, len(parts) - 1, 2)] + (
            [parts[-1]] if len(parts) % 2 else [])
    return parts[0]


KEY_CHUNK = 2 * TILE


def _bit_planes(u):
    a = [u[8 * g:8 * g + 8] for g in range(32)]
    j, m = 16, 0x0000FFFF
    while j:
        k = 0
        while k < 32:
            t = (a[k] ^ (a[k + j] >> j)) & m
            a[k] = a[k] ^ t
            a[k + j] = a[k + j] ^ (t << j)
            k = (k + j + 1) & ~j
        j >>= 1
        m = m ^ ((m << j) & 0xFFFFFFFF)
    return [a[31 - b] for b in range(32)]


def _dsa_kernel(q_ref, iq_ref, small_ref, gate_ref, kk_ref, vv_ref, ik_ref, qg_ref, kg_ref, sl_ref,
                o_ref, kn_ref, vo_ref, ikb_ref, key_ref, plane_ref, lg_ref, tau_ref, acc_ref, mx_ref, *, top_k):
    i = pl.program_id(1)
    nh = DSA_HEADS
    nc = key_ref.shape[0]
    kc = KEY_CHUNK

    srow = lax.broadcasted_iota(I32, (kc, LANE), 0)
    lane = lax.broadcasted_iota(I32, (kc, LANE), 1)
    qpos = i * TILE + lane
    pos_lanes = (lane == HALF) | (lane == HALF + 1)

    @pl.when(i == 0)
    def _():
        def prep(jc, carry):
            rows = pl.ds(pl.multiple_of(jc * kc, kc), kc)
            kpos = jnp.where(lane == HALF, srow, jc * kc).astype(F32)
            kn = jnp.where(lane < HALF, _rms(kk_ref[rows, :].astype(F32), kg_ref[...]), jnp.where(pos_lanes, kpos, 0.0))
            kn_ref[jc] = kn.astype(BF16)
            vo_ref[jc] = jnp.where(lane < HALF, vv_ref[rows, :], jnp.ones((kc, LANE), BF16))
            ikb_ref[jc] = ik_ref[rows, :]
            return carry
        lax.fori_loop(0, nc, prep, 0)

    n_ch = (i + 2) // 2
    n_ch2 = (n_ch + 1) // 2

    @pl.when(n_ch < nc)
    def _():
        key_ref[n_ch] = jnp.full((kc, LANE), _INT_MIN, I32)

    iq = iq_ref[...].astype(F32)
    lane_q = lax.broadcasted_iota(I32, (TILE, LANE), 1)
    lo = lane_q < HALF
    pos_lanes_q = (lane_q == HALF) | (lane_q == HALF + 1)
    iq_st = jnp.concatenate(
        [jnp.where(lo if h % 2 == 0 else ~lo, iq[:, (h // 2) * LANE:(h // 2 + 1) * LANE], 0.0) for h in range(nh)],
        axis=0).astype(BF16)
    small_t = small_ref[...].T * (IDX_HEADS ** -0.5 * IDX_HD ** -0.5)
    iw_row = jnp.concatenate([small_t[SM_IW + h:SM_IW + h + 1, :] for h in range(nh)], axis=1)

    def score_pair(j2, carry):
        jcs = (2 * j2, 2 * j2 + 1)
        raw = [_dot_nt(ikb_ref[jc], iq_st) for jc in jcs]
        for jc, s_raw in zip(jcs, raw):
            s_h = jnp.maximum(s_raw, 0.0) * iw_row
            s = s_h[:, 0:TILE]
            for h in range(1, nh):
                s = s + s_h[:, h * TILE:(h + 1) * TILE]
            bits = pltpu.bitcast(s, I32)
            key = jnp.where(bits < 0, bits ^ 0x7FFFFFFF, bits)
            key = jnp.where(jc * kc + srow <= qpos, key, _INT_MIN)
            key_ref[jc] = key
            planes = _bit_planes(key ^ _INT_MIN)
            for b in range(32):
                plane_ref[jc, b] = planes[b]
        return carry

    lax.fori_loop(0, n_ch2, score_pair, 0)

    def count(pred):
        def body(j2, acc):
            for u in range(2):
                acc = acc + _fold8(jnp.where(pred(key_ref[2 * j2 + u]), 1.0, 0.0), jnp.add)
            return acc
        acc = lax.fori_loop(0, n_ch2, body, jnp.zeros((8, LANE), F32))
        return jnp.sum(acc, axis=0, keepdims=True)

    tau_ref[...] = jnp.full(tau_ref.shape, _INT_MIN + 1, I32)

    @pl.when((i + 1) * TILE > top_k)
    def _():
        def ones_at(live, b):
            acc = jnp.zeros((8, LANE), I32)
            for jc in range(nc):
                acc = acc + lax.population_count(live[jc] & plane_ref[jc, b])
            return jnp.sum(acc.astype(F32), axis=0, keepdims=True)

        live0 = tuple(jnp.where(jc < 2 * n_ch2, jnp.full((8, LANE), -1, I32), 0) for jc in range(nc))

        def bit_body(bi, carry):
            live, cnt, left, tau_u = carry
            b = 31 - bi
            take = cnt >= left
            left = jnp.where(take, left, left - cnt)
            tau_u = jnp.where(take, tau_u | lax.shift_left(jnp.int32(1), b), tau_u)
            b_next = jnp.maximum(b - 1, 0)
            acc = jnp.zeros((8, LANE), I32)
            new_live = []
            for jc in range(nc):
                with_bit = live[jc] & plane_ref[jc, b]
                keep = jnp.where(take, with_bit, live[jc] ^ with_bit)
                new_live.append(keep)
                acc = acc + lax.population_count(keep & plane_ref[jc, b_next])
            return tuple(new_live), jnp.sum(acc.astype(F32), axis=0, keepdims=True), left, tau_u

        init = (live0, ones_at(live0, 31), jnp.full((1, LANE), float(top_k), F32), jnp.zeros((1, LANE), I32))
        tau = lax.fori_loop(0, 32, bit_body, init)[3] ^ _INT_MIN
        tau = jnp.where(qpos[0:1, :] + 1 <= top_k, _INT_MIN + 1, tau)
        tau_ref[...] = jnp.broadcast_to(tau, tau_ref.shape)

    tau = tau_ref[0:1, :]
    need = top_k - count(lambda kx: kx > tau)

    q = q_ref[...].astype(F32)
    q_parts = []
    for mpair in range(nh // 2):
        qp = q[:, mpair * LANE:(mpair + 1) * LANE]
        sq = qp * qp
        s_lo = jnp.sum(jnp.where(lo, sq, 0.0), axis=1, keepdims=True)
        s_hi = jnp.sum(jnp.where(lo, 0.0, sq), axis=1, keepdims=True)
        qn = qp * lax.rsqrt(jnp.where(lo, s_lo, s_hi) * (1.0 / DSA_HD) + EPS) * qg_ref[...] * (DSA_HD ** -0.5)
        for e, qh in enumerate((qn, pltpu.roll(qn, HALF, 1))):
            slope = 2.0 ** (-8.0 * (2 * mpair + e + 1) / nh)
            q_parts.append(jnp.where(lo, qh, jnp.where(pos_lanes_q, slope, 0.0)))
    q_st = jnp.concatenate(q_parts, axis=0).astype(BF16)

    mx_ref[...] = jnp.full(mx_ref.shape, _NEG, F32)

    def mask_chunk(jc, run):
        key = key_ref[jc]
        eq = key == tau
        eqf = jnp.where(eq, 1.0, 0.0)
        prefix = _dot(sl_ref[...], eqf.astype(BF16)) + run
        selt = jnp.where((key > tau) | (eq & (prefix < need)), 1.0, 0.0)
        sel = jnp.concatenate([selt[r:r + TILE].T for r in range(0, kc, TILE)], axis=1) > 0.5
        lg = _dot_nt(q_st, kn_ref[jc])
        for h in range(nh):
            rows = slice(h * TILE, (h + 1) * TILE)
            x = jnp.where(sel, lg[rows], _NEG)
            lg_ref[jc, rows, :] = x
            mx_ref[rows, :] = jnp.maximum(mx_ref[rows, :], jnp.maximum(x[:, :LANE], x[:, LANE:]))
        return run + jnp.sum(_fold8(eqf, jnp.add), axis=0, keepdims=True)

    lax.fori_loop(0, 2 * n_ch2, mask_chunk, jnp.zeros((1, LANE), F32))
    mx_ref[...] = jnp.broadcast_to(jnp.max(mx_ref[...], axis=1, keepdims=True), mx_ref.shape)

    acc_ref[...] = jnp.zeros_like(acc_ref)

    def pv_pair(j2, carry):
        pv = None
        for jc in (2 * j2, 2 * j2 + 1):
            ps = []
            for h in range(nh):
                rows = slice(h * TILE, (h + 1) * TILE)
                m = mx_ref[rows, :]
                ps.append(jnp.exp(lg_ref[jc, rows, :] - jnp.concatenate([m, m], axis=1)).astype(BF16))
            term = _dot(jnp.concatenate(ps, axis=0), vo_ref[jc])
            pv = term if pv is None else pv + term
        acc_ref[...] += pv
        return carry

    lax.fori_loop(0, n_ch2, pv_pair, 0)

    pairs = []
    for mp in range(nh // 2):
        even = acc_ref[(2 * mp) * TILE:(2 * mp + 1) * TILE, :]
        odd = acc_ref[(2 * mp + 1) * TILE:(2 * mp + 2) * TILE, :]
        pairs.append(jnp.where(lo, even / pltpu.roll(even, HALF, 1), pltpu.roll(odd, HALF, 1) / odd))
    o_ref[...] = (jnp.concatenate(pairs, axis=1) * _silu(gate_ref[...].astype(F32))).astype(o_ref.dtype)


def _dsa(p3, sm3, q_norm_g, k_norm_g):
    b, l, _ = p3.shape
    top_k = min(DSA_TOPK_MAX, l // 4)
    nb = l // TILE
    nc = l // KEY_CHUNK
    sl = np.tril(np.ones((KEY_CHUNK, KEY_CHUNK), np.float32), -1)
    dup = lambda g: jnp.concatenate([g, g]).reshape(1, LANE).astype(F32)
    consts = [dup(q_norm_g), dup(k_norm_g), jnp.asarray(sl, BF16)]
    qblk = lambda name: pl.BlockSpec((None, TILE, BRANCH_W), lambda i, j: (i, j, P_OFF[name] // BRANCH_W))
    seq = lambda name: pl.BlockSpec((None, l, LANE), lambda i, j: (i, 0, P_OFF[name] // LANE),
                                    pipeline_mode=pl.Buffered(1))
    chunks = lambda dt: pltpu.VMEM((nc, KEY_CHUNK, LANE), dt)
    return pl.pallas_call(
        functools.partial(_dsa_kernel, top_k=top_k),
        grid=(b, nb),
        in_specs=[qblk("c_q"), qblk("c_iq"),
                  pl.BlockSpec((None, TILE, LANE), lambda i, j: (i, j, 0)),
                  qblk("c_gate"), seq("c_kk"), seq("c_vv"), seq("c_ikik")]
        + [_full_spec(a.shape) for a in consts],
        out_specs=pl.BlockSpec((None, TILE, BRANCH_W), lambda i, j: (i, j, 0)),
        out_shape=jax.ShapeDtypeStruct((b, l, BRANCH_W), BF16),
        scratch_shapes=[chunks(BF16), chunks(BF16), chunks(BF16), chunks(I32),
                        pltpu.VMEM((nc, 32, 8, LANE), I32),
                        pltpu.VMEM((nc, DSA_HEADS * TILE, KEY_CHUNK), F32), pltpu.VMEM((8, LANE), I32),
                        pltpu.VMEM((DSA_HEADS * TILE, LANE), F32), pltpu.VMEM((DSA_HEADS * TILE, LANE), F32)],
        compiler_params=pltpu.CompilerParams(dimension_semantics=("arbitrary", "arbitrary"),
                                             vmem_limit_bytes=VMEM_LIMIT),
        name="dsa",
    )(p3, p3, sm3, p3, p3, p3, p3, *consts)


def _ssd_kernel(z_ref, xbc_ref, small_ref, cw_ref, cbias_ref, dtb_ref, apad_ref, dexp_ref, ng_ref, tril_ref,
                e64_ref, e128_ref, o_ref, st_ref, halo_ref, buf_ref):
    @pl.when(pl.program_id(1) == 0)
    def _():
        st_ref[...] = jnp.zeros_like(st_ref)
        halo_ref[...] = jnp.zeros_like(halo_ref)

    nhp = SSD_HEADS * SSD_HD
    gs = SSD_STATE
    act = _silu(_causal_conv(xbc_ref[...].astype(F32), cw_ref, halo_ref, buf_ref) + cbias_ref[...])
    xs = act[:, :nhp]
    bm = act[:, nhp:nhp + SSD_GROUPS * gs]
    cm = act[:, nhp + SSD_GROUPS * gs:]
    dt_full = _softplus(small_ref[...] + dtb_ref[...])
    cs_full = _dot01_l(tril_ref[...], dt_full * apad_ref[...])
    dt_exp = _dot01_r(dt_full, e64_ref[...])
    cs_exp = _dot01_r(cs_full, e64_ref[...])
    cs_lb = _dot01_r(cs_full, e128_ref[...])
    xdt = xs * dt_exp
    cs_last = cs_exp[TILE - 1:TILE, :]
    xds = (xdt * jnp.exp(cs_last - cs_exp)).astype(BF16)
    ecs = jnp.exp(cs_exp)
    xdt_b = xdt.astype(BF16)

    r = lax.broadcasted_iota(I32, (TILE, TILE), 0)
    c = lax.broadcasted_iota(I32, (TILE, TILE), 1)
    incl = r >= c
    lo = c < HALF
    hpg = SSD_HEADS // SSD_GROUPS
    gw = hpg * SSD_HD
    y_parts = []
    for g in range(SSD_GROUPS):
        bmg = bm[:, g * gs:(g + 1) * gs]
        cmg = cm[:, g * gs:(g + 1) * gs].astype(BF16)
        cbg = _dot_nt(cmg, bmg.astype(BF16))
        for mp in range(hpg // 2):
            pair = (g * hpg) // 2 + mp
            xpair = xdt_b[:, pair * LANE:(pair + 1) * LANE]
            ys = []
            for e in range(2):
                h = 2 * pair + e
                cb = cs_lb[:, h * LANE:(h + 1) * LANE]
                lm = jnp.where(incl, jnp.exp(jnp.where(incl, cb - cb.T, 0.0)), 0.0)
                ys.append(_dot((cbg * lm).astype(BF16), xpair))
            y_parts.append(jnp.where(lo, ys[0], ys[1]))
        st = st_ref[g]
        y_off = _dot(cmg, st.astype(BF16)) * ecs[:, g * gw:(g + 1) * gw]
        y_parts[-(hpg // 2):] = [yp + y_off[:, k * LANE:(k + 1) * LANE]
                                 for k, yp in enumerate(y_parts[-(hpg // 2):])]
        new = _dot(bmg.T.astype(BF16), xds[:, g * gw:(g + 1) * gw])
        st_ref[g] = jnp.exp(cs_last[:, g * gw:(g + 1) * gw]) * st + new
    y = jnp.concatenate(y_parts, axis=1) + xs * dexp_ref[...]
    yz = y * _silu(z_ref[...].astype(F32))
    o_ref[...] = _rms(yz, ng_ref[...]).astype(o_ref.dtype)


def _ssd(p3, sm3, conv_w, conv_b, a_log, dt_bias, d_skip, norm_g):
    b, l, _ = p3.shape
    consts = [conv_w.astype(F32), conv_b.reshape(1, -1).astype(F32), _pad_lanes(dt_bias, SM_DT),
              _pad_lanes(-jnp.exp(a_log.astype(F32)), SM_DT),
              jnp.repeat(d_skip.astype(F32), SSD_HD).reshape(1, -1), norm_g.reshape(1, -1).astype(F32),
              jnp.asarray(np.tril(np.ones((TILE, TILE), np.float32)), BF16),
              jnp.asarray(_np_lane_expand(SM_DT, SSD_HEADS, SSD_HD), BF16),
              jnp.asarray(_np_lane_expand(SM_DT, SSD_HEADS, LANE), BF16)]
    return pl.pallas_call(
        _ssd_kernel,
        grid=(b, l // TILE),
        in_specs=[pl.BlockSpec((None, TILE, BRANCH_W), lambda i, j: (i, j, P_OFF["d_z"] // BRANCH_W)),
                  pl.BlockSpec((None, TILE, SSD_XBC), lambda i, j: (i, j, P_OFF["d_xbc"] // SSD_XBC)),
                  pl.BlockSpec((None, TILE, LANE), lambda i, j: (i, j, 0))]
        + [_full_spec(a.shape) for a in consts],
        out_specs=pl.BlockSpec((None, TILE, BRANCH_W), lambda i, j: (i, j, 0)),
        out_shape=jax.ShapeDtypeStruct((b, l, BRANCH_W), BF16),
        scratch_shapes=[pltpu.VMEM((SSD_GROUPS, SSD_STATE, (SSD_HEADS // SSD_GROUPS) * SSD_HD), F32),
                        pltpu.VMEM((8, SSD_XBC), F32), pltpu.VMEM((TILE + 8, SSD_XBC), F32)],
        compiler_params=pltpu.CompilerParams(dimension_semantics=("arbitrary", "arbitrary"),
                                             vmem_limit_bytes=VMEM_LIMIT),
        name="ssd",
    )(p3, p3, sm3, *consts)


def _mem_kv_kernel(mem_ref, g_ref, w_ref, kg_ref, k_ref, v_ref):
    kv = _dot(_rms(mem_ref[...], g_ref[...]).astype(BF16), w_ref[...])
    for h in range(MEM_HEADS):
        hs = slice(h * MEM_HD, (h + 1) * MEM_HD)
        k_ref[:, hs] = _rms(kv[:, hs], kg_ref[...]).astype(k_ref.dtype)
    v_ref[...] = kv[:, BRANCH_W:].astype(v_ref.dtype)


def _mem_kv(mem, mem_norm_g, w_kv_bf16, k_norm_g):
    b, m, _ = mem.shape
    consts = [mem_norm_g.reshape(1, -1).astype(F32), w_kv_bf16, k_norm_g.reshape(1, -1).astype(F32)]
    out = jax.ShapeDtypeStruct((b, m, BRANCH_W), BF16)
    return pl.pallas_call(
        _mem_kv_kernel,
        grid=(b,),
        in_specs=[pl.BlockSpec((None, m, D_MODEL), lambda i: (i, 0, 0))] + [_full_spec(a.shape) for a in consts],
        out_specs=[pl.BlockSpec((None, m, BRANCH_W), lambda i: (i, 0, 0))] * 2,
        out_shape=[out, out],
        compiler_params=pltpu.CompilerParams(dimension_semantics=("arbitrary",), vmem_limit_bytes=VMEM_LIMIT),
        name="mem_kv",
    )(mem, *consts)


def _mem_attn_kernel(q_ref, gate_ref, k_ref, v_ref, qg_ref, o_ref):
    q = q_ref[...].astype(F32)
    outs = []
    for h in range(MEM_HEADS):
        hs = slice(h * MEM_HD, (h + 1) * MEM_HD)
        qn = _rms(q[:, hs], qg_ref[...]).astype(BF16)
        lg = _dot_nt(qn, k_ref[:, hs]) * (MEM_HD ** -0.5)
        p = jnp.exp(lg - jnp.max(lg, axis=1, keepdims=True))
        outs.append(_dot(p.astype(BF16), v_ref[:, hs]) / jnp.sum(p, axis=1, keepdims=True))
    o_ref[...] = (jnp.concatenate(outs, axis=1) * _silu(gate_ref[...].astype(F32))).astype(o_ref.dtype)


def _mem_attn(p3, k, v, q_norm_g):
    b, l, _ = p3.shape
    m = k.shape[1]
    tm = min(512, l)
    qg = q_norm_g.reshape(1, -1).astype(F32)
    blk = lambda name: pl.BlockSpec((None, tm, BRANCH_W), lambda i, j: (i, j, P_OFF[name] // BRANCH_W))
    kv = pl.BlockSpec((None, m, BRANCH_W), lambda i, j: (i, 0, 0))
    return pl.pallas_call(
        _mem_attn_kernel,
        grid=(b, l // tm),
        in_specs=[blk("m_q"), blk("m_gate"), kv, kv, _full_spec(qg.shape)],
        out_specs=pl.BlockSpec((None, tm, BRANCH_W), lambda i, j: (i, j, 0)),
        out_shape=jax.ShapeDtypeStruct((b, l, BRANCH_W), BF16),
        compiler_params=pltpu.CompilerParams(dimension_semantics=("arbitrary", "arbitrary"),
                                             vmem_limit_bytes=VMEM_LIMIT),
        name="mem_attn",
    )(p3, p3, k, v, qg)


def _reordered_w_in(w):
    cols = jnp.asarray(np.maximum(_P_COLUMNS, 0))
    keep = jnp.asarray((_P_COLUMNS >= 0).astype(np.float32))
    return (jnp.take(w, cols, axis=1) * keep).astype(BF16)


def kernel(x, mem, norm_g, w_in, gdn_conv_w, gdn_a_log, gdn_dt_bias, gdn_norm_g, sg_ln_g, sg_ln_b, sg_w, sg_b, dsa_q_norm_g, dsa_k_norm_g, ssd_conv_w, ssd_conv_b, ssd_a_log, ssd_dt_bias, ssd_d, ssd_norm_g, mem_norm_g, w_mem_kv, mem_q_norm_g, mem_k_norm_g, w_gate, w_branch, w_out):
    b, l, d = x.shape
    depth = norm_g.shape[0]
    x2d = x.reshape(b * l, d)
    for i in range(depth):
        g = norm_g[i].reshape(1, d).astype(F32)
        p2d, sm2d = _in_proj(x2d, g, _reordered_w_in(w_in[i]))
        p3 = p2d.reshape(b, l, P_MAIN)
        sm3 = sm2d.reshape(b, l, LANE)
        mk, mv = _mem_kv(mem, mem_norm_g[i], w_mem_kv[i].astype(BF16), mem_k_norm_g[i])
        ys = (
            _gdn(p3, sm3, gdn_conv_w[i], gdn_a_log[i], gdn_dt_bias[i], gdn_norm_g[i]),
            _sg(p3, sg_ln_g[i], sg_ln_b[i], sg_w[i], sg_b[i]),
            _dsa(p3, sm3, dsa_q_norm_g[i], dsa_k_norm_g[i]),
            _ssd(p3, sm3, ssd_conv_w[i], ssd_conv_b[i], ssd_a_log[i], ssd_dt_bias[i], ssd_d[i], ssd_norm_g[i]),
            _mem_attn(p3, mk, mv, mem_q_norm_g[i]),
        )
        ys = [y.reshape(b * l, BRANCH_W) for y in ys]
        x2d = _merge(x2d, g, ys, w_gate[i].astype(BF16), w_branch[i].astype(BF16), w_out[i].astype(BF16))
    return x2d.reshape(b, l, d)
```

```python
import functools

import numpy as np
import jax
import jax.numpy as jnp
from jax import lax
from jax.experimental import pallas as pl
from jax.experimental.pallas import tpu as pltpu

F32 = jnp.float32
BF16 = jnp.bfloat16
I32 = jnp.int32

D_MODEL = 1024
N_BRANCH = 5
BRANCH_W = D_MODEL // 2
CONV_K = 4
EPS = 1e-6
GDN_HEADS = 4
GDN_DK = BRANCH_W // GDN_HEADS
GDN_CHUNK = 64
GDN_QKV = 3 * BRANCH_W
SG_CHUNK = 128
SG_GROUPS = 4
DSA_HEADS = 8
DSA_HD = BRANCH_W // DSA_HEADS
IDX_HEADS = 8
IDX_HD = 64
DSA_TOPK_MAX = 256
Q_BLOCK = 128
SSD_HEADS = 8
SSD_HD = BRANCH_W // SSD_HEADS
SSD_GROUPS = 2
SSD_STATE = 128
SSD_CHUNK = 128
SSD_XBC = SSD_HEADS * SSD_HD + 2 * SSD_GROUPS * SSD_STATE
MEM_HEADS = 4
MEM_HD = BRANCH_W // MEM_HEADS

IN_SPLITS = (
    GDN_QKV, GDN_HEADS, GDN_HEADS, BRANCH_W,
    BRANCH_W, BRANCH_W, BRANCH_W,
    DSA_HEADS * DSA_HD, DSA_HD, DSA_HD, IDX_HEADS * IDX_HD, IDX_HD, IDX_HEADS, BRANCH_W,
    BRANCH_W, SSD_XBC, SSD_HEADS,
    MEM_HEADS * MEM_HD, BRANCH_W,
)
_SEG_NAMES = ("a_qkv", "a_a", "a_b", "a_gate", "b_u", "b_v", "b_gate", "c_q", "c_k", "c_v", "c_iq", "c_ik",
              "c_iw", "c_gate", "d_z", "d_xbc", "d_dt", "m_q", "m_gate")
_SEG_START = dict(zip(_SEG_NAMES, np.concatenate([[0], np.cumsum(IN_SPLITS)[:-1]]).tolist()))
_SEG_WIDTH = dict(zip(_SEG_NAMES, IN_SPLITS))

LANE = 128
HALF = LANE // 2
TILE = 128
VMEM_LIMIT = 56 * 1024 * 1024

_P_LAYOUT = (("a_qkv", 1536), ("b_u", 512), ("d_xbc", 1024), ("a_gate", 512), ("b_v", 512), ("b_gate", 512),
             ("c_q", 512), ("c_iq", 512), ("c_gate", 512), ("d_z", 512), ("m_q", 512), ("m_gate", 512),
             ("c_kk", 128), ("c_vv", 128), ("c_ikik", 128), ("small", 128))
P_OFF = {}
_o = 0
for _n, _w in _P_LAYOUT:
    P_OFF[_n] = _o
    _o += _w
P_COLS = _o
P_MAIN = P_OFF["small"]
SM_A, SM_B, SM_DT, SM_IW = 0, 4, 8, 16


def _p_columns():
    cols = []
    for name, width in _P_LAYOUT:
        if name in _SEG_START:
            cols += list(range(_SEG_START[name], _SEG_START[name] + width))
        elif name in ("c_kk", "c_vv", "c_ikik"):
            src = {"c_kk": "c_k", "c_vv": "c_v", "c_ikik": "c_ik"}[name]
            one = list(range(_SEG_START[src], _SEG_START[src] + _SEG_WIDTH[src]))
            cols += one + one
        else:
            sm = [-1] * LANE
            for seg, at in (("a_a", SM_A), ("a_b", SM_B), ("d_dt", SM_DT), ("c_iw", SM_IW)):
                for j in range(_SEG_WIDTH[seg]):
                    sm[at + j] = _SEG_START[seg] + j
            cols += sm
    return np.asarray(cols, np.int32)


_P_COLUMNS = _p_columns()


def _dot(a, b, precision=None):
    return jnp.dot(a, b, preferred_element_type=F32, precision=precision)


def _dot_nt(a, b):
    return lax.dot_general(a, b, (((1,), (1,)), ((), ())), preferred_element_type=F32)


def _split3(x):
    hi = x.astype(BF16)
    r1 = x - hi.astype(F32)
    mid = r1.astype(BF16)
    lo = (r1 - mid.astype(F32)).astype(BF16)
    return hi, mid, lo


def _dot01_r(x, e):
    hi, mid, lo = _split3(x)
    return _dot(hi, e) + _dot(mid, e) + _dot(lo, e)


def _dot01_l(e, x):
    hi, mid, lo = _split3(x)
    return _dot(e, hi) + _dot(e, mid) + _dot(e, lo)


def _sigmoid(x):
    return 1.0 / (1.0 + jnp.exp(-x))


def _silu(x):
    return x * _sigmoid(x)


def _softplus(x):
    return jnp.maximum(x, 0.0) + jnp.log1p(jnp.exp(-jnp.abs(x)))


def _gelu_tanh(x):
    return 0.5 * x * (1.0 + jnp.tanh(np.sqrt(2.0 / np.pi).astype(np.float32) * (x + 0.044715 * (x * x * x))))


def _rms(x, g, eps=EPS):
    return x * lax.rsqrt(jnp.mean(x * x, axis=-1, keepdims=True) + eps) * g


def _full_spec(shape):
    nd = len(shape)
    return pl.BlockSpec(shape, lambda *_: (0,) * nd)


def _in_proj_kernel(x_ref, g_ref, w_ref, o_ref, s_ref):
    h = _rms(x_ref[...], g_ref[...]).astype(BF16)
    step = 1024
    for c0 in range(0, P_MAIN, step):
        c1 = min(c0 + step, P_MAIN)
        o_ref[:, c0:c1] = _dot(h, w_ref[:, c0:c1]).astype(o_ref.dtype)
    s_ref[...] = _dot(h, w_ref[:, P_MAIN:])


def _in_proj(x2d, g, w_bf16):
    m = x2d.shape[0]
    tm = min(512, m)
    return pl.pallas_call(
        _in_proj_kernel,
        grid=(m // tm,),
        in_specs=[pl.BlockSpec((tm, D_MODEL), lambda i: (i, 0)),
                  pl.BlockSpec((1, D_MODEL), lambda i: (0, 0)),
                  pl.BlockSpec((D_MODEL, P_COLS), lambda i: (0, 0), pipeline_mode=pl.Buffered(1))],
        out_specs=[pl.BlockSpec((tm, P_MAIN), lambda i: (i, 0)), pl.BlockSpec((tm, LANE), lambda i: (i, 0))],
        out_shape=[jax.ShapeDtypeStruct((m, P_MAIN), BF16), jax.ShapeDtypeStruct((m, LANE), F32)],
        compiler_params=pltpu.CompilerParams(dimension_semantics=("arbitrary",), vmem_limit_bytes=VMEM_LIMIT),
        name="in_proj",
    )(x2d, g, w_bf16)


def _merge_kernel(x_ref, g_ref, y0, y1, y2, y3, y4, wg_ref, wb_ref, wo_ref, o_ref):
    x = x_ref[...]
    h = _rms(x, g_ref[...]).astype(BF16)
    acc = None
    for p, y in enumerate((y0, y1, y2, y3, y4)):
        term = _sigmoid(_dot(h, wg_ref[p])) * _dot(y[...], wb_ref[p])
        acc = term if acc is None else acc + term
    o_ref[...] = x + _dot(acc.astype(BF16), wo_ref[...])


def _merge(x2d, g, ys, wg, wb, wo):
    m = x2d.shape[0]
    tm = min(512, m)
    row = lambda i: (i, 0)
    const = pl.Buffered(1)
    return pl.pallas_call(
        _merge_kernel,
        grid=(m // tm,),
        in_specs=[pl.BlockSpec((tm, D_MODEL), row), pl.BlockSpec((1, D_MODEL), lambda i: (0, 0))]
        + [pl.BlockSpec((tm, BRANCH_W), row)] * N_BRANCH
        + [pl.BlockSpec((N_BRANCH, D_MODEL, D_MODEL), lambda i: (0, 0, 0), pipeline_mode=const),
           pl.BlockSpec((N_BRANCH, BRANCH_W, D_MODEL), lambda i: (0, 0, 0), pipeline_mode=const),
           pl.BlockSpec((D_MODEL, D_MODEL), lambda i: (0, 0), pipeline_mode=const)],
        out_specs=pl.BlockSpec((tm, D_MODEL), row),
        out_shape=jax.ShapeDtypeStruct((m, D_MODEL), F32),
        compiler_params=pltpu.CompilerParams(dimension_semantics=("arbitrary",), vmem_limit_bytes=VMEM_LIMIT),
        name="merge_out",
    )(x2d, g, *ys, wg, wb, wo)


def _causal_conv(x, w_ref, halo_ref, buf_ref):
    buf_ref[0:8, :] = halo_ref[...]
    buf_ref[8:8 + TILE, :] = x
    y = w_ref[CONV_K - 1:CONV_K, :] * x
    for j in range(CONV_K - 1):
        y = y + w_ref[j:j + 1, :] * buf_ref[8 - (CONV_K - 1) + j:8 - (CONV_K - 1) + j + TILE, :]
    halo_ref[...] = x[TILE - 8:TILE, :]
    return y


def _np_lane_expand(first_row, n_heads, width):
    e = np.zeros((LANE, n_heads * width), np.float32)
    for h in range(n_heads):
        e[first_row + h, h * width:(h + 1) * width] = 1.0
    return e


def _pad_lanes(vec, at):
    out = jnp.zeros((1, LANE), F32)
    return lax.dynamic_update_slice(out, vec.reshape(1, -1).astype(F32), (0, at))


def _gdn_kernel(qkv_ref, small_ref, gate_ref, cw_ref, apad_ref, dtb_ref, ng_ref, btril_ref, eg_ref, eb_ref,
                o_ref, st_ref, halo_ref, buf_ref):
    @pl.when(pl.program_id(1) == 0)
    def _():
        st_ref[...] = jnp.zeros_like(st_ref)
        halo_ref[...] = jnp.zeros_like(halo_ref)

    act = _silu(_causal_conv(qkv_ref[...].astype(F32), cw_ref, halo_ref, buf_ref))
    small = small_ref[...]
    g_full = -apad_ref[...] * _softplus(small + dtb_ref[...])
    beta_full = _sigmoid(small)
    gc_full = _dot01_l(btril_ref[...], g_full)
    g_lb = _dot01_r(gc_full, eg_ref[...])
    b_lb = _dot01_r(beta_full, eb_ref[...])

    r = lax.broadcasted_iota(I32, (TILE, TILE), 0)
    c = lax.broadcasted_iota(I32, (TILE, TILE), 1)
    same = (r >> 6) == (c >> 6)
    incl = same & (r >= c)
    strict = same & (r > c)
    eye = (r == c).astype(F32)
    n_sub = TILE // GDN_CHUNK
    ng = ng_ref[...]
    heads = range(GDN_HEADS)
    q, k, v, cb, bb = [], [], [], [], []
    for h in heads:
        qh = act[:, h * GDN_DK:(h + 1) * GDN_DK]
        kh = act[:, BRANCH_W + h * GDN_DK:BRANCH_W + (h + 1) * GDN_DK]
        q.append(qh * lax.rsqrt(jnp.sum(qh * qh, axis=-1, keepdims=True) + EPS) * (GDN_DK ** -0.5))
        k.append(kh * lax.rsqrt(jnp.sum(kh * kh, axis=-1, keepdims=True) + EPS))
        v.append(act[:, 2 * BRANCH_W + h * GDN_DK:2 * BRANCH_W + (h + 1) * GDN_DK])
        cb.append(g_lb[:, h * LANE:(h + 1) * LANE])
        bb.append(b_lb[:, h * LANE:(h + 1) * LANE])
    dm = [jnp.where(incl, jnp.exp(jnp.where(incl, cb[h] - cb[h].T, 0.0)), 0.0) for h in heads]
    kb = [k[h].astype(BF16) for h in heads]
    kk = [_dot_nt(kb[h], kb[h]) for h in heads]
    qkb = [(_dot_nt(q[h].astype(BF16), kb[h]) * dm[h]).astype(BF16) for h in heads]
    mp = [jnp.where(strict, -(bb[h] * kk[h] * dm[h]), 0.0) for h in heads]
    inv = [eye + mp[h] for h in heads]
    for _ in range(int(np.log2(GDN_CHUNK)) - 1):
        mpb = [mp[h].astype(BF16) for h in heads]
        mp = [_dot(mpb[h], mpb[h]) for h in heads]
        inv = [inv[h] + _dot(inv[h].astype(BF16), mp[h].astype(BF16)) for h in heads]
    eg = [jnp.exp(cb[h]) for h in heads]
    sol = [_dot(inv[h].astype(BF16),
                jnp.concatenate([bb[h] * eg[h] * k[h], bb[h] * v[h]], axis=1).astype(BF16)) for h in heads]
    wk = [sol[h][:, :GDN_DK].astype(BF16) for h in heads]
    u0 = [sol[h][:, GDN_DK:] for h in heads]
    qd = [(q[h] * eg[h]).astype(BF16) for h in heads]
    zero = jnp.zeros((GDN_CHUNK, LANE), F32)
    st = [st_ref[h] for h in heads]
    o_parts = [[] for _ in heads]
    for cc in range(n_sub):
        rows = slice(cc * GDN_CHUNK, (cc + 1) * GDN_CHUNK)
        in_chunk = (r >> 6) == cc
        glast = [cb[h][(cc + 1) * GDN_CHUNK - 1:(cc + 1) * GDN_CHUNK, :] for h in heads]
        stb = [st[h].astype(BF16) for h in heads]
        u = [u0[h][rows] - _dot(wk[h][rows], stb[h]) for h in heads]
        upad = [jnp.concatenate([u[h] if j == cc else zero for j in range(n_sub)], axis=0).astype(BF16)
                for h in heads]
        for h in heads:
            o_parts[h].append(_dot(qd[h][rows], stb[h]) + _dot(qkb[h][rows], upad[h]))
        kd = [jnp.where(in_chunk, k[h] * jnp.exp(jnp.where(in_chunk, glast[h] - cb[h], 0.0)), 0.0) for h in heads]
        st = [jnp.exp(glast[h]) * st[h] + _dot(kd[h].T.astype(BF16), upad[h]) for h in heads]
    for h in heads:
        st_ref[h] = st[h]
    outs = [_rms(jnp.concatenate(o_parts[h], axis=0), ng) for h in heads]
    y = jnp.concatenate(outs, axis=1) * _silu(gate_ref[...].astype(F32))
    o_ref[...] = y.astype(o_ref.dtype)


def _gdn(p3, sm3, conv_w, a_log, dt_bias, norm_g):
    b, l, _ = p3.shape
    btril = np.zeros((TILE, TILE), np.float32)
    for i in range(TILE):
        lo = (i // GDN_CHUNK) * GDN_CHUNK
        btril[i, lo:i + 1] = 1.0
    consts = [conv_w.astype(F32), _pad_lanes(jnp.exp(a_log.astype(F32)), SM_A), _pad_lanes(dt_bias, SM_A),
              norm_g.reshape(1, -1).astype(F32), jnp.asarray(btril, BF16),
              jnp.asarray(_np_lane_expand(SM_A, GDN_HEADS, LANE), BF16),
              jnp.asarray(_np_lane_expand(SM_B, GDN_HEADS, LANE), BF16)]
    return pl.pallas_call(
        _gdn_kernel,
        grid=(b, l // TILE),
        in_specs=[pl.BlockSpec((None, TILE, GDN_QKV), lambda i, j: (i, j, P_OFF["a_qkv"] // GDN_QKV)),
                  pl.BlockSpec((None, TILE, LANE), lambda i, j: (i, j, 0)),
                  pl.BlockSpec((None, TILE, BRANCH_W), lambda i, j: (i, j, P_OFF["a_gate"] // BRANCH_W))]
        + [_full_spec(a.shape) for a in consts],
        out_specs=pl.BlockSpec((None, TILE, BRANCH_W), lambda i, j: (i, j, 0)),
        out_shape=jax.ShapeDtypeStruct((b, l, BRANCH_W), BF16),
        scratch_shapes=[pltpu.VMEM((GDN_HEADS, GDN_DK, GDN_DK), F32), pltpu.VMEM((8, GDN_QKV), F32),
                        pltpu.VMEM((TILE + 8, GDN_QKV), F32)],
        compiler_params=pltpu.CompilerParams(dimension_semantics=("arbitrary", "arbitrary"),
                                             vmem_limit_bytes=VMEM_LIMIT),
        name="gdn",
    )(p3, sm3, p3, *consts)


SG_TILE = 4 * SG_CHUNK


def _sg_kernel(u_ref, v_ref, gate_ref, lng_ref, lnb_ref, w_ref, bs_ref, o_ref):
    r = lax.broadcasted_iota(I32, (SG_CHUNK, SG_CHUNK), 0)
    c = lax.broadcasted_iota(I32, (SG_CHUNK, SG_CHUNK), 1)
    gw = BRANCH_W // SG_GROUPS
    wc = [jnp.where(r >= c, w_ref[g], 0.0).astype(BF16) for g in range(SG_GROUPS)]
    for t0 in range(0, u_ref.shape[0], SG_CHUNK):
        rows = slice(t0, t0 + SG_CHUNK)
        u = _gelu_tanh(u_ref[rows, :].astype(F32))
        v = _gelu_tanh(v_ref[rows, :].astype(F32))
        mu = jnp.mean(v, axis=-1, keepdims=True)
        var = jnp.mean(jnp.square(v - mu), axis=-1, keepdims=True)
        vn = ((v - mu) * lax.rsqrt(var + 1e-5) * lng_ref[...] + lnb_ref[...]).astype(BF16)
        mixed = jnp.concatenate([_dot(wc[g], vn[:, g * gw:(g + 1) * gw]) for g in range(SG_GROUPS)], axis=1)
        o_ref[rows, :] = (u * (mixed + bs_ref[...]) * _silu(gate_ref[rows, :].astype(F32))).astype(o_ref.dtype)


def _sg(p3, ln_g, ln_b, w_s, b_s):
    b, l, _ = p3.shape
    gw = BRANCH_W // SG_GROUPS
    tile = min(SG_TILE, l)
    bs_full = jnp.repeat(jnp.swapaxes(b_s, 0, 1).astype(F32), gw, axis=1)
    consts = [ln_g.reshape(1, -1).astype(F32), ln_b.reshape(1, -1).astype(F32), w_s.astype(F32), bs_full]
    blk = lambda name: pl.BlockSpec((None, tile, BRANCH_W), lambda i, j: (i, j, P_OFF[name] // BRANCH_W))
    return pl.pallas_call(
        _sg_kernel,
        grid=(b, l // tile),
        in_specs=[blk("b_u"), blk("b_v"), blk("b_gate")] + [_full_spec(a.shape) for a in consts],
        out_specs=pl.BlockSpec((None, tile, BRANCH_W), lambda i, j: (i, j, 0)),
        out_shape=jax.ShapeDtypeStruct((b, l, BRANCH_W), BF16),
        compiler_params=pltpu.CompilerParams(dimension_semantics=("arbitrary", "arbitrary"),
                                             vmem_limit_bytes=VMEM_LIMIT),
        name="spatial_gating",
    )(p3, p3, p3, *consts)


_NEG = -1e30
_INT_MIN = -2 ** 31


def _fold8(x, op):
    parts = [x[r:r + 8] for r in range(0, x.shape[0], 8)]
    while len(parts) > 1:
        parts = [op(parts[k], parts[k + 1]) for k in range(0, len(parts) - 1, 2)] + (
            [parts[-1]] if len(parts) % 2 else [])
    return parts[0]


KEY_CHUNK = 2 * TILE


def _bit_planes(u):
    a = [u[8 * g:8 * g + 8] for g in range(32)]
    j, m = 16, 0x0000FFFF
    while j:
        k = 0
        while k < 32:
            t = (a[k] ^ (a[k + j] >> j)) & m
            a[k] = a[k] ^ t
            a[k + j] = a[k + j] ^ (t << j)
            k = (k + j + 1) & ~j
        j >>= 1
        m = m ^ ((m << j) & 0xFFFFFFFF)
    return [a[31 - b] for b in range(32)]


def _dsa_kernel(q_ref, iq_ref, small_ref, gate_ref, kk_ref, vv_ref, ik_ref, qg_ref, kg_ref, sl_ref,
                o_ref, kn_ref, vt_ref, ikb_ref, key_ref, plane_ref, lg_ref, tau_ref, acc_ref, *, top_k):
    i = pl.program_id(1)
    nh = DSA_HEADS
    nc = key_ref.shape[0]
    kc = KEY_CHUNK

    srow = lax.broadcasted_iota(I32, (kc, LANE), 0)
    lane = lax.broadcasted_iota(I32, (kc, LANE), 1)
    qpos = i * TILE + lane
    pos_lanes = (lane == HALF) | (lane == HALF + 1)

    @pl.when(i == 0)
    def _():
        def prep(jc, carry):
            rows = pl.ds(pl.multiple_of(jc * kc, kc), kc)
            kpos = jnp.where(lane == HALF, srow, jc * kc).astype(F32)
            kn = jnp.where(lane < HALF, _rms(kk_ref[rows, :].astype(F32), kg_ref[...]), jnp.where(pos_lanes, kpos, 0.0))
            kn_ref[jc] = kn.astype(BF16)
            v = jnp.where(lane < HALF, vv_ref[rows, :].astype(F32), 1.0)
            vt_ref[jc] = jnp.concatenate([v[r:r + TILE].T for r in range(0, kc, TILE)], axis=1).astype(BF16)
            ikb_ref[jc] = ik_ref[rows, :]
            return carry
        lax.fori_loop(0, nc, prep, 0)

    n_ch = (i + 2) // 2
    n_ch2 = (n_ch + 1) // 2

    @pl.when(n_ch < nc)
    def _():
        key_ref[n_ch] = jnp.full((kc, LANE), _INT_MIN, I32)

    iq = iq_ref[...].astype(F32)
    lane_q = lax.broadcasted_iota(I32, (TILE, LANE), 1)
    lo = lane_q < HALF
    pos_lanes_q = (lane_q == HALF) | (lane_q == HALF + 1)
    iq_st = jnp.concatenate(
        [jnp.where(lo if h % 2 == 0 else ~lo, iq[:, (h // 2) * LANE:(h // 2 + 1) * LANE], 0.0) for h in range(nh)],
        axis=0).astype(BF16)
    small_t = small_ref[...].T * (IDX_HEADS ** -0.5 * IDX_HD ** -0.5)
    iw_row = jnp.concatenate([small_t[SM_IW + h:SM_IW + h + 1, :] for h in range(nh)], axis=1)

    def score_pair(j2, carry):
        jcs = (2 * j2, 2 * j2 + 1)
        raw = [_dot_nt(ikb_ref[jc], iq_st) for jc in jcs]
        for jc, s_raw in zip(jcs, raw):
            s_h = jnp.maximum(s_raw, 0.0) * iw_row
            s = s_h[:, 0:TILE]
            for h in range(1, nh):
                s = s + s_h[:, h * TILE:(h + 1) * TILE]
            bits = pltpu.bitcast(s, I32)
            key = jnp.where(bits < 0, bits ^ 0x7FFFFFFF, bits)
            key = jnp.where(jc * kc + srow <= qpos, key, _INT_MIN)
            key_ref[jc] = key
            planes = _bit_planes(key ^ _INT_MIN)
            for b in range(32):
                plane_ref[jc, b] = planes[b]
        return carry

    lax.fori_loop(0, n_ch2, score_pair, 0)

    def count(pred):
        def body(j2, acc):
            for u in range(2):
                acc = acc + _fold8(jnp.where(pred(key_ref[2 * j2 + u]), 1.0, 0.0), jnp.add)
            return acc
        acc = lax.fori_loop(0, n_ch2, body, jnp.zeros((8, LANE), F32))
        return jnp.sum(acc, axis=0, keepdims=True)

    tau_ref[...] = jnp.full(tau_ref.shape, _INT_MIN + 1, I32)

    @pl.when((i + 1) * TILE > top_k)
    def _():
        def ones_at(live, b):
            acc = jnp.zeros((8, LANE), I32)
            for jc in range(nc):
                acc = acc + lax.population_count(live[jc] & plane_ref[jc, b])
            return jnp.sum(acc.astype(F32), axis=0, keepdims=True)

        live0 = tuple(jnp.where(jc < 2 * n_ch2, jnp.full((8, LANE), -1, I32), 0) for jc in range(nc))

        def bit_body(bi, carry):
            live, cnt, left, tau_u = carry
            b = 31 - bi
            take = cnt >= left
            left = jnp.where(take, left, left - cnt)
            tau_u = jnp.where(take, tau_u | lax.shift_left(jnp.int32(1), b), tau_u)
            b_next = jnp.maximum(b - 1, 0)
            acc = jnp.zeros((8, LANE), I32)
            new_live = []
            for jc in range(nc):
                with_bit = live[jc] & plane_ref[jc, b]
                keep = jnp.where(take, with_bit, live[jc] ^ with_bit)
                new_live.append(keep)
                acc = acc + lax.population_count(keep & plane_ref[jc, b_next])
            return tuple(new_live), jnp.sum(acc.astype(F32), axis=0, keepdims=True), left, tau_u

        init = (live0, ones_at(live0, 31), jnp.full((1, LANE), float(top_k), F32), jnp.zeros((1, LANE), I32))
        tau = lax.fori_loop(0, 32, bit_body, init)[3] ^ _INT_MIN
        tau = jnp.where(qpos[0:1, :] + 1 <= top_k, _INT_MIN + 1, tau)
        tau_ref[...] = jnp.broadcast_to(tau, tau_ref.shape)

    tau = tau_ref[0:1, :]
    need = top_k - count(lambda kx: kx > tau)

    q = q_ref[...].astype(F32)
    q_parts = []
    for mpair in range(nh // 2):
        qp = q[:, mpair * LANE:(mpair + 1) * LANE]
        sq = qp * qp
        s_lo = jnp.sum(jnp.where(lo, sq, 0.0), axis=1, keepdims=True)
        s_hi = jnp.sum(jnp.where(lo, 0.0, sq), axis=1, keepdims=True)
        qn = qp * lax.rsqrt(jnp.where(lo, s_lo, s_hi) * (1.0 / DSA_HD) + EPS) * qg_ref[...] * (DSA_HD ** -0.5)
        for e, qh in enumerate((qn, pltpu.roll(qn, HALF, 1))):
            slope = 2.0 ** (-8.0 * (2 * mpair + e + 1) / nh)
            q_parts.append(jnp.where(lo, qh, jnp.where(pos_lanes_q, slope, 0.0)))
    q_st = jnp.concatenate(q_parts, axis=0).astype(BF16)

    def mask_chunk(jc, carry):
        run, mred = carry
        key = key_ref[jc]
        eq = key == tau
        eqf = jnp.where(eq, 1.0, 0.0)
        prefix = _dot(sl_ref[...], eqf.astype(BF16)) + run
        sel = (key > tau) | (eq & (prefix < need))
        lg = _dot_nt(kn_ref[jc], q_st)
        new = []
        for h in range(nh):
            x = jnp.where(sel, lg[:, h * TILE:(h + 1) * TILE], _NEG)
            lg_ref[jc, :, h * TILE:(h + 1) * TILE] = x
            new.append(jnp.maximum(mred[h], _fold8(x, jnp.maximum)))
        return run + jnp.sum(_fold8(eqf, jnp.add), axis=0, keepdims=True), tuple(new)

    _, mred = lax.fori_loop(0, 2 * n_ch2, mask_chunk,
                            (jnp.zeros((1, LANE), F32), tuple(jnp.full((8, LANE), _NEG, F32) for _ in range(nh))))
    m_row = [jnp.max(mh, axis=0, keepdims=True) for mh in mred]

    acc_ref[...] = jnp.zeros_like(acc_ref)

    def pv_pair(j2, carry):
        pv = None
        for jc in (2 * j2, 2 * j2 + 1):
            ps = [jnp.exp((lg_ref[jc, :, h * TILE:(h + 1) * TILE] - m_row[h]).astype(BF16)) for h in range(nh)]
            term = _dot(vt_ref[jc], jnp.concatenate(ps, axis=1))
            pv = term if pv is None else pv + term
        acc_ref[...] += pv
        return carry

    lax.fori_loop(0, n_ch2, pv_pair, 0)

    heads = []
    for h in range(nh):
        blk = acc_ref[:, h * TILE:(h + 1) * TILE]
        o_t = (blk / blk[HALF:HALF + 1, :]).T
        heads.append(o_t if h % 2 == 0 else pltpu.roll(o_t, HALF, 1))
    pairs = [jnp.where(lo, heads[2 * mp], heads[2 * mp + 1]) for mp in range(nh // 2)]
    o_ref[...] = (jnp.concatenate(pairs, axis=1) * _silu(gate_ref[...].astype(F32))).astype(o_ref.dtype)


def _dsa(p3, sm3, q_norm_g, k_norm_g):
    b, l, _ = p3.shape
    top_k = min(DSA_TOPK_MAX, l // 4)
    nb = l // TILE
    nc = l // KEY_CHUNK
    sl = np.tril(np.ones((KEY_CHUNK, KEY_CHUNK), np.float32), -1)
    dup = lambda g: jnp.concatenate([g, g]).reshape(1, LANE).astype(F32)
    consts = [dup(q_norm_g), dup(k_norm_g), jnp.asarray(sl, BF16)]
    qblk = lambda name: pl.BlockSpec((None, TILE, BRANCH_W), lambda i, j: (i, j, P_OFF[name] // BRANCH_W))
    seq = lambda name: pl.BlockSpec((None, l, LANE), lambda i, j: (i, 0, P_OFF[name] // LANE),
                                    pipeline_mode=pl.Buffered(1))
    chunks = lambda dt: pltpu.VMEM((nc, KEY_CHUNK, LANE), dt)
    return pl.pallas_call(
        functools.partial(_dsa_kernel, top_k=top_k),
        grid=(b, nb),
        in_specs=[qblk("c_q"), qblk("c_iq"),
                  pl.BlockSpec((None, TILE, LANE), lambda i, j: (i, j, 0)),
                  qblk("c_gate"), seq("c_kk"), seq("c_vv"), seq("c_ikik")]
        + [_full_spec(a.shape) for a in consts],
        out_specs=pl.BlockSpec((None, TILE, BRANCH_W), lambda i, j: (i, j, 0)),
        out_shape=jax.ShapeDtypeStruct((b, l, BRANCH_W), BF16),
        scratch_shapes=[chunks(BF16), pltpu.VMEM((nc, LANE, KEY_CHUNK), BF16), chunks(BF16), chunks(I32),
                        pltpu.VMEM((nc, 32, 8, LANE), I32),
                        pltpu.VMEM((nc, KEY_CHUNK, DSA_HEADS * TILE), F32), pltpu.VMEM((8, LANE), I32),
                        pltpu.VMEM((LANE, DSA_HEADS * TILE), F32)],
        compiler_params=pltpu.CompilerParams(dimension_semantics=("arbitrary", "arbitrary"),
                                             vmem_limit_bytes=VMEM_LIMIT),
        name="dsa",
    )(p3, p3, sm3, p3, p3, p3, p3, *consts)


def _ssd_kernel(z_ref, xbc_ref, small_ref, cw_ref, cbias_ref, dtb_ref, apad_ref, dexp_ref, ng_ref, tril_ref,
                e64_ref, e128_ref, o_ref, st_ref, halo_ref, buf_ref):
    @pl.when(pl.program_id(1) == 0)
    def _():
        st_ref[...] = jnp.zeros_like(st_ref)
        halo_ref[...] = jnp.zeros_like(halo_ref)

    nhp = SSD_HEADS * SSD_HD
    gs = SSD_STATE
    act = _silu(_causal_conv(xbc_ref[...].astype(F32), cw_ref, halo_ref, buf_ref) + cbias_ref[...])
    xs = act[:, :nhp]
    bm = act[:, nhp:nhp + SSD_GROUPS * gs]
    cm = act[:, nhp + SSD_GROUPS * gs:]
    dt_full = _softplus(small_ref[...] + dtb_ref[...])
    cs_full = _dot01_l(tril_ref[...], dt_full * apad_ref[...])
    dt_exp = _dot01_r(dt_full, e64_ref[...])
    cs_exp = _dot01_r(cs_full, e64_ref[...])
    cs_lb = _dot01_r(cs_full, e128_ref[...])
    xdt = xs * dt_exp
    cs_last = cs_exp[TILE - 1:TILE, :]
    xds = (xdt * jnp.exp(cs_last - cs_exp)).astype(BF16)
    ecs = jnp.exp(cs_exp)
    xdt_b = xdt.astype(BF16)

    r = lax.broadcasted_iota(I32, (TILE, TILE), 0)
    c = lax.broadcasted_iota(I32, (TILE, TILE), 1)
    incl = r >= c
    lo = c < HALF
    hpg = SSD_HEADS // SSD_GROUPS
    gw = hpg * SSD_HD
    y_parts = []
    for g in range(SSD_GROUPS):
        bmg = bm[:, g * gs:(g + 1) * gs]
        cmg = cm[:, g * gs:(g + 1) * gs].astype(BF16)
        cbg = _dot_nt(cmg, bmg.astype(BF16))
        for mp in range(hpg // 2):
            pair = (g * hpg) // 2 + mp
            xpair = xdt_b[:, pair * LANE:(pair + 1) * LANE]
            ys = []
            for e in range(2):
                h = 2 * pair + e
                cb = cs_lb[:, h * LANE:(h + 1) * LANE]
                lm = jnp.where(incl, jnp.exp(jnp.where(incl, cb - cb.T, 0.0)), 0.0)
                ys.append(_dot((cbg * lm).astype(BF16), xpair))
            y_parts.append(jnp.where(lo, ys[0], ys[1]))
        st = st_ref[g]
        y_off = _dot(cmg, st.astype(BF16)) * ecs[:, g * gw:(g + 1) * gw]
        y_parts[-(hpg // 2):] = [yp + y_off[:, k * LANE:(k + 1) * LANE]
                                 for k, yp in enumerate(y_parts[-(hpg // 2):])]
        new = _dot(bmg.T.astype(BF16), xds[:, g * gw:(g + 1) * gw])
        st_ref[g] = jnp.exp(cs_last[:, g * gw:(g + 1) * gw]) * st + new
    y = jnp.concatenate(y_parts, axis=1) + xs * dexp_ref[...]
    yz = y * _silu(z_ref[...].astype(F32))
    o_ref[...] = _rms(yz, ng_ref[...]).astype(o_ref.dtype)


def _ssd(p3, sm3, conv_w, conv_b, a_log, dt_bias, d_skip, norm_g):
    b, l, _ = p3.shape
    consts = [conv_w.astype(F32), conv_b.reshape(1, -1).astype(F32), _pad_lanes(dt_bias, SM_DT),
              _pad_lanes(-jnp.exp(a_log.astype(F32)), SM_DT),
              jnp.repeat(d_skip.astype(F32), SSD_HD).reshape(1, -1), norm_g.reshape(1, -1).astype(F32),
              jnp.asarray(np.tril(np.ones((TILE, TILE), np.float32)), BF16),
              jnp.asarray(_np_lane_expand(SM_DT, SSD_HEADS, SSD_HD), BF16),
              jnp.asarray(_np_lane_expand(SM_DT, SSD_HEADS, LANE), BF16)]
    return pl.pallas_call(
        _ssd_kernel,
        grid=(b, l // TILE),
        in_specs=[pl.BlockSpec((None, TILE, BRANCH_W), lambda i, j: (i, j, P_OFF["d_z"] // BRANCH_W)),
                  pl.BlockSpec((None, TILE, SSD_XBC), lambda i, j: (i, j, P_OFF["d_xbc"] // SSD_XBC)),
                  pl.BlockSpec((None, TILE, LANE), lambda i, j: (i, j, 0))]
        + [_full_spec(a.shape) for a in consts],
        out_specs=pl.BlockSpec((None, TILE, BRANCH_W), lambda i, j: (i, j, 0)),
        out_shape=jax.ShapeDtypeStruct((b, l, BRANCH_W), BF16),
        scratch_shapes=[pltpu.VMEM((SSD_GROUPS, SSD_STATE, (SSD_HEADS // SSD_GROUPS) * SSD_HD), F32),
                        pltpu.VMEM((8, SSD_XBC), F32), pltpu.VMEM((TILE + 8, SSD_XBC), F32)],
        compiler_params=pltpu.CompilerParams(dimension_semantics=("arbitrary", "arbitrary"),
                                             vmem_limit_bytes=VMEM_LIMIT),
        name="ssd",
    )(p3, p3, sm3, *consts)


def _mem_kv_kernel(mem_ref, g_ref, w_ref, kg_ref, k_ref, v_ref):
    kv = _dot(_rms(mem_ref[...], g_ref[...]).astype(BF16), w_ref[...])
    for h in range(MEM_HEADS):
        hs = slice(h * MEM_HD, (h + 1) * MEM_HD)
        k_ref[:, hs] = _rms(kv[:, hs], kg_ref[...]).astype(k_ref.dtype)
    v_ref[...] = kv[:, BRANCH_W:].astype(v_ref.dtype)


def _mem_kv(mem, mem_norm_g, w_kv_bf16, k_norm_g):
    b, m, _ = mem.shape
    consts = [mem_norm_g.reshape(1, -1).astype(F32), w_kv_bf16, k_norm_g.reshape(1, -1).astype(F32)]
    out = jax.ShapeDtypeStruct((b, m, BRANCH_W), BF16)
    return pl.pallas_call(
        _mem_kv_kernel,
        grid=(b,),
        in_specs=[pl.BlockSpec((None, m, D_MODEL), lambda i: (i, 0, 0))] + [_full_spec(a.shape) for a in consts],
        out_specs=[pl.BlockSpec((None, m, BRANCH_W), lambda i: (i, 0, 0))] * 2,
        out_shape=[out, out],
        compiler_params=pltpu.CompilerParams(dimension_semantics=("arbitrary",), vmem_limit_bytes=VMEM_LIMIT),
        name="mem_kv",
    )(mem, *consts)


def _mem_attn_kernel(q_ref, gate_ref, k_ref, v_ref, qg_ref, o_ref):
    q = q_ref[...].astype(F32)
    outs = []
    for h in range(MEM_HEADS):
        hs = slice(h * MEM_HD, (h + 1) * MEM_HD)
        qn = _rms(q[:, hs], qg_ref[...]).astype(BF16)
        lg = _dot_nt(qn, k_ref[:, hs]) * (MEM_HD ** -0.5)
        p = jnp.exp(lg - jnp.max(lg, axis=1, keepdims=True))
        outs.append(_dot(p.astype(BF16), v_ref[:, hs]) / jnp.sum(p, axis=1, keepdims=True))
    o_ref[...] = (jnp.concatenate(outs, axis=1) * _silu(gate_ref[...].astype(F32))).astype(o_ref.dtype)


def _mem_attn(p3, k, v, q_norm_g):
    b, l, _ = p3.shape
    m = k.shape[1]
    tm = min(512, l)
    qg = q_norm_g.reshape(1, -1).astype(F32)
    blk = lambda name: pl.BlockSpec((None, tm, BRANCH_W), lambda i, j: (i, j, P_OFF[name] // BRANCH_W))
    kv = pl.BlockSpec((None, m, BRANCH_W), lambda i, j: (i, 0, 0))
    return pl.pallas_call(
        _mem_attn_kernel,
        grid=(b, l // tm),
        in_specs=[blk("m_q"), blk("m_gate"), kv, kv, _full_spec(qg.shape)],
        out_specs=pl.BlockSpec((None, tm, BRANCH_W), lambda i, j: (i, j, 0)),
        out_shape=jax.ShapeDtypeStruct((b, l, BRANCH_W), BF16),
        compiler_params=pltpu.CompilerParams(dimension_semantics=("arbitrary", "arbitrary"),
                                             vmem_limit_bytes=VMEM_LIMIT),
        name="mem_attn",
    )(p3, p3, k, v, qg)


def _reordered_w_in(w):
    cols = jnp.asarray(np.maximum(_P_COLUMNS, 0))
    keep = jnp.asarray((_P_COLUMNS >= 0).astype(np.float32))
    return (jnp.take(w, cols, axis=1) * keep).astype(BF16)


def kernel(x, mem, norm_g, w_in, gdn_conv_w, gdn_a_log, gdn_dt_bias, gdn_norm_g, sg_ln_g, sg_ln_b, sg_w, sg_b, dsa_q_norm_g, dsa_k_norm_g, ssd_conv_w, ssd_conv_b, ssd_a_log, ssd_dt_bias, ssd_d, ssd_norm_g, mem_norm_g, w_mem_kv, mem_q_norm_g, mem_k_norm_g, w_gate, w_branch, w_out):
    b, l, d = x.shape
    depth = norm_g.shape[0]
    x2d = x.reshape(b * l, d)
    for i in range(depth):
        g = norm_g[i].reshape(1, d).astype(F32)
        p2d, sm2d = _in_proj(x2d, g, _reordered_w_in(w_in[i]))
        p3 = p2d.reshape(b, l, P_MAIN)
        sm3 = sm2d.reshape(b, l, LANE)
        mk, mv = _mem_kv(mem, mem_norm_g[i], w_mem_kv[i].astype(BF16), mem_k_norm_g[i])
        ys = (
            _gdn(p3, sm3, gdn_conv_w[i], gdn_a_log[i], gdn_dt_bias[i], gdn_norm_g[i]),
            _sg(p3, sg_ln_g[i], sg_ln_b[i], sg_w[i], sg_b[i]),
            _dsa(p3, sm3, dsa_q_norm_g[i], dsa_k_norm_g[i]),
            _ssd(p3, sm3, ssd_conv_w[i], ssd_conv_b[i], ssd_a_log[i], ssd_dt_bias[i], ssd_d[i], ssd_norm_g[i]),
            _mem_attn(p3, mk, mv, mem_q_norm_g[i]),
        )
        ys = [y.reshape(b * l, BRANCH_W) for y in ys]
        x2d = _merge(x2d, g, ys, w_gate[i].astype(BF16), w_branch[i].astype(BF16), w_out[i].astype(BF16))
    return x2d.reshape(b, l, d)
```

```python
import functools

import numpy as np
import jax
import jax.numpy as jnp
from jax import lax
from jax.experimental import pallas as pl
from jax.experimental.pallas import tpu as pltpu

F32 = jnp.float32
BF16 = jnp.bfloat16
I32 = jnp.int32

D_MODEL = 1024
N_BRANCH = 5
BRANCH_W = D_MODEL // 2
CONV_K = 4
EPS = 1e-6
GDN_HEADS = 4
GDN_DK = BRANCH_W // GDN_HEADS
GDN_CHUNK = 64
GDN_QKV = 3 * BRANCH_W
SG_CHUNK = 128
SG_GROUPS = 4
DSA_HEADS = 8
DSA_HD = BRANCH_W // DSA_HEADS
IDX_HEADS = 8
IDX_HD = 64
DSA_TOPK_MAX = 256
Q_BLOCK = 128
SSD_HEADS = 8
SSD_HD = BRANCH_W // SSD_HEADS
SSD_GROUPS = 2
SSD_STATE = 128
SSD_CHUNK = 128
SSD_XBC = SSD_HEADS * SSD_HD + 2 * SSD_GROUPS * SSD_STATE
MEM_HEADS = 4
MEM_HD = BRANCH_W // MEM_HEADS

IN_SPLITS = (
    GDN_QKV, GDN_HEADS, GDN_HEADS, BRANCH_W,
    BRANCH_W, BRANCH_W, BRANCH_W,
    DSA_HEADS * DSA_HD, DSA_HD, DSA_HD, IDX_HEADS * IDX_HD, IDX_HD, IDX_HEADS, BRANCH_W,
    BRANCH_W, SSD_XBC, SSD_HEADS,
    MEM_HEADS * MEM_HD, BRANCH_W,
)
_SEG_NAMES = ("a_qkv", "a_a", "a_b", "a_gate", "b_u", "b_v", "b_gate", "c_q", "c_k", "c_v", "c_iq", "c_ik",
              "c_iw", "c_gate", "d_z", "d_xbc", "d_dt", "m_q", "m_gate")
_SEG_START = dict(zip(_SEG_NAMES, np.concatenate([[0], np.cumsum(IN_SPLITS)[:-1]]).tolist()))
_SEG_WIDTH = dict(zip(_SEG_NAMES, IN_SPLITS))

LANE = 128
HALF = LANE // 2
TILE = 128
SEQ_STEP = 4 * TILE
VMEM_LIMIT = 56 * 1024 * 1024

_P_LAYOUT = (("a_qkv", 1536), ("b_u", 512), ("d_xbc", 1024), ("a_gate", 512), ("b_v", 512), ("b_gate", 512),
             ("c_q", 512), ("c_iq", 512), ("c_gate", 512), ("d_z", 512), ("m_q", 512), ("m_gate", 512),
             ("c_kk", 128), ("c_vv", 128), ("c_ikik", 128), ("small", 128))
P_OFF = {}
_o = 0
for _n, _w in _P_LAYOUT:
    P_OFF[_n] = _o
    _o += _w
P_COLS = _o
P_MAIN = P_OFF["small"]
SM_A, SM_B, SM_DT, SM_IW = 0, 4, 8, 16


def _p_columns():
    cols = []
    for name, width in _P_LAYOUT:
        if name in _SEG_START:
            cols += list(range(_SEG_START[name], _SEG_START[name] + width))
        elif name in ("c_kk", "c_vv", "c_ikik"):
            src = {"c_kk": "c_k", "c_vv": "c_v", "c_ikik": "c_ik"}[name]
            one = list(range(_SEG_START[src], _SEG_START[src] + _SEG_WIDTH[src]))
            cols += one + one
        else:
            sm = [-1] * LANE
            for seg, at in (("a_a", SM_A), ("a_b", SM_B), ("d_dt", SM_DT), ("c_iw", SM_IW)):
                for j in range(_SEG_WIDTH[seg]):
                    sm[at + j] = _SEG_START[seg] + j
            cols += sm
    return np.asarray(cols, np.int32)


_P_COLUMNS = _p_columns()


def _dot(a, b, precision=None):
    return jnp.dot(a, b, preferred_element_type=F32, precision=precision)


def _dot_nt(a, b):
    return lax.dot_general(a, b, (((1,), (1,)), ((), ())), preferred_element_type=F32)


def _split3(x):
    hi = x.astype(BF16)
    r1 = x - hi.astype(F32)
    mid = r1.astype(BF16)
    lo = (r1 - mid.astype(F32)).astype(BF16)
    return hi, mid, lo


def _dot01_r(x, e):
    hi, mid, lo = _split3(x)
    return _dot(hi, e) + _dot(mid, e) + _dot(lo, e)


def _dot01_l(e, x):
    hi, mid, lo = _split3(x)
    return _dot(e, hi) + _dot(e, mid) + _dot(e, lo)


def _sigmoid(x):
    return 1.0 / (1.0 + jnp.exp(-x))


def _silu(x):
    return x * _sigmoid(x)


def _softplus(x):
    return jnp.maximum(x, 0.0) + jnp.log1p(jnp.exp(-jnp.abs(x)))


def _gelu_tanh(x):
    return 0.5 * x * (1.0 + jnp.tanh(np.sqrt(2.0 / np.pi).astype(np.float32) * (x + 0.044715 * (x * x * x))))


def _rms(x, g, eps=EPS):
    return x * lax.rsqrt(jnp.mean(x * x, axis=-1, keepdims=True) + eps) * g


def _full_spec(shape):
    nd = len(shape)
    return pl.BlockSpec(shape, lambda *_: (0,) * nd)


def _in_proj_kernel(x_ref, g_ref, w_ref, o_ref, s_ref):
    h = _rms(x_ref[...], g_ref[...]).astype(BF16)
    step = 1024
    for c0 in range(0, P_MAIN, step):
        c1 = min(c0 + step, P_MAIN)
        o_ref[:, c0:c1] = _dot(h, w_ref[:, c0:c1]).astype(o_ref.dtype)
    s_ref[...] = _dot(h, w_ref[:, P_MAIN:])


def _in_proj(x2d, g, w_bf16):
    m = x2d.shape[0]
    tm = min(512, m)
    return pl.pallas_call(
        _in_proj_kernel,
        grid=(m // tm,),
        in_specs=[pl.BlockSpec((tm, D_MODEL), lambda i: (i, 0)),
                  pl.BlockSpec((1, D_MODEL), lambda i: (0, 0)),
                  pl.BlockSpec((D_MODEL, P_COLS), lambda i: (0, 0), pipeline_mode=pl.Buffered(1))],
        out_specs=[pl.BlockSpec((tm, P_MAIN), lambda i: (i, 0)), pl.BlockSpec((tm, LANE), lambda i: (i, 0))],
        out_shape=[jax.ShapeDtypeStruct((m, P_MAIN), BF16), jax.ShapeDtypeStruct((m, LANE), F32)],
        compiler_params=pltpu.CompilerParams(dimension_semantics=("arbitrary",), vmem_limit_bytes=VMEM_LIMIT),
        name="in_proj",
    )(x2d, g, w_bf16)


def _merge_kernel(x_ref, g_ref, y0, y1, y2, y3, y4, wg_ref, wb_ref, wo_ref, o_ref):
    x = x_ref[...]
    h = _rms(x, g_ref[...]).astype(BF16)
    acc = None
    for p, y in enumerate((y0, y1, y2, y3, y4)):
        term = _sigmoid(_dot(h, wg_ref[p])) * _dot(y[...], wb_ref[p])
        acc = term if acc is None else acc + term
    o_ref[...] = x + _dot(acc.astype(BF16), wo_ref[...])


def _merge(x2d, g, ys, wg, wb, wo):
    m = x2d.shape[0]
    tm = min(512, m)
    row = lambda i: (i, 0)
    const = pl.Buffered(1)
    return pl.pallas_call(
        _merge_kernel,
        grid=(m // tm,),
        in_specs=[pl.BlockSpec((tm, D_MODEL), row), pl.BlockSpec((1, D_MODEL), lambda i: (0, 0))]
        + [pl.BlockSpec((tm, BRANCH_W), row)] * N_BRANCH
        + [pl.BlockSpec((N_BRANCH, D_MODEL, D_MODEL), lambda i: (0, 0, 0), pipeline_mode=const),
           pl.BlockSpec((N_BRANCH, BRANCH_W, D_MODEL), lambda i: (0, 0, 0), pipeline_mode=const),
           pl.BlockSpec((D_MODEL, D_MODEL), lambda i: (0, 0), pipeline_mode=const)],
        out_specs=pl.BlockSpec((tm, D_MODEL), row),
        out_shape=jax.ShapeDtypeStruct((m, D_MODEL), F32),
        compiler_params=pltpu.CompilerParams(dimension_semantics=("arbitrary",), vmem_limit_bytes=VMEM_LIMIT),
        name="merge_out",
    )(x2d, g, *ys, wg, wb, wo)


def _causal_conv(x, w_ref, halo_ref, buf_ref):
    buf_ref[0:8, :] = halo_ref[...]
    buf_ref[8:8 + TILE, :] = x
    y = w_ref[CONV_K - 1:CONV_K, :] * x
    for j in range(CONV_K - 1):
        y = y + w_ref[j:j + 1, :] * buf_ref[8 - (CONV_K - 1) + j:8 - (CONV_K - 1) + j + TILE, :]
    halo_ref[...] = x[TILE - 8:TILE, :]
    return y


def _np_lane_expand(first_row, n_heads, width):
    e = np.zeros((LANE, n_heads * width), np.float32)
    for h in range(n_heads):
        e[first_row + h, h * width:(h + 1) * width] = 1.0
    return e


def _pad_lanes(vec, at):
    out = jnp.zeros((1, LANE), F32)
    return lax.dynamic_update_slice(out, vec.reshape(1, -1).astype(F32), (0, at))


def _tiles_in_order(tile_fn, seq_refs, const_refs, o_ref, scratch_refs):
    @pl.when(pl.program_id(1) == 0)
    def _():
        for ref in scratch_refs[:2]:
            ref[...] = jnp.zeros_like(ref)

    def body(t, carry):
        rows = pl.ds(pl.multiple_of(t * TILE, TILE), TILE)
        tile_fn(*[r.at[rows] for r in seq_refs], *const_refs, o_ref.at[rows], *scratch_refs)
        return carry

    lax.fori_loop(0, o_ref.shape[0] // TILE, body, 0)


def _gdn_kernel(qkv_ref, small_ref, gate_ref, cw_ref, apad_ref, dtb_ref, ng_ref, btril_ref, eg_ref, eb_ref,
                o_ref, st_ref, halo_ref, buf_ref):
    _tiles_in_order(_gdn_tile, (qkv_ref, small_ref, gate_ref),
                    (cw_ref, apad_ref, dtb_ref, ng_ref, btril_ref, eg_ref, eb_ref), o_ref, (st_ref, halo_ref, buf_ref))


def _gdn_tile(qkv_ref, small_ref, gate_ref, cw_ref, apad_ref, dtb_ref, ng_ref, btril_ref, eg_ref, eb_ref,
              o_ref, st_ref, halo_ref, buf_ref):
    act = _silu(_causal_conv(qkv_ref[...].astype(F32), cw_ref, halo_ref, buf_ref))
    small = small_ref[...]
    g_full = -apad_ref[...] * _softplus(small + dtb_ref[...])
    beta_full = _sigmoid(small)
    gc_full = _dot01_l(btril_ref[...], g_full)
    g_lb = _dot01_r(gc_full, eg_ref[...])
    b_lb = _dot01_r(beta_full, eb_ref[...])

    r = lax.broadcasted_iota(I32, (TILE, TILE), 0)
    c = lax.broadcasted_iota(I32, (TILE, TILE), 1)
    same = (r >> 6) == (c >> 6)
    incl = same & (r >= c)
    strict = same & (r > c)
    eye = (r == c).astype(F32)
    n_sub = TILE // GDN_CHUNK
    ng = ng_ref[...]
    heads = range(GDN_HEADS)
    q, k, v, cb, bb = [], [], [], [], []
    for h in heads:
        qh = act[:, h * GDN_DK:(h + 1) * GDN_DK]
        kh = act[:, BRANCH_W + h * GDN_DK:BRANCH_W + (h + 1) * GDN_DK]
        q.append(qh * lax.rsqrt(jnp.sum(qh * qh, axis=-1, keepdims=True) + EPS) * (GDN_DK ** -0.5))
        k.append(kh * lax.rsqrt(jnp.sum(kh * kh, axis=-1, keepdims=True) + EPS))
        v.append(act[:, 2 * BRANCH_W + h * GDN_DK:2 * BRANCH_W + (h + 1) * GDN_DK])
        cb.append(g_lb[:, h * LANE:(h + 1) * LANE])
        bb.append(b_lb[:, h * LANE:(h + 1) * LANE])
    dm = [jnp.where(incl, jnp.exp(jnp.where(incl, cb[h] - cb[h].T, 0.0)), 0.0) for h in heads]
    kb = [k[h].astype(BF16) for h in heads]
    kk = [_dot_nt(kb[h], kb[h]) for h in heads]
    qkb = [(_dot_nt(q[h].astype(BF16), kb[h]) * dm[h]).astype(BF16) for h in heads]
    mp = [jnp.where(strict, -(bb[h] * kk[h] * dm[h]), 0.0) for h in heads]
    inv = [eye + mp[h] for h in heads]
    for _ in range(int(np.log2(GDN_CHUNK)) - 1):
        mpb = [mp[h].astype(BF16) for h in heads]
        mp = [_dot(mpb[h], mpb[h]) for h in heads]
        inv = [inv[h] + _dot(inv[h].astype(BF16), mp[h].astype(BF16)) for h in heads]
    eg = [jnp.exp(cb[h]) for h in heads]
    sol = [_dot(inv[h].astype(BF16),
                jnp.concatenate([bb[h] * eg[h] * k[h], bb[h] * v[h]], axis=1).astype(BF16)) for h in heads]
    wk = [sol[h][:, :GDN_DK].astype(BF16) for h in heads]
    u0 = [sol[h][:, GDN_DK:] for h in heads]
    qd = [(q[h] * eg[h]).astype(BF16) for h in heads]
    zero = jnp.zeros((GDN_CHUNK, LANE), F32)
    st = [st_ref[h] for h in heads]
    o_parts = [[] for _ in heads]
    for cc in range(n_sub):
        rows = slice(cc * GDN_CHUNK, (cc + 1) * GDN_CHUNK)
        in_chunk = (r >> 6) == cc
        glast = [cb[h][(cc + 1) * GDN_CHUNK - 1:(cc + 1) * GDN_CHUNK, :] for h in heads]
        stb = [st[h].astype(BF16) for h in heads]
        u = [u0[h][rows] - _dot(wk[h][rows], stb[h]) for h in heads]
        upad = [jnp.concatenate([u[h] if j == cc else zero for j in range(n_sub)], axis=0).astype(BF16)
                for h in heads]
        for h in heads:
            o_parts[h].append(_dot(qd[h][rows], stb[h]) + _dot(qkb[h][rows], upad[h]))
        kd = [jnp.where(in_chunk, k[h] * jnp.exp(jnp.where(in_chunk, glast[h] - cb[h], 0.0)), 0.0) for h in heads]
        st = [jnp.exp(glast[h]) * st[h] + _dot(kd[h].T.astype(BF16), upad[h]) for h in heads]
    for h in heads:
        st_ref[h] = st[h]
    outs = [_rms(jnp.concatenate(o_parts[h], axis=0), ng) for h in heads]
    y = jnp.concatenate(outs, axis=1) * _silu(gate_ref[...].astype(F32))
    o_ref[...] = y.astype(o_ref.dtype)


def _gdn(p3, sm3, conv_w, a_log, dt_bias, norm_g):
    b, l, _ = p3.shape
    step = min(SEQ_STEP, l)
    btril = np.zeros((TILE, TILE), np.float32)
    for i in range(TILE):
        lo = (i // GDN_CHUNK) * GDN_CHUNK
        btril[i, lo:i + 1] = 1.0
    consts = [conv_w.astype(F32), _pad_lanes(jnp.exp(a_log.astype(F32)), SM_A), _pad_lanes(dt_bias, SM_A),
              norm_g.reshape(1, -1).astype(F32), jnp.asarray(btril, BF16),
              jnp.asarray(_np_lane_expand(SM_A, GDN_HEADS, LANE), BF16),
              jnp.asarray(_np_lane_expand(SM_B, GDN_HEADS, LANE), BF16)]
    return pl.pallas_call(
        _gdn_kernel,
        grid=(b, l // step),
        in_specs=[pl.BlockSpec((None, step, GDN_QKV), lambda i, j: (i, j, P_OFF["a_qkv"] // GDN_QKV)),
                  pl.BlockSpec((None, step, LANE), lambda i, j: (i, j, 0)),
                  pl.BlockSpec((None, step, BRANCH_W), lambda i, j: (i, j, P_OFF["a_gate"] // BRANCH_W))]
        + [_full_spec(a.shape) for a in consts],
        out_specs=pl.BlockSpec((None, step, BRANCH_W), lambda i, j: (i, j, 0)),
        out_shape=jax.ShapeDtypeStruct((b, l, BRANCH_W), BF16),
        scratch_shapes=[pltpu.VMEM((GDN_HEADS, GDN_DK, GDN_DK), F32), pltpu.VMEM((8, GDN_QKV), F32),
                        pltpu.VMEM((TILE + 8, GDN_QKV), F32)],
        compiler_params=pltpu.CompilerParams(dimension_semantics=("arbitrary", "arbitrary"),
                                             vmem_limit_bytes=VMEM_LIMIT),
        name="gdn",
    )(p3, sm3, p3, *consts)


SG_TILE = 4 * SG_CHUNK


def _sg_kernel(u_ref, v_ref, gate_ref, lng_ref, lnb_ref, w_ref, bs_ref, o_ref):
    r = lax.broadcasted_iota(I32, (SG_CHUNK, SG_CHUNK), 0)
    c = lax.broadcasted_iota(I32, (SG_CHUNK, SG_CHUNK), 1)
    gw = BRANCH_W // SG_GROUPS
    wc = [jnp.where(r >= c, w_ref[g], 0.0).astype(BF16) for g in range(SG_GROUPS)]
    for t0 in range(0, u_ref.shape[0], SG_CHUNK):
        rows = slice(t0, t0 + SG_CHUNK)
        u = _gelu_tanh(u_ref[rows, :].astype(F32))
        v = _gelu_tanh(v_ref[rows, :].astype(F32))
        mu = jnp.mean(v, axis=-1, keepdims=True)
        var = jnp.mean(jnp.square(v - mu), axis=-1, keepdims=True)
        vn = ((v - mu) * lax.rsqrt(var + 1e-5) * lng_ref[...] + lnb_ref[...]).astype(BF16)
        mixed = jnp.concatenate([_dot(wc[g], vn[:, g * gw:(g + 1) * gw]) for g in range(SG_GROUPS)], axis=1)
        o_ref[rows, :] = (u * (mixed + bs_ref[...]) * _silu(gate_ref[rows, :].astype(F32))).astype(o_ref.dtype)


def _sg(p3, ln_g, ln_b, w_s, b_s):
    b, l, _ = p3.shape
    gw = BRANCH_W // SG_GROUPS
    tile = min(SG_TILE, l)
    bs_full = jnp.repeat(jnp.swapaxes(b_s, 0, 1).astype(F32), gw, axis=1)
    consts = [ln_g.reshape(1, -1).astype(F32), ln_b.reshape(1, -1).astype(F32), w_s.astype(F32), bs_full]
    blk = lambda name: pl.BlockSpec((None, tile, BRANCH_W), lambda i, j: (i, j, P_OFF[name] // BRANCH_W))
    return pl.pallas_call(
        _sg_kernel,
        grid=(b, l // tile),
        in_specs=[blk("b_u"), blk("b_v"), blk("b_gate")] + [_full_spec(a.shape) for a in consts],
        out_specs=pl.BlockSpec((None, tile, BRANCH_W), lambda i, j: (i, j, 0)),
        out_shape=jax.ShapeDtypeStruct((b, l, BRANCH_W), BF16),
        compiler_params=pltpu.CompilerParams(dimension_semantics=("arbitrary", "arbitrary"),
                                             vmem_limit_bytes=VMEM_LIMIT),
        name="spatial_gating",
    )(p3, p3, p3, *consts)


_NEG = -1e30
_INT_MIN = -2 ** 31


def _fold8(x, op):
    parts = [x[r:r + 8] for r in range(0, x.shape[0], 8)]
    while len(parts) > 1:
        parts = [op(parts[k], parts[k + 1]) for k in range(0, len(parts) - 1, 2)] + (
            [parts[-1]] if len(parts) % 2 else [])
    return parts[0]


KEY_CHUNK = 2 * TILE


def _bit_planes(u):
    a = [u[8 * g:8 * g + 8] for g in range(32)]
    j, m = 16, 0x0000FFFF
    while j:
        k = 0
        while k < 32:
            t = (a[k] ^ (a[k + j] >> j)) & m
            a[k] = a[k] ^ t
            a[k + j] = a[k + j] ^ (t << j)
            k = (k + j + 1) & ~j
        j >>= 1
        m = m ^ ((m << j) & 0xFFFFFFFF)
    return [a[31 - b] for b in range(32)]


def _dsa_kernel(q_ref, iq_ref, small_ref, gate_ref, kk_ref, vv_ref, ik_ref, qg_ref, kg_ref, sl_ref,
                o_ref, kn_ref, vt_ref, ikb_ref, key_ref, plane_ref, lg_ref, tau_ref, acc_ref, *, top_k):
    i = pl.program_id(1)
    nh = DSA_HEADS
    nc = key_ref.shape[0]
    kc = KEY_CHUNK

    srow = lax.broadcasted_iota(I32, (kc, LANE), 0)
    lane = lax.broadcasted_iota(I32, (kc, LANE), 1)
    qpos = i * TILE + lane
    pos_lanes = (lane == HALF) | (lane == HALF + 1)

    @pl.when(i == 0)
    def _():
        def prep(jc, carry):
            rows = pl.ds(pl.multiple_of(jc * kc, kc), kc)
            kpos = jnp.where(lane == HALF, srow, jc * kc).astype(F32)
            kn = jnp.where(lane < HALF, _rms(kk_ref[rows, :].astype(F32), kg_ref[...]), jnp.where(pos_lanes, kpos, 0.0))
            kn_ref[jc] = kn.astype(BF16)
            v = jnp.where(lane < HALF, vv_ref[rows, :].astype(F32), 1.0)
            vt_ref[jc] = jnp.concatenate([v[r:r + TILE].T for r in range(0, kc, TILE)], axis=1).astype(BF16)
            ikb_ref[jc] = ik_ref[rows, :]
            return carry
        lax.fori_loop(0, nc, prep, 0)

    n_ch = (i + 2) // 2
    n_ch2 = (n_ch + 1) // 2

    @pl.when(n_ch < nc)
    def _():
        key_ref[n_ch] = jnp.full((kc, LANE), _INT_MIN, I32)

    iq = iq_ref[...].astype(F32)
    lane_q = lax.broadcasted_iota(I32, (TILE, LANE), 1)
    lo = lane_q < HALF
    pos_lanes_q = (lane_q == HALF) | (lane_q == HALF + 1)
    iq_st = jnp.concatenate(
        [jnp.where(lo if h % 2 == 0 else ~lo, iq[:, (h // 2) * LANE:(h // 2 + 1) * LANE], 0.0) for h in range(nh)],
        axis=0).astype(BF16)
    small_t = small_ref[...].T * (IDX_HEADS ** -0.5 * IDX_HD ** -0.5)
    iw_row = jnp.concatenate([small_t[SM_IW + h:SM_IW + h + 1, :] for h in range(nh)], axis=1)

    def score_pair(j2, carry):
        jcs = (2 * j2, 2 * j2 + 1)
        raw = [_dot_nt(ikb_ref[jc], iq_st) for jc in jcs]
        for jc, s_raw in zip(jcs, raw):
            s_h = jnp.maximum(s_raw, 0.0) * iw_row
            s = s_h[:, 0:TILE]
            for h in range(1, nh):
                s = s + s_h[:, h * TILE:(h + 1) * TILE]
            bits = pltpu.bitcast(s, I32)
            key = jnp.where(bits < 0, bits ^ 0x7FFFFFFF, bits)
            key = jnp.where(jc * kc + srow <= qpos, key, _INT_MIN)
            key_ref[jc] = key
            planes = _bit_planes(key ^ _INT_MIN)
            for b in range(32):
                plane_ref[jc, b] = planes[b]
        return carry

    lax.fori_loop(0, n_ch2, score_pair, 0)

    def count(pred):
        def body(j2, acc):
            for u in range(2):
                acc = acc + _fold8(jnp.where(pred(key_ref[2 * j2 + u]), 1.0, 0.0), jnp.add)
            return acc
        acc = lax.fori_loop(0, n_ch2, body, jnp.zeros((8, LANE), F32))
        return jnp.sum(acc, axis=0, keepdims=True)

    tau_ref[...] = jnp.full(tau_ref.shape, _INT_MIN + 1, I32)

    @pl.when((i + 1) * TILE > top_k)
    def _():
        def ones_at(live, b):
            acc = jnp.zeros((8, LANE), I32)
            for jc in range(nc):
                acc = acc + lax.population_count(live[jc] & plane_ref[jc, b])
            return jnp.sum(acc.astype(F32), axis=0, keepdims=True)

        live0 = tuple(jnp.where(jc < 2 * n_ch2, jnp.full((8, LANE), -1, I32), 0) for jc in range(nc))

        def bit_body(bi, carry):
            live, cnt, left, tau_u = carry
            b = 31 - bi
            take = cnt >= left
            left = jnp.where(take, left, left - cnt)
            tau_u = jnp.where(take, tau_u | lax.shift_left(jnp.int32(1), b), tau_u)
            b_next = jnp.maximum(b - 1, 0)
            acc = jnp.zeros((8, LANE), I32)
            new_live = []
            for jc in range(nc):
                with_bit = live[jc] & plane_ref[jc, b]
                keep = jnp.where(take, with_bit, live[jc] ^ with_bit)
                new_live.append(keep)
                acc = acc + lax.population_count(keep & plane_ref[jc, b_next])
            return tuple(new_live), jnp.sum(acc.astype(F32), axis=0, keepdims=True), left, tau_u

        init = (live0, ones_at(live0, 31), jnp.full((1, LANE), float(top_k), F32), jnp.zeros((1, LANE), I32))
        tau = lax.fori_loop(0, 32, bit_body, init)[3] ^ _INT_MIN
        tau = jnp.where(qpos[0:1, :] + 1 <= top_k, _INT_MIN + 1, tau)
        tau_ref[...] = jnp.broadcast_to(tau, tau_ref.shape)

    tau = tau_ref[0:1, :]
    need = top_k - count(lambda kx: kx > tau)

    q = q_ref[...].astype(F32)
    q_parts = []
    for mpair in range(nh // 2):
        qp = q[:, mpair * LANE:(mpair + 1) * LANE]
        sq = qp * qp
        s_lo = jnp.sum(jnp.where(lo, sq, 0.0), axis=1, keepdims=True)
        s_hi = jnp.sum(jnp.where(lo, 0.0, sq), axis=1, keepdims=True)
        qn = qp * lax.rsqrt(jnp.where(lo, s_lo, s_hi) * (1.0 / DSA_HD) + EPS) * qg_ref[...] * (DSA_HD ** -0.5)
        for e, qh in enumerate((qn, pltpu.roll(qn, HALF, 1))):
            slope = 2.0 ** (-8.0 * (2 * mpair + e + 1) / nh)
            q_parts.append(jnp.where(lo, qh, jnp.where(pos_lanes_q, slope, 0.0)))
    q_st = jnp.concatenate(q_parts, axis=0).astype(BF16)

    def mask_chunk(jc, carry):
        run, mred = carry
        key = key_ref[jc]
        eq = key == tau
        eqf = jnp.where(eq, 1.0, 0.0)
        prefix = _dot(sl_ref[...], eqf.astype(BF16)) + run
        sel = (key > tau) | (eq & (prefix < need))
        lg = _dot_nt(kn_ref[jc], q_st)
        new = []
        for h in range(nh):
            x = jnp.where(sel, lg[:, h * TILE:(h + 1) * TILE], _NEG)
            lg_ref[jc, :, h * TILE:(h + 1) * TILE] = x
            new.append(jnp.maximum(mred[h], _fold8(x, jnp.maximum)))
        return run + jnp.sum(_fold8(eqf, jnp.add), axis=0, keepdims=True), tuple(new)

    _, mred = lax.fori_loop(0, 2 * n_ch2, mask_chunk,
                            (jnp.zeros((1, LANE), F32), tuple(jnp.full((8, LANE), _NEG, F32) for _ in range(nh))))
    m_row = [jnp.max(mh, axis=0, keepdims=True) for mh in mred]

    acc_ref[...] = jnp.zeros_like(acc_ref)

    def pv_pair(j2, carry):
        pv = None
        for jc in (2 * j2, 2 * j2 + 1):
            ps = [jnp.exp((lg_ref[jc, :, h * TILE:(h + 1) * TILE] - m_row[h]).astype(BF16)) for h in range(nh)]
            term = _dot(vt_ref[jc], jnp.concatenate(ps, axis=1))
            pv = term if pv is None else pv + term
        acc_ref[...] += pv
        return carry

    lax.fori_loop(0, n_ch2, pv_pair, 0)

    heads = []
    for h in range(nh):
        blk = acc_ref[:, h * TILE:(h + 1) * TILE]
        o_t = (blk / blk[HALF:HALF + 1, :]).T
        heads.append(o_t if h % 2 == 0 else pltpu.roll(o_t, HALF, 1))
    pairs = [jnp.where(lo, heads[2 * mp], heads[2 * mp + 1]) for mp in range(nh // 2)]
    o_ref[...] = (jnp.concatenate(pairs, axis=1) * _silu(gate_ref[...].astype(F32))).astype(o_ref.dtype)


def _dsa(p3, sm3, q_norm_g, k_norm_g):
    b, l, _ = p3.shape
    top_k = min(DSA_TOPK_MAX, l // 4)
    nb = l // TILE
    nc = l // KEY_CHUNK
    sl = np.tril(np.ones((KEY_CHUNK, KEY_CHUNK), np.float32), -1)
    dup = lambda g: jnp.concatenate([g, g]).reshape(1, LANE).astype(F32)
    consts = [dup(q_norm_g), dup(k_norm_g), jnp.asarray(sl, BF16)]
    qblk = lambda name: pl.BlockSpec((None, TILE, BRANCH_W), lambda i, j: (i, j, P_OFF[name] // BRANCH_W))
    seq = lambda name: pl.BlockSpec((None, l, LANE), lambda i, j: (i, 0, P_OFF[name] // LANE),
                                    pipeline_mode=pl.Buffered(1))
    chunks = lambda dt: pltpu.VMEM((nc, KEY_CHUNK, LANE), dt)
    return pl.pallas_call(
        functools.partial(_dsa_kernel, top_k=top_k),
        grid=(b, nb),
        in_specs=[qblk("c_q"), qblk("c_iq"),
                  pl.BlockSpec((None, TILE, LANE), lambda i, j: (i, j, 0)),
                  qblk("c_gate"), seq("c_kk"), seq("c_vv"), seq("c_ikik")]
        + [_full_spec(a.shape) for a in consts],
        out_specs=pl.BlockSpec((None, TILE, BRANCH_W), lambda i, j: (i, j, 0)),
        out_shape=jax.ShapeDtypeStruct((b, l, BRANCH_W), BF16),
        scratch_shapes=[chunks(BF16), pltpu.VMEM((nc, LANE, KEY_CHUNK), BF16), chunks(BF16), chunks(I32),
                        pltpu.VMEM((nc, 32, 8, LANE), I32),
                        pltpu.VMEM((nc, KEY_CHUNK, DSA_HEADS * TILE), F32), pltpu.VMEM((8, LANE), I32),
                        pltpu.VMEM((LANE, DSA_HEADS * TILE), F32)],
        compiler_params=pltpu.CompilerParams(dimension_semantics=("arbitrary", "arbitrary"),
                                             vmem_limit_bytes=VMEM_LIMIT),
        name="dsa",
    )(p3, p3, sm3, p3, p3, p3, p3, *consts)


def _ssd_kernel(z_ref, xbc_ref, small_ref, cw_ref, cbias_ref, dtb_ref, apad_ref, dexp_ref, ng_ref, tril_ref,
                e64_ref, e128_ref, o_ref, st_ref, halo_ref, buf_ref):
    _tiles_in_order(_ssd_tile, (z_ref, xbc_ref, small_ref),
                    (cw_ref, cbias_ref, dtb_ref, apad_ref, dexp_ref, ng_ref, tril_ref, e64_ref, e128_ref),
                    o_ref, (st_ref, halo_ref, buf_ref))


def _ssd_tile(z_ref, xbc_ref, small_ref, cw_ref, cbias_ref, dtb_ref, apad_ref, dexp_ref, ng_ref, tril_ref,
              e64_ref, e128_ref, o_ref, st_ref, halo_ref, buf_ref):
    nhp = SSD_HEADS * SSD_HD
    gs = SSD_STATE
    act = _silu(_causal_conv(xbc_ref[...].astype(F32), cw_ref, halo_ref, buf_ref) + cbias_ref[...])
    xs = act[:, :nhp]
    bm = act[:, nhp:nhp + SSD_GROUPS * gs]
    cm = act[:, nhp + SSD_GROUPS * gs:]
    dt_full = _softplus(small_ref[...] + dtb_ref[...])
    cs_full = _dot01_l(tril_ref[...], dt_full * apad_ref[...])
    dt_exp = _dot01_r(dt_full, e64_ref[...])
    cs_exp = _dot01_r(cs_full, e64_ref[...])
    cs_lb = _dot01_r(cs_full, e128_ref[...])
    xdt = xs * dt_exp
    cs_last = cs_exp[TILE - 1:TILE, :]
    xds = (xdt * jnp.exp(cs_last - cs_exp)).astype(BF16)
    ecs = jnp.exp(cs_exp)
    xdt_b = xdt.astype(BF16)

    r = lax.broadcasted_iota(I32, (TILE, TILE), 0)
    c = lax.broadcasted_iota(I32, (TILE, TILE), 1)
    incl = r >= c
    lo = c < HALF
    hpg = SSD_HEADS // SSD_GROUPS
    gw = hpg * SSD_HD
    y_parts = []
    for g in range(SSD_GROUPS):
        bmg = bm[:, g * gs:(g + 1) * gs]
        cmg = cm[:, g * gs:(g + 1) * gs].astype(BF16)
        cbg = _dot_nt(cmg, bmg.astype(BF16))
        for mp in range(hpg // 2):
            pair = (g * hpg) // 2 + mp
            xpair = xdt_b[:, pair * LANE:(pair + 1) * LANE]
            ys = []
            for e in range(2):
                h = 2 * pair + e
                cb = cs_lb[:, h * LANE:(h + 1) * LANE]
                lm = jnp.where(incl, jnp.exp(jnp.where(incl, cb - cb.T, 0.0)), 0.0)
                ys.append(_dot((cbg * lm).astype(BF16), xpair))
            y_parts.append(jnp.where(lo, ys[0], ys[1]))
        st = st_ref[g]
        y_off = _dot(cmg, st.astype(BF16)) * ecs[:, g * gw:(g + 1) * gw]
        y_parts[-(hpg // 2):] = [yp + y_off[:, k * LANE:(k + 1) * LANE]
                                 for k, yp in enumerate(y_parts[-(hpg // 2):])]
        new = _dot(bmg.T.astype(BF16), xds[:, g * gw:(g + 1) * gw])
        st_ref[g] = jnp.exp(cs_last[:, g * gw:(g + 1) * gw]) * st + new
    y = jnp.concatenate(y_parts, axis=1) + xs * dexp_ref[...]
    yz = y * _silu(z_ref[...].astype(F32))
    o_ref[...] = _rms(yz, ng_ref[...]).astype(o_ref.dtype)


def _ssd(p3, sm3, conv_w, conv_b, a_log, dt_bias, d_skip, norm_g):
    b, l, _ = p3.shape
    step = min(SEQ_STEP, l)
    consts = [conv_w.astype(F32), conv_b.reshape(1, -1).astype(F32), _pad_lanes(dt_bias, SM_DT),
              _pad_lanes(-jnp.exp(a_log.astype(F32)), SM_DT),
              jnp.repeat(d_skip.astype(F32), SSD_HD).reshape(1, -1), norm_g.reshape(1, -1).astype(F32),
              jnp.asarray(np.tril(np.ones((TILE, TILE), np.float32)), BF16),
              jnp.asarray(_np_lane_expand(SM_DT, SSD_HEADS, SSD_HD), BF16),
              jnp.asarray(_np_lane_expand(SM_DT, SSD_HEADS, LANE), BF16)]
    return pl.pallas_call(
        _ssd_kernel,
        grid=(b, l // step),
        in_specs=[pl.BlockSpec((None, step, BRANCH_W), lambda i, j: (i, j, P_OFF["d_z"] // BRANCH_W)),
                  pl.BlockSpec((None, step, SSD_XBC), lambda i, j: (i, j, P_OFF["d_xbc"] // SSD_XBC)),
                  pl.BlockSpec((None, step, LANE), lambda i, j: (i, j, 0))]
        + [_full_spec(a.shape) for a in consts],
        out_specs=pl.BlockSpec((None, step, BRANCH_W), lambda i, j: (i, j, 0)),
        out_shape=jax.ShapeDtypeStruct((b, l, BRANCH_W), BF16),
        scratch_shapes=[pltpu.VMEM((SSD_GROUPS, SSD_STATE, (SSD_HEADS // SSD_GROUPS) * SSD_HD), F32),
                        pltpu.VMEM((8, SSD_XBC), F32), pltpu.VMEM((TILE + 8, SSD_XBC), F32)],
        compiler_params=pltpu.CompilerParams(dimension_semantics=("arbitrary", "arbitrary"),
                                             vmem_limit_bytes=VMEM_LIMIT),
        name="ssd",
    )(p3, p3, sm3, *consts)


def _mem_kv_kernel(mem_ref, g_ref, w_ref, kg_ref, k_ref, v_ref):
    kv = _dot(_rms(mem_ref[...], g_ref[...]).astype(BF16), w_ref[...])
    for h in range(MEM_HEADS):
        hs = slice(h * MEM_HD, (h + 1) * MEM_HD)
        k_ref[:, hs] = _rms(kv[:, hs], kg_ref[...]).astype(k_ref.dtype)
    v_ref[...] = kv[:, BRANCH_W:].astype(v_ref.dtype)


def _mem_kv(mem, mem_norm_g, w_kv_bf16, k_norm_g):
    b, m, _ = mem.shape
    consts = [mem_norm_g.reshape(1, -1).astype(F32), w_kv_bf16, k_norm_g.reshape(1, -1).astype(F32)]
    out = jax.ShapeDtypeStruct((b, m, BRANCH_W), BF16)
    return pl.pallas_call(
        _mem_kv_kernel,
        grid=(b,),
        in_specs=[pl.BlockSpec((None, m, D_MODEL), lambda i: (i, 0, 0))] + [_full_spec(a.shape) for a in consts],
        out_specs=[pl.BlockSpec((None, m, BRANCH_W), lambda i: (i, 0, 0))] * 2,
        out_shape=[out, out],
        compiler_params=pltpu.CompilerParams(dimension_semantics=("arbitrary",), vmem_limit_bytes=VMEM_LIMIT),
        name="mem_kv",
    )(mem, *consts)


def _mem_attn_kernel(q_ref, gate_ref, k_ref, v_ref, qg_ref, o_ref):
    q = q_ref[...].astype(F32)
    outs = []
    for h in range(MEM_HEADS):
        hs = slice(h * MEM_HD, (h + 1) * MEM_HD)
        qn = _rms(q[:, hs], qg_ref[...]).astype(BF16)
        lg = _dot_nt(qn, k_ref[:, hs]) * (MEM_HD ** -0.5)
        p = jnp.exp(lg - jnp.max(lg, axis=1, keepdims=True))
        outs.append(_dot(p.astype(BF16), v_ref[:, hs]) / jnp.sum(p, axis=1, keepdims=True))
    o_ref[...] = (jnp.concatenate(outs, axis=1) * _silu(gate_ref[...].astype(F32))).astype(o_ref.dtype)


def _mem_attn(p3, k, v, q_norm_g):
    b, l, _ = p3.shape
    m = k.shape[1]
    tm = min(512, l)
    qg = q_norm_g.reshape(1, -1).astype(F32)
    blk = lambda name: pl.BlockSpec((None, tm, BRANCH_W), lambda i, j: (i, j, P_OFF[name] // BRANCH_W))
    kv = pl.BlockSpec((None, m, BRANCH_W), lambda i, j: (i, 0, 0))
    return pl.pallas_call(
        _mem_attn_kernel,
        grid=(b, l // tm),
        in_specs=[blk("m_q"), blk("m_gate"), kv, kv, _full_spec(qg.shape)],
        out_specs=pl.BlockSpec((None, tm, BRANCH_W), lambda i, j: (i, j, 0)),
        out_shape=jax.ShapeDtypeStruct((b, l, BRANCH_W), BF16),
        compiler_params=pltpu.CompilerParams(dimension_semantics=("arbitrary", "arbitrary"),
                                             vmem_limit_bytes=VMEM_LIMIT),
        name="mem_attn",
    )(p3, p3, k, v, qg)


def _reordered_w_in(w):
    parts, start = [], 0
    cols = _P_COLUMNS
    for end in range(1, len(cols) + 1):
        if end == len(cols) or (cols[end] < 0) != (cols[start] < 0) or (cols[start] >= 0 and cols[end] != cols[end - 1] + 1):
            if cols[start] < 0:
                parts.append(jnp.zeros((w.shape[0], end - start), BF16))
            else:
                parts.append(w[:, int(cols[start]):int(cols[end - 1]) + 1].astype(BF16))
            start = end
    return jnp.concatenate(parts, axis=1)


def kernel(x, mem, norm_g, w_in, gdn_conv_w, gdn_a_log, gdn_dt_bias, gdn_norm_g, sg_ln_g, sg_ln_b, sg_w, sg_b, dsa_q_norm_g, dsa_k_norm_g, ssd_conv_w, ssd_conv_b, ssd_a_log, ssd_dt_bias, ssd_d, ssd_norm_g, mem_norm_g, w_mem_kv, mem_q_norm_g, mem_k_norm_g, w_gate, w_branch, w_out):
    b, l, d = x.shape
    depth = norm_g.shape[0]
    x2d = x.reshape(b * l, d)
    for i in range(depth):
        g = norm_g[i].reshape(1, d).astype(F32)
        p2d, sm2d = _in_proj(x2d, g, _reordered_w_in(w_in[i]))
        p3 = p2d.reshape(b, l, P_MAIN)
        sm3 = sm2d.reshape(b, l, LANE)
        mk, mv = _mem_kv(mem, mem_norm_g[i], w_mem_kv[i].astype(BF16), mem_k_norm_g[i])
        ys = (
            _gdn(p3, sm3, gdn_conv_w[i], gdn_a_log[i], gdn_dt_bias[i], gdn_norm_g[i]),
            _sg(p3, sg_ln_g[i], sg_ln_b[i], sg_w[i], sg_b[i]),
            _dsa(p3, sm3, dsa_q_norm_g[i], dsa_k_norm_g[i]),
            _ssd(p3, sm3, ssd_conv_w[i], ssd_conv_b[i], ssd_a_log[i], ssd_dt_bias[i], ssd_d[i], ssd_norm_g[i]),
            _mem_attn(p3, mk, mv, mem_q_norm_g[i]),
        )
        ys = [y.reshape(b * l, BRANCH_W) for y in ys]
        x2d = _merge(x2d, g, ys, w_gate[i].astype(BF16), w_branch[i].astype(BF16), w_out[i].astype(BF16))
    return x2d.reshape(b, l, d)
```

```python
import functools

import numpy as np
import jax
import jax.numpy as jnp
from jax import lax
from jax.experimental import pallas as pl
from jax.experimental.pallas import tpu as pltpu

F32 = jnp.float32
BF16 = jnp.bfloat16
I32 = jnp.int32

D_MODEL = 1024
N_BRANCH = 5
BRANCH_W = D_MODEL // 2
CONV_K = 4
EPS = 1e-6
GDN_HEADS = 4
GDN_DK = BRANCH_W // GDN_HEADS
GDN_CHUNK = 64
GDN_QKV = 3 * BRANCH_W
SG_CHUNK = 128
SG_GROUPS = 4
DSA_HEADS = 8
DSA_HD = BRANCH_W // DSA_HEADS
IDX_HEADS = 8
IDX_HD = 64
DSA_TOPK_MAX = 256
Q_BLOCK = 128
SSD_HEADS = 8
SSD_HD = BRANCH_W // SSD_HEADS
SSD_GROUPS = 2
SSD_STATE = 128
SSD_CHUNK = 128
SSD_XBC = SSD_HEADS * SSD_HD + 2 * SSD_GROUPS * SSD_STATE
MEM_HEADS = 4
MEM_HD = BRANCH_W // MEM_HEADS

IN_SPLITS = (
    GDN_QKV, GDN_HEADS, GDN_HEADS, BRANCH_W,
    BRANCH_W, BRANCH_W, BRANCH_W,
    DSA_HEADS * DSA_HD, DSA_HD, DSA_HD, IDX_HEADS * IDX_HD, IDX_HD, IDX_HEADS, BRANCH_W,
    BRANCH_W, SSD_XBC, SSD_HEADS,
    MEM_HEADS * MEM_HD, BRANCH_W,
)
_SEG_NAMES = ("a_qkv", "a_a", "a_b", "a_gate", "b_u", "b_v", "b_gate", "c_q", "c_k", "c_v", "c_iq", "c_ik",
              "c_iw", "c_gate", "d_z", "d_xbc", "d_dt", "m_q", "m_gate")
_SEG_START = dict(zip(_SEG_NAMES, np.concatenate([[0], np.cumsum(IN_SPLITS)[:-1]]).tolist()))
_SEG_WIDTH = dict(zip(_SEG_NAMES, IN_SPLITS))

LANE = 128
HALF = LANE // 2
TILE = 128
SEQ_STEP = 4 * TILE
VMEM_LIMIT = 56 * 1024 * 1024

_P_LAYOUT = (("a_qkv", 1536), ("b_u", 512), ("d_xbc", 1024), ("a_gate", 512), ("b_v", 512), ("b_gate", 512),
             ("c_q", 512), ("c_iq", 512), ("c_gate", 512), ("d_z", 512), ("m_q", 512), ("m_gate", 512),
             ("c_kk", 128), ("c_vv", 128), ("c_ikik", 128), ("small", 128))
P_OFF = {}
_o = 0
for _n, _w in _P_LAYOUT:
    P_OFF[_n] = _o
    _o += _w
P_COLS = _o
P_MAIN = P_OFF["small"]
SM_A, SM_B, SM_DT, SM_IW = 0, 4, 8, 16


def _p_columns():
    cols = []
    for name, width in _P_LAYOUT:
        if name in _SEG_START:
            cols += list(range(_SEG_START[name], _SEG_START[name] + width))
        elif name in ("c_kk", "c_vv", "c_ikik"):
            src = {"c_kk": "c_k", "c_vv": "c_v", "c_ikik": "c_ik"}[name]
            one = list(range(_SEG_START[src], _SEG_START[src] + _SEG_WIDTH[src]))
            cols += one + one
        else:
            sm = [-1] * LANE
            for seg, at in (("a_a", SM_A), ("a_b", SM_B), ("d_dt", SM_DT), ("c_iw", SM_IW)):
                for j in range(_SEG_WIDTH[seg]):
                    sm[at + j] = _SEG_START[seg] + j
            cols += sm
    return np.asarray(cols, np.int32)


_P_COLUMNS = _p_columns()


def _dot(a, b, precision=None):
    return jnp.dot(a, b, preferred_element_type=F32, precision=precision)


def _dot_nt(a, b):
    return lax.dot_general(a, b, (((1,), (1,)), ((), ())), preferred_element_type=F32)


def _split3(x):
    hi = x.astype(BF16)
    r1 = x - hi.astype(F32)
    mid = r1.astype(BF16)
    lo = (r1 - mid.astype(F32)).astype(BF16)
    return hi, mid, lo


def _dot01_r(x, e):
    hi, mid, lo = _split3(x)
    return _dot(hi, e) + _dot(mid, e) + _dot(lo, e)


def _dot01_l(e, x):
    hi, mid, lo = _split3(x)
    return _dot(e, hi) + _dot(e, mid) + _dot(e, lo)


def _sigmoid(x):
    return 1.0 / (1.0 + jnp.exp(-x))


def _silu(x):
    return x * _sigmoid(x)


def _softplus(x):
    return jnp.maximum(x, 0.0) + jnp.log1p(jnp.exp(-jnp.abs(x)))


def _gelu_tanh(x):
    return 0.5 * x * (1.0 + jnp.tanh(np.sqrt(2.0 / np.pi).astype(np.float32) * (x + 0.044715 * (x * x * x))))


def _rms(x, g, eps=EPS):
    return x * lax.rsqrt(jnp.mean(x * x, axis=-1, keepdims=True) + eps) * g


def _full_spec(shape):
    nd = len(shape)
    return pl.BlockSpec(shape, lambda *_: (0,) * nd)


def _in_proj_kernel(x_ref, g_ref, w_ref, o_ref, s_ref):
    h = _rms(x_ref[...], g_ref[...]).astype(BF16)
    step = 1024
    for c0 in range(0, P_MAIN, step):
        c1 = min(c0 + step, P_MAIN)
        o_ref[:, c0:c1] = _dot(h, w_ref[:, c0:c1]).astype(o_ref.dtype)
    s_ref[...] = _dot(h, w_ref[:, P_MAIN:])


def _in_proj(x2d, g, w_bf16):
    m = x2d.shape[0]
    tm = min(512, m)
    return pl.pallas_call(
        _in_proj_kernel,
        grid=(m // tm,),
        in_specs=[pl.BlockSpec((tm, D_MODEL), lambda i: (i, 0)),
                  pl.BlockSpec((1, D_MODEL), lambda i: (0, 0)),
                  pl.BlockSpec((D_MODEL, P_COLS), lambda i: (0, 0), pipeline_mode=pl.Buffered(1))],
        out_specs=[pl.BlockSpec((tm, P_MAIN), lambda i: (i, 0)), pl.BlockSpec((tm, LANE), lambda i: (i, 0))],
        out_shape=[jax.ShapeDtypeStruct((m, P_MAIN), BF16), jax.ShapeDtypeStruct((m, LANE), F32)],
        compiler_params=pltpu.CompilerParams(dimension_semantics=("arbitrary",), vmem_limit_bytes=VMEM_LIMIT),
        name="in_proj",
    )(x2d, g, w_bf16)


def _merge_kernel(x_ref, g_ref, y0, y1, y2, y3, y4, wg_ref, wb_ref, wo_ref, o_ref):
    x = x_ref[...]
    h = _rms(x, g_ref[...]).astype(BF16)
    acc = None
    for p, y in enumerate((y0, y1, y2, y3, y4)):
        term = _sigmoid(_dot(h, wg_ref[p])) * _dot(y[...], wb_ref[p])
        acc = term if acc is None else acc + term
    o_ref[...] = x + _dot(acc.astype(BF16), wo_ref[...])


def _merge(x2d, g, ys, wg, wb, wo):
    m = x2d.shape[0]
    tm = min(512, m)
    row = lambda i: (i, 0)
    const = pl.Buffered(1)
    return pl.pallas_call(
        _merge_kernel,
        grid=(m // tm,),
        in_specs=[pl.BlockSpec((tm, D_MODEL), row), pl.BlockSpec((1, D_MODEL), lambda i: (0, 0))]
        + [pl.BlockSpec((tm, BRANCH_W), row)] * N_BRANCH
        + [pl.BlockSpec((N_BRANCH, D_MODEL, D_MODEL), lambda i: (0, 0, 0), pipeline_mode=const),
           pl.BlockSpec((N_BRANCH, BRANCH_W, D_MODEL), lambda i: (0, 0, 0), pipeline_mode=const),
           pl.BlockSpec((D_MODEL, D_MODEL), lambda i: (0, 0), pipeline_mode=const)],
        out_specs=pl.BlockSpec((tm, D_MODEL), row),
        out_shape=jax.ShapeDtypeStruct((m, D_MODEL), F32),
        compiler_params=pltpu.CompilerParams(dimension_semantics=("arbitrary",), vmem_limit_bytes=VMEM_LIMIT),
        name="merge_out",
    )(x2d, g, *ys, wg, wb, wo)


def _causal_conv(x, w_ref, halo_ref, buf_ref):
    buf_ref[0:8, :] = halo_ref[...]
    buf_ref[8:8 + TILE, :] = x
    y = w_ref[CONV_K - 1:CONV_K, :] * x
    for j in range(CONV_K - 1):
        y = y + w_ref[j:j + 1, :] * buf_ref[8 - (CONV_K - 1) + j:8 - (CONV_K - 1) + j + TILE, :]
    halo_ref[...] = x[TILE - 8:TILE, :]
    return y


def _np_lane_expand(first_row, n_heads, width):
    e = np.zeros((LANE, n_heads * width), np.float32)
    for h in range(n_heads):
        e[first_row + h, h * width:(h + 1) * width] = 1.0
    return e


def _pad_lanes(vec, at):
    out = jnp.zeros((1, LANE), F32)
    return lax.dynamic_update_slice(out, vec.reshape(1, -1).astype(F32), (0, at))


def _tiles_in_order(tile_fn, seq_refs, const_refs, o_ref, scratch_refs):
    @pl.when(pl.program_id(1) == 0)
    def _():
        for ref in scratch_refs[:2]:
            ref[...] = jnp.zeros_like(ref)

    def body(t, carry):
        rows = pl.ds(pl.multiple_of(t * TILE, TILE), TILE)
        tile_fn(*[r.at[rows] for r in seq_refs], *const_refs, o_ref.at[rows], *scratch_refs)
        return carry

    lax.fori_loop(0, o_ref.shape[0] // TILE, body, 0)


def _gdn_kernel(qkv_ref, small_ref, gate_ref, cw_ref, apad_ref, dtb_ref, ng_ref, btril_ref, eg_ref, eb_ref,
                o_ref, st_ref, halo_ref, buf_ref):
    _tiles_in_order(_gdn_tile, (qkv_ref, small_ref, gate_ref),
                    (cw_ref, apad_ref, dtb_ref, ng_ref, btril_ref, eg_ref, eb_ref), o_ref, (st_ref, halo_ref, buf_ref))


def _gdn_tile(qkv_ref, small_ref, gate_ref, cw_ref, apad_ref, dtb_ref, ng_ref, btril_ref, eg_ref, eb_ref,
              o_ref, st_ref, halo_ref, buf_ref):
    act = _silu(_causal_conv(qkv_ref[...].astype(F32), cw_ref, halo_ref, buf_ref))
    small = small_ref[...]
    g_full = -apad_ref[...] * _softplus(small + dtb_ref[...])
    beta_full = _sigmoid(small)
    gc_full = _dot01_l(btril_ref[...], g_full)
    g_lb = _dot01_r(gc_full, eg_ref[...])
    b_lb = _dot01_r(beta_full, eb_ref[...])

    r = lax.broadcasted_iota(I32, (TILE, TILE), 0)
    c = lax.broadcasted_iota(I32, (TILE, TILE), 1)
    same = (r >> 6) == (c >> 6)
    incl = same & (r >= c)
    strict = same & (r > c)
    eye = (r == c).astype(F32)
    n_sub = TILE // GDN_CHUNK
    ng = ng_ref[...]
    heads = range(GDN_HEADS)
    q, k, v, cb, bb = [], [], [], [], []
    for h in heads:
        qh = act[:, h * GDN_DK:(h + 1) * GDN_DK]
        kh = act[:, BRANCH_W + h * GDN_DK:BRANCH_W + (h + 1) * GDN_DK]
        q.append(qh * lax.rsqrt(jnp.sum(qh * qh, axis=-1, keepdims=True) + EPS) * (GDN_DK ** -0.5))
        k.append(kh * lax.rsqrt(jnp.sum(kh * kh, axis=-1, keepdims=True) + EPS))
        v.append(act[:, 2 * BRANCH_W + h * GDN_DK:2 * BRANCH_W + (h + 1) * GDN_DK])
        cb.append(g_lb[:, h * LANE:(h + 1) * LANE])
        bb.append(b_lb[:, h * LANE:(h + 1) * LANE])
    dm = [jnp.where(incl, jnp.exp(jnp.where(incl, cb[h] - cb[h].T, 0.0)), 0.0) for h in heads]
    kb = [k[h].astype(BF16) for h in heads]
    kk = [_dot_nt(kb[h], kb[h]) for h in heads]
    qkb = [(_dot_nt(q[h].astype(BF16), kb[h]) * dm[h]).astype(BF16) for h in heads]
    mp = [jnp.where(strict, -(bb[h] * kk[h] * dm[h]), 0.0) for h in heads]
    inv = [eye + mp[h] for h in heads]
    for _ in range(int(np.log2(GDN_CHUNK)) - 1):
        mpb = [mp[h].astype(BF16) for h in heads]
        mp = [_dot(mpb[h], mpb[h]) for h in heads]
        inv = [inv[h] + _dot(inv[h].astype(BF16), mp[h].astype(BF16)) for h in heads]
    eg = [jnp.exp(cb[h]) for h in heads]
    sol = [_dot(inv[h].astype(BF16),
                jnp.concatenate([bb[h] * eg[h] * k[h], bb[h] * v[h]], axis=1).astype(BF16)) for h in heads]
    wk = [sol[h][:, :GDN_DK].astype(BF16) for h in heads]
    u0 = [sol[h][:, GDN_DK:] for h in heads]
    qd = [(q[h] * eg[h]).astype(BF16) for h in heads]
    zero = jnp.zeros((GDN_CHUNK, LANE), F32)
    st = [st_ref[h] for h in heads]
    o_parts = [[] for _ in heads]
    for cc in range(n_sub):
        rows = slice(cc * GDN_CHUNK, (cc + 1) * GDN_CHUNK)
        in_chunk = (r >> 6) == cc
        glast = [cb[h][(cc + 1) * GDN_CHUNK - 1:(cc + 1) * GDN_CHUNK, :] for h in heads]
        stb = [st[h].astype(BF16) for h in heads]
        u = [u0[h][rows] - _dot(wk[h][rows], stb[h]) for h in heads]
        upad = [jnp.concatenate([u[h] if j == cc else zero for j in range(n_sub)], axis=0).astype(BF16)
                for h in heads]
        for h in heads:
            o_parts[h].append(_dot(qd[h][rows], stb[h]) + _dot(qkb[h][rows], upad[h]))
        kd = [jnp.where(in_chunk, k[h] * jnp.exp(jnp.where(in_chunk, glast[h] - cb[h], 0.0)), 0.0) for h in heads]
        st = [jnp.exp(glast[h]) * st[h] + _dot(kd[h].T.astype(BF16), upad[h]) for h in heads]
    for h in heads:
        st_ref[h] = st[h]
    outs = [_rms(jnp.concatenate(o_parts[h], axis=0), ng) for h in heads]
    y = jnp.concatenate(outs, axis=1) * _silu(gate_ref[...].astype(F32))
    o_ref[...] = y.astype(o_ref.dtype)


def _gdn(p3, sm3, conv_w, a_log, dt_bias, norm_g):
    b, l, _ = p3.shape
    step = min(SEQ_STEP, l)
    btril = np.zeros((TILE, TILE), np.float32)
    for i in range(TILE):
        lo = (i // GDN_CHUNK) * GDN_CHUNK
        btril[i, lo:i + 1] = 1.0
    consts = [conv_w.astype(F32), _pad_lanes(jnp.exp(a_log.astype(F32)), SM_A), _pad_lanes(dt_bias, SM_A),
              norm_g.reshape(1, -1).astype(F32), jnp.asarray(btril, BF16),
              jnp.asarray(_np_lane_expand(SM_A, GDN_HEADS, LANE), BF16),
              jnp.asarray(_np_lane_expand(SM_B, GDN_HEADS, LANE), BF16)]
    return pl.pallas_call(
        _gdn_kernel,
        grid=(b, l // step),
        in_specs=[pl.BlockSpec((None, step, GDN_QKV), lambda i, j: (i, j, P_OFF["a_qkv"] // GDN_QKV)),
                  pl.BlockSpec((None, step, LANE), lambda i, j: (i, j, 0)),
                  pl.BlockSpec((None, step, BRANCH_W), lambda i, j: (i, j, P_OFF["a_gate"] // BRANCH_W))]
        + [_full_spec(a.shape) for a in consts],
        out_specs=pl.BlockSpec((None, step, BRANCH_W), lambda i, j: (i, j, 0)),
        out_shape=jax.ShapeDtypeStruct((b, l, BRANCH_W), BF16),
        scratch_shapes=[pltpu.VMEM((GDN_HEADS, GDN_DK, GDN_DK), F32), pltpu.VMEM((8, GDN_QKV), F32),
                        pltpu.VMEM((TILE + 8, GDN_QKV), F32)],
        compiler_params=pltpu.CompilerParams(dimension_semantics=("arbitrary", "arbitrary"),
                                             vmem_limit_bytes=VMEM_LIMIT),
        name="gdn",
    )(p3, sm3, p3, *consts)


SG_TILE = 4 * SG_CHUNK


def _sg_kernel(u_ref, v_ref, gate_ref, lng_ref, lnb_ref, w_ref, bs_ref, o_ref):
    r = lax.broadcasted_iota(I32, (SG_CHUNK, SG_CHUNK), 0)
    c = lax.broadcasted_iota(I32, (SG_CHUNK, SG_CHUNK), 1)
    gw = BRANCH_W // SG_GROUPS
    wc = [jnp.where(r >= c, w_ref[g], 0.0).astype(BF16) for g in range(SG_GROUPS)]
    for t0 in range(0, u_ref.shape[0], SG_CHUNK):
        rows = slice(t0, t0 + SG_CHUNK)
        u = _gelu_tanh(u_ref[rows, :].astype(F32))
        v = _gelu_tanh(v_ref[rows, :].astype(F32))
        mu = jnp.mean(v, axis=-1, keepdims=True)
        var = jnp.mean(jnp.square(v - mu), axis=-1, keepdims=True)
        vn = ((v - mu) * lax.rsqrt(var + 1e-5) * lng_ref[...] + lnb_ref[...]).astype(BF16)
        mixed = jnp.concatenate([_dot(wc[g], vn[:, g * gw:(g + 1) * gw]) for g in range(SG_GROUPS)], axis=1)
        o_ref[rows, :] = (u * (mixed + bs_ref[...]) * _silu(gate_ref[rows, :].astype(F32))).astype(o_ref.dtype)


def _sg(p3, ln_g, ln_b, w_s, b_s):
    b, l, _ = p3.shape
    gw = BRANCH_W // SG_GROUPS
    tile = min(SG_TILE, l)
    bs_full = jnp.repeat(jnp.swapaxes(b_s, 0, 1).astype(F32), gw, axis=1)
    consts = [ln_g.reshape(1, -1).astype(F32), ln_b.reshape(1, -1).astype(F32), w_s.astype(F32), bs_full]
    blk = lambda name: pl.BlockSpec((None, tile, BRANCH_W), lambda i, j: (i, j, P_OFF[name] // BRANCH_W))
    return pl.pallas_call(
        _sg_kernel,
        grid=(b, l // tile),
        in_specs=[blk("b_u"), blk("b_v"), blk("b_gate")] + [_full_spec(a.shape) for a in consts],
        out_specs=pl.BlockSpec((None, tile, BRANCH_W), lambda i, j: (i, j, 0)),
        out_shape=jax.ShapeDtypeStruct((b, l, BRANCH_W), BF16),
        compiler_params=pltpu.CompilerParams(dimension_semantics=("arbitrary", "arbitrary"),
                                             vmem_limit_bytes=VMEM_LIMIT),
        name="spatial_gating",
    )(p3, p3, p3, *consts)


_NEG = -1e30
_INT_MIN = -2 ** 31


def _fold8(x, op):
    parts = [x[r:r + 8] for r in range(0, x.shape[0], 8)]
    while len(parts) > 1:
        parts = [op(parts[k], parts[k + 1]) for k in range(0, len(parts) - 1, 2)] + (
            [parts[-1]] if len(parts) % 2 else [])
    return parts[0]


KEY_CHUNK = 2 * TILE


def _bit_planes(u):
    a = [u[8 * g:8 * g + 8] for g in range(32)]
    j, m = 16, 0x0000FFFF
    while j:
        k = 0
        while k < 32:
            t = (a[k] ^ (a[k + j] >> j)) & m
            a[k] = a[k] ^ t
            a[k + j] = a[k + j] ^ (t << j)
            k = (k + j + 1) & ~j
        j >>= 1
        m = m ^ ((m << j) & 0xFFFFFFFF)
    return [a[31 - b] for b in range(32)]


def _dsa_kernel(q_ref, iq_ref, small_ref, gate_ref, kk_ref, vv_ref, ik_ref, qg_ref, kg_ref, sl_ref,
                o_ref, kn_ref, vt_ref, ikb_ref, key_ref, plane_ref, lg_ref, tau_ref, need_ref, nequal_ref, acc_ref,
                *, top_k):
    i = pl.program_id(1)
    nh = DSA_HEADS
    nc = key_ref.shape[0]
    kc = KEY_CHUNK

    srow = lax.broadcasted_iota(I32, (kc, LANE), 0)
    lane = lax.broadcasted_iota(I32, (kc, LANE), 1)
    qpos = i * TILE + lane
    pos_lanes = (lane == HALF) | (lane == HALF + 1)

    @pl.when(i == 0)
    def _():
        def prep(jc, carry):
            rows = pl.ds(pl.multiple_of(jc * kc, kc), kc)
            kpos = jnp.where(lane == HALF, srow, jc * kc).astype(F32)
            kn = jnp.where(lane < HALF, _rms(kk_ref[rows, :].astype(F32), kg_ref[...]), jnp.where(pos_lanes, kpos, 0.0))
            kn_ref[jc] = kn.astype(BF16)
            v = jnp.where(lane < HALF, vv_ref[rows, :].astype(F32), 1.0)
            vt_ref[jc] = jnp.concatenate([v[r:r + TILE].T for r in range(0, kc, TILE)], axis=1).astype(BF16)
            ikb_ref[jc] = ik_ref[rows, :]
            return carry
        lax.fori_loop(0, nc, prep, 0)

    n_ch = (i + 2) // 2
    n_ch2 = (n_ch + 1) // 2

    @pl.when(n_ch < nc)
    def _():
        key_ref[n_ch] = jnp.full((kc, LANE), _INT_MIN, I32)

    iq = iq_ref[...].astype(F32)
    lane_q = lax.broadcasted_iota(I32, (TILE, LANE), 1)
    lo = lane_q < HALF
    pos_lanes_q = (lane_q == HALF) | (lane_q == HALF + 1)
    iq_st = jnp.concatenate(
        [jnp.where(lo if h % 2 == 0 else ~lo, iq[:, (h // 2) * LANE:(h // 2 + 1) * LANE], 0.0) for h in range(nh)],
        axis=0).astype(BF16)
    small_t = small_ref[...].T * (IDX_HEADS ** -0.5 * IDX_HD ** -0.5)
    iw_row = jnp.concatenate([small_t[SM_IW + h:SM_IW + h + 1, :] for h in range(nh)], axis=1)

    q = q_ref[...].astype(F32)
    q_parts = []
    for mpair in range(nh // 2):
        qp = q[:, mpair * LANE:(mpair + 1) * LANE]
        sq = qp * qp
        s_lo = jnp.sum(jnp.where(lo, sq, 0.0), axis=1, keepdims=True)
        s_hi = jnp.sum(jnp.where(lo, 0.0, sq), axis=1, keepdims=True)
        qn = qp * lax.rsqrt(jnp.where(lo, s_lo, s_hi) * (1.0 / DSA_HD) + EPS) * qg_ref[...] * (DSA_HD ** -0.5)
        for e, qh in enumerate((qn, pltpu.roll(qn, HALF, 1))):
            slope = 2.0 ** (-8.0 * (2 * mpair + e + 1) / nh)
            q_parts.append(jnp.where(lo, qh, jnp.where(pos_lanes_q, slope, 0.0)))
    q_st = jnp.concatenate(q_parts, axis=0).astype(BF16)

    def score_pair(j2, carry):
        jcs = (2 * j2, 2 * j2 + 1)
        raw = [_dot_nt(ikb_ref[jc], iq_st) for jc in jcs]
        for jc in jcs:
            lg_ref[jc] = _dot_nt(kn_ref[jc], q_st)
        for jc, s_raw in zip(jcs, raw):
            s_h = jnp.maximum(s_raw, 0.0) * iw_row
            s = s_h[:, 0:TILE]
            for h in range(1, nh):
                s = s + s_h[:, h * TILE:(h + 1) * TILE]
            bits = pltpu.bitcast(s, I32)
            key = jnp.where(bits < 0, bits ^ 0x7FFFFFFF, bits)
            key = jnp.where(jc * kc + srow <= qpos, key, _INT_MIN)
            key_ref[jc] = key
            planes = _bit_planes(key ^ _INT_MIN)
            for b in range(32):
                plane_ref[jc, b] = planes[b]
        return carry

    lax.fori_loop(0, n_ch2, score_pair, 0)

    tau_ref[...] = jnp.full(tau_ref.shape, _INT_MIN + 1, I32)
    need_ref[...] = jnp.zeros_like(need_ref)
    nequal_ref[...] = jnp.ones_like(nequal_ref)

    @pl.when((i + 1) * TILE > top_k)
    def _():
        def ones_at(live, b):
            acc = jnp.zeros((8, LANE), I32)
            for jc in range(nc):
                acc = acc + lax.population_count(live[jc] & plane_ref[jc, b])
            return jnp.sum(acc.astype(F32), axis=0, keepdims=True)

        live0 = tuple(jnp.where(jc < 2 * n_ch2, jnp.full((8, LANE), -1, I32), 0) for jc in range(nc))

        def bit_body(bi, carry):
            live, cnt, left, tau_u = carry
            b = 31 - bi
            take = cnt >= left
            left = jnp.where(take, left, left - cnt)
            tau_u = jnp.where(take, tau_u | lax.shift_left(jnp.int32(1), b), tau_u)
            b_next = jnp.maximum(b - 1, 0)
            acc = jnp.zeros((8, LANE), I32)
            new_live = []
            for jc in range(nc):
                with_bit = live[jc] & plane_ref[jc, b]
                keep = jnp.where(take, with_bit, live[jc] ^ with_bit)
                new_live.append(keep)
                acc = acc + lax.population_count(keep & plane_ref[jc, b_next])
            return tuple(new_live), jnp.sum(acc.astype(F32), axis=0, keepdims=True), left, tau_u

        init = (live0, ones_at(live0, 31), jnp.full((1, LANE), float(top_k), F32), jnp.zeros((1, LANE), I32))
        live, _, left, tau_u = lax.fori_loop(0, 32, bit_body, init)
        n_equal = jnp.zeros((8, LANE), I32)
        for jc in range(nc):
            n_equal = n_equal + lax.population_count(live[jc])
        n_equal = jnp.sum(n_equal.astype(F32), axis=0, keepdims=True)
        takes_all = qpos[0:1, :] + 1 <= top_k
        tau_ref[...] = jnp.broadcast_to(jnp.where(takes_all, _INT_MIN + 1, tau_u ^ _INT_MIN), tau_ref.shape)
        need_ref[...] = jnp.broadcast_to(jnp.where(takes_all, 0.0, left), need_ref.shape)
        nequal_ref[...] = jnp.broadcast_to(jnp.where(takes_all, 1.0, n_equal), nequal_ref.shape)

    tau = tau_ref[0:1, :]
    need = need_ref[0:1, :]

    def mask_chunk_no_ties(jc, mred):
        sel = key_ref[jc] >= tau
        new = []
        for h in range(nh):
            x = jnp.where(sel, lg_ref[jc, :, h * TILE:(h + 1) * TILE], _NEG)
            lg_ref[jc, :, h * TILE:(h + 1) * TILE] = x
            new.append(jnp.maximum(mred[h], _fold8(x, jnp.maximum)))
        return tuple(new)

    def mask_chunk(jc, carry):
        run, mred = carry
        key = key_ref[jc]
        eq = key == tau
        eqf = jnp.where(eq, 1.0, 0.0)
        prefix = _dot(sl_ref[...], eqf.astype(BF16)) + run
        sel = (key > tau) | (eq & (prefix < need))
        new = []
        for h in range(nh):
            x = jnp.where(sel, lg_ref[jc, :, h * TILE:(h + 1) * TILE], _NEG)
            lg_ref[jc, :, h * TILE:(h + 1) * TILE] = x
            new.append(jnp.maximum(mred[h], _fold8(x, jnp.maximum)))
        return run + jnp.sum(_fold8(eqf, jnp.add), axis=0, keepdims=True), tuple(new)

    mred0 = tuple(jnp.full((8, LANE), _NEG, F32) for _ in range(nh))
    mred = lax.cond(
        jnp.max(nequal_ref[...]) <= 1.0,
        lambda: lax.fori_loop(0, 2 * n_ch2, mask_chunk_no_ties, mred0),
        lambda: lax.fori_loop(0, 2 * n_ch2, mask_chunk, (jnp.zeros((1, LANE), F32), mred0))[1])
    m_row = [jnp.max(mh, axis=0, keepdims=True) for mh in mred]

    acc_ref[...] = jnp.zeros_like(acc_ref)

    def pv_pair(j2, carry):
        pv = None
        for jc in (2 * j2, 2 * j2 + 1):
            ps = [jnp.exp((lg_ref[jc, :, h * TILE:(h + 1) * TILE] - m_row[h]).astype(BF16)) for h in range(nh)]
            term = _dot(vt_ref[jc], jnp.concatenate(ps, axis=1))
            pv = term if pv is None else pv + term
        acc_ref[...] += pv
        return carry

    lax.fori_loop(0, n_ch2, pv_pair, 0)

    heads = []
    for h in range(nh):
        blk = acc_ref[:, h * TILE:(h + 1) * TILE]
        o_t = (blk / blk[HALF:HALF + 1, :]).T
        heads.append(o_t if h % 2 == 0 else pltpu.roll(o_t, HALF, 1))
    pairs = [jnp.where(lo, heads[2 * mp], heads[2 * mp + 1]) for mp in range(nh // 2)]
    o_ref[...] = (jnp.concatenate(pairs, axis=1) * _silu(gate_ref[...].astype(F32))).astype(o_ref.dtype)


def _dsa(p3, sm3, q_norm_g, k_norm_g):
    b, l, _ = p3.shape
    top_k = min(DSA_TOPK_MAX, l // 4)
    nb = l // TILE
    nc = l // KEY_CHUNK
    sl = np.tril(np.ones((KEY_CHUNK, KEY_CHUNK), np.float32), -1)
    dup = lambda g: jnp.concatenate([g, g]).reshape(1, LANE).astype(F32)
    consts = [dup(q_norm_g), dup(k_norm_g), jnp.asarray(sl, BF16)]
    qblk = lambda name: pl.BlockSpec((None, TILE, BRANCH_W), lambda i, j: (i, j, P_OFF[name] // BRANCH_W))
    seq = lambda name: pl.BlockSpec((None, l, LANE), lambda i, j: (i, 0, P_OFF[name] // LANE),
                                    pipeline_mode=pl.Buffered(1))
    chunks = lambda dt: pltpu.VMEM((nc, KEY_CHUNK, LANE), dt)
    return pl.pallas_call(
        functools.partial(_dsa_kernel, top_k=top_k),
        grid=(b, nb),
        in_specs=[qblk("c_q"), qblk("c_iq"),
                  pl.BlockSpec((None, TILE, LANE), lambda i, j: (i, j, 0)),
                  qblk("c_gate"), seq("c_kk"), seq("c_vv"), seq("c_ikik")]
        + [_full_spec(a.shape) for a in consts],
        out_specs=pl.BlockSpec((None, TILE, BRANCH_W), lambda i, j: (i, j, 0)),
        out_shape=jax.ShapeDtypeStruct((b, l, BRANCH_W), BF16),
        scratch_shapes=[chunks(BF16), pltpu.VMEM((nc, LANE, KEY_CHUNK), BF16), chunks(BF16), chunks(I32),
                        pltpu.VMEM((nc, 32, 8, LANE), I32),
                        pltpu.VMEM((nc, KEY_CHUNK, DSA_HEADS * TILE), F32), pltpu.VMEM((8, LANE), I32),
                        pltpu.VMEM((8, LANE), F32), pltpu.VMEM((8, LANE), F32),
                        pltpu.VMEM((LANE, DSA_HEADS * TILE), F32)],
        compiler_params=pltpu.CompilerParams(dimension_semantics=("arbitrary", "arbitrary"),
                                             vmem_limit_bytes=VMEM_LIMIT),
        name="dsa",
    )(p3, p3, sm3, p3, p3, p3, p3, *consts)


def _ssd_kernel(z_ref, xbc_ref, small_ref, cw_ref, cbias_ref, dtb_ref, apad_ref, dexp_ref, ng_ref, tril_ref,
                e64_ref, e128_ref, o_ref, st_ref, halo_ref, buf_ref):
    _tiles_in_order(_ssd_tile, (z_ref, xbc_ref, small_ref),
                    (cw_ref, cbias_ref, dtb_ref, apad_ref, dexp_ref, ng_ref, tril_ref, e64_ref, e128_ref),
                    o_ref, (st_ref, halo_ref, buf_ref))


def _ssd_tile(z_ref, xbc_ref, small_ref, cw_ref, cbias_ref, dtb_ref, apad_ref, dexp_ref, ng_ref, tril_ref,
              e64_ref, e128_ref, o_ref, st_ref, halo_ref, buf_ref):
    nhp = SSD_HEADS * SSD_HD
    gs = SSD_STATE
    act = _silu(_causal_conv(xbc_ref[...].astype(F32), cw_ref, halo_ref, buf_ref) + cbias_ref[...])
    xs = act[:, :nhp]
    bm = act[:, nhp:nhp + SSD_GROUPS * gs]
    cm = act[:, nhp + SSD_GROUPS * gs:]
    dt_full = _softplus(small_ref[...] + dtb_ref[...])
    cs_full = _dot01_l(tril_ref[...], dt_full * apad_ref[...])
    dt_exp = _dot01_r(dt_full, e64_ref[...])
    cs_exp = _dot01_r(cs_full, e64_ref[...])
    cs_lb = _dot01_r(cs_full, e128_ref[...])
    xdt = xs * dt_exp
    cs_last = cs_exp[TILE - 1:TILE, :]
    xds = (xdt * jnp.exp(cs_last - cs_exp)).astype(BF16)
    ecs = jnp.exp(cs_exp)
    xdt_b = xdt.astype(BF16)

    r = lax.broadcasted_iota(I32, (TILE, TILE), 0)
    c = lax.broadcasted_iota(I32, (TILE, TILE), 1)
    incl = r >= c
    lo = c < HALF
    hpg = SSD_HEADS // SSD_GROUPS
    gw = hpg * SSD_HD
    y_parts = []
    for g in range(SSD_GROUPS):
        bmg = bm[:, g * gs:(g + 1) * gs]
        cmg = cm[:, g * gs:(g + 1) * gs].astype(BF16)
        cbg = _dot_nt(cmg, bmg.astype(BF16))
        for mp in range(hpg // 2):
            pair = (g * hpg) // 2 + mp
            xpair = xdt_b[:, pair * LANE:(pair + 1) * LANE]
            ys = []
            for e in range(2):
                h = 2 * pair + e
                cb = cs_lb[:, h * LANE:(h + 1) * LANE]
                lm = jnp.where(incl, jnp.exp(jnp.where(incl, cb - cb.T, 0.0)), 0.0)
                ys.append(_dot((cbg * lm).astype(BF16), xpair))
            y_parts.append(jnp.where(lo, ys[0], ys[1]))
        st = st_ref[g]
        y_off = _dot(cmg, st.astype(BF16)) * ecs[:, g * gw:(g + 1) * gw]
        y_parts[-(hpg // 2):] = [yp + y_off[:, k * LANE:(k + 1) * LANE]
                                 for k, yp in enumerate(y_parts[-(hpg // 2):])]
        new = _dot(bmg.T.astype(BF16), xds[:, g * gw:(g + 1) * gw])
        st_ref[g] = jnp.exp(cs_last[:, g * gw:(g + 1) * gw]) * st + new
    y = jnp.concatenate(y_parts, axis=1) + xs * dexp_ref[...]
    yz = y * _silu(z_ref[...].astype(F32))
    o_ref[...] = _rms(yz, ng_ref[...]).astype(o_ref.dtype)


def _ssd(p3, sm3, conv_w, conv_b, a_log, dt_bias, d_skip, norm_g):
    b, l, _ = p3.shape
    step = min(SEQ_STEP, l)
    consts = [conv_w.astype(F32), conv_b.reshape(1, -1).astype(F32), _pad_lanes(dt_bias, SM_DT),
              _pad_lanes(-jnp.exp(a_log.astype(F32)), SM_DT),
              jnp.repeat(d_skip.astype(F32), SSD_HD).reshape(1, -1), norm_g.reshape(1, -1).astype(F32),
              jnp.asarray(np.tril(np.ones((TILE, TILE), np.float32)), BF16),
              jnp.asarray(_np_lane_expand(SM_DT, SSD_HEADS, SSD_HD), BF16),
              jnp.asarray(_np_lane_expand(SM_DT, SSD_HEADS, LANE), BF16)]
    return pl.pallas_call(
        _ssd_kernel,
        grid=(b, l // step),
        in_specs=[pl.BlockSpec((None, step, BRANCH_W), lambda i, j: (i, j, P_OFF["d_z"] // BRANCH_W)),
                  pl.BlockSpec((None, step, SSD_XBC), lambda i, j: (i, j, P_OFF["d_xbc"] // SSD_XBC)),
                  pl.BlockSpec((None, step, LANE), lambda i, j: (i, j, 0))]
        + [_full_spec(a.shape) for a in consts],
        out_specs=pl.BlockSpec((None, step, BRANCH_W), lambda i, j: (i, j, 0)),
        out_shape=jax.ShapeDtypeStruct((b, l, BRANCH_W), BF16),
        scratch_shapes=[pltpu.VMEM((SSD_GROUPS, SSD_STATE, (SSD_HEADS // SSD_GROUPS) * SSD_HD), F32),
                        pltpu.VMEM((8, SSD_XBC), F32), pltpu.VMEM((TILE + 8, SSD_XBC), F32)],
        compiler_params=pltpu.CompilerParams(dimension_semantics=("arbitrary", "arbitrary"),
                                             vmem_limit_bytes=VMEM_LIMIT),
        name="ssd",
    )(p3, p3, sm3, *consts)


def _mem_kv_kernel(mem_ref, g_ref, w_ref, kg_ref, k_ref, v_ref):
    kv = _dot(_rms(mem_ref[...], g_ref[...]).astype(BF16), w_ref[...])
    for h in range(MEM_HEADS):
        hs = slice(h * MEM_HD, (h + 1) * MEM_HD)
        k_ref[:, hs] = _rms(kv[:, hs], kg_ref[...]).astype(k_ref.dtype)
    v_ref[...] = kv[:, BRANCH_W:].astype(v_ref.dtype)


def _mem_kv(mem, mem_norm_g, w_kv_bf16, k_norm_g):
    b, m, _ = mem.shape
    consts = [mem_norm_g.reshape(1, -1).astype(F32), w_kv_bf16, k_norm_g.reshape(1, -1).astype(F32)]
    out = jax.ShapeDtypeStruct((b, m, BRANCH_W), BF16)
    return pl.pallas_call(
        _mem_kv_kernel,
        grid=(b,),
        in_specs=[pl.BlockSpec((None, m, D_MODEL), lambda i: (i, 0, 0))] + [_full_spec(a.shape) for a in consts],
        out_specs=[pl.BlockSpec((None, m, BRANCH_W), lambda i: (i, 0, 0))] * 2,
        out_shape=[out, out],
        compiler_params=pltpu.CompilerParams(dimension_semantics=("arbitrary",), vmem_limit_bytes=VMEM_LIMIT),
        name="mem_kv",
    )(mem, *consts)


def _mem_attn_kernel(q_ref, gate_ref, k_ref, v_ref, qg_ref, o_ref):
    q = q_ref[...].astype(F32)
    outs = []
    for h in range(MEM_HEADS):
        hs = slice(h * MEM_HD, (h + 1) * MEM_HD)
        qn = _rms(q[:, hs], qg_ref[...]).astype(BF16)
        lg = _dot_nt(qn, k_ref[:, hs]) * (MEM_HD ** -0.5)
        p = jnp.exp(lg - jnp.max(lg, axis=1, keepdims=True))
        outs.append(_dot(p.astype(BF16), v_ref[:, hs]) / jnp.sum(p, axis=1, keepdims=True))
    o_ref[...] = (jnp.concatenate(outs, axis=1) * _silu(gate_ref[...].astype(F32))).astype(o_ref.dtype)


def _mem_attn(p3, k, v, q_norm_g):
    b, l, _ = p3.shape
    m = k.shape[1]
    tm = min(512, l)
    qg = q_norm_g.reshape(1, -1).astype(F32)
    blk = lambda name: pl.BlockSpec((None, tm, BRANCH_W), lambda i, j: (i, j, P_OFF[name] // BRANCH_W))
    kv = pl.BlockSpec((None, m, BRANCH_W), lambda i, j: (i, 0, 0))
    return pl.pallas_call(
        _mem_attn_kernel,
        grid=(b, l // tm),
        in_specs=[blk("m_q"), blk("m_gate"), kv, kv, _full_spec(qg.shape)],
        out_specs=pl.BlockSpec((None, tm, BRANCH_W), lambda i, j: (i, j, 0)),
        out_shape=jax.ShapeDtypeStruct((b, l, BRANCH_W), BF16),
        compiler_params=pltpu.CompilerParams(dimension_semantics=("arbitrary", "arbitrary"),
                                             vmem_limit_bytes=VMEM_LIMIT),
        name="mem_attn",
    )(p3, p3, k, v, qg)


def _reordered_w_in(w):
    parts, start = [], 0
    cols = _P_COLUMNS
    for end in range(1, len(cols) + 1):
        if end == len(cols) or (cols[end] < 0) != (cols[start] < 0) or (cols[start] >= 0 and cols[end] != cols[end - 1] + 1):
            if cols[start] < 0:
                parts.append(jnp.zeros((w.shape[0], end - start), BF16))
            else:
                parts.append(w[:, int(cols[start]):int(cols[end - 1]) + 1].astype(BF16))
            start = end
    return jnp.concatenate(parts, axis=1)


def kernel(x, mem, norm_g, w_in, gdn_conv_w, gdn_a_log, gdn_dt_bias, gdn_norm_g, sg_ln_g, sg_ln_b, sg_w, sg_b, dsa_q_norm_g, dsa_k_norm_g, ssd_conv_w, ssd_conv_b, ssd_a_log, ssd_dt_bias, ssd_d, ssd_norm_g, mem_norm_g, w_mem_kv, mem_q_norm_g, mem_k_norm_g, w_gate, w_branch, w_out):
    b, l, d = x.shape
    depth = norm_g.shape[0]
    x2d = x.reshape(b * l, d)
    for i in range(depth):
        g = norm_g[i].reshape(1, d).astype(F32)
        p2d, sm2d = _in_proj(x2d, g, _reordered_w_in(w_in[i]))
        p3 = p2d.reshape(b, l, P_MAIN)
        sm3 = sm2d.reshape(b, l, LANE)
        mk, mv = _mem_kv(mem, mem_norm_g[i], w_mem_kv[i].astype(BF16), mem_k_norm_g[i])
        ys = (
            _gdn(p3, sm3, gdn_conv_w[i], gdn_a_log[i], gdn_dt_bias[i], gdn_norm_g[i]),
            _sg(p3, sg_ln_g[i], sg_ln_b[i], sg_w[i], sg_b[i]),
            _dsa(p3, sm3, dsa_q_norm_g[i], dsa_k_norm_g[i]),
            _ssd(p3, sm3, ssd_conv_w[i], ssd_conv_b[i], ssd_a_log[i], ssd_dt_bias[i], ssd_d[i], ssd_norm_g[i]),
            _mem_attn(p3, mk, mv, mem_q_norm_g[i]),
        )
        ys = [y.reshape(b * l, BRANCH_W) for y in ys]
        x2d = _merge(x2d, g, ys, w_gate[i].astype(BF16), w_branch[i].astype(BF16), w_out[i].astype(BF16))
    return x2d.reshape(b, l, d)
```

```python
import functools

import numpy as np
import jax
import jax.numpy as jnp
from jax import lax
from jax.experimental import pallas as pl
from jax.experimental.pallas import tpu as pltpu

F32 = jnp.float32
BF16 = jnp.bfloat16
I32 = jnp.int32

D_MODEL = 1024
N_BRANCH = 5
BRANCH_W = D_MODEL // 2
CONV_K = 4
EPS = 1e-6
GDN_HEADS = 4
GDN_DK = BRANCH_W // GDN_HEADS
GDN_CHUNK = 64
GDN_QKV = 3 * BRANCH_W
SG_CHUNK = 128
SG_GROUPS = 4
DSA_HEADS = 8
DSA_HD = BRANCH_W // DSA_HEADS
IDX_HEADS = 8
IDX_HD = 64
DSA_TOPK_MAX = 256
Q_BLOCK = 128
SSD_HEADS = 8
SSD_HD = BRANCH_W // SSD_HEADS
SSD_GROUPS = 2
SSD_STATE = 128
SSD_CHUNK = 128
SSD_XBC = SSD_HEADS * SSD_HD + 2 * SSD_GROUPS * SSD_STATE
MEM_HEADS = 4
MEM_HD = BRANCH_W // MEM_HEADS

IN_SPLITS = (
    GDN_QKV, GDN_HEADS, GDN_HEADS, BRANCH_W,
    BRANCH_W, BRANCH_W, BRANCH_W,
    DSA_HEADS * DSA_HD, DSA_HD, DSA_HD, IDX_HEADS * IDX_HD, IDX_HD, IDX_HEADS, BRANCH_W,
    BRANCH_W, SSD_XBC, SSD_HEADS,
    MEM_HEADS * MEM_HD, BRANCH_W,
)
_SEG_NAMES = ("a_qkv", "a_a", "a_b", "a_gate", "b_u", "b_v", "b_gate", "c_q", "c_k", "c_v", "c_iq", "c_ik",
              "c_iw", "c_gate", "d_z", "d_xbc", "d_dt", "m_q", "m_gate")
_SEG_START = dict(zip(_SEG_NAMES, np.concatenate([[0], np.cumsum(IN_SPLITS)[:-1]]).tolist()))
_SEG_WIDTH = dict(zip(_SEG_NAMES, IN_SPLITS))

LANE = 128
HALF = LANE // 2
TILE = 128
SEQ_STEP = 4 * TILE
VMEM_LIMIT = 56 * 1024 * 1024

_P_LAYOUT = (("a_qkv", 1536), ("b_u", 512), ("d_xbc", 1024), ("a_gate", 512), ("b_v", 512), ("b_gate", 512),
             ("c_q", 512), ("c_iq", 512), ("c_gate", 512), ("d_z", 512), ("m_q", 512), ("m_gate", 512),
             ("c_kk", 128), ("c_vv", 128), ("c_ikik", 128), ("small", 128))
P_OFF = {}
_o = 0
for _n, _w in _P_LAYOUT:
    P_OFF[_n] = _o
    _o += _w
P_COLS = _o
P_MAIN = P_OFF["small"]
SM_A, SM_B, SM_DT, SM_IW = 0, 4, 8, 16


def _p_columns():
    cols = []
    for name, width in _P_LAYOUT:
        if name in _SEG_START:
            cols += list(range(_SEG_START[name], _SEG_START[name] + width))
        elif name in ("c_kk", "c_vv", "c_ikik"):
            src = {"c_kk": "c_k", "c_vv": "c_v", "c_ikik": "c_ik"}[name]
            one = list(range(_SEG_START[src], _SEG_START[src] + _SEG_WIDTH[src]))
            cols += one + one
        else:
            sm = [-1] * LANE
            for seg, at in (("a_a", SM_A), ("a_b", SM_B), ("d_dt", SM_DT), ("c_iw", SM_IW)):
                for j in range(_SEG_WIDTH[seg]):
                    sm[at + j] = _SEG_START[seg] + j
            cols += sm
    return np.asarray(cols, np.int32)


_P_COLUMNS = _p_columns()


def _dot(a, b, precision=None):
    return jnp.dot(a, b, preferred_element_type=F32, precision=precision)


def _dot_nt(a, b):
    return lax.dot_general(a, b, (((1,), (1,)), ((), ())), preferred_element_type=F32)


def _split3(x):
    hi = x.astype(BF16)
    r1 = x - hi.astype(F32)
    mid = r1.astype(BF16)
    lo = (r1 - mid.astype(F32)).astype(BF16)
    return hi, mid, lo


def _dot01_r(x, e):
    hi, mid, lo = _split3(x)
    return _dot(hi, e) + _dot(mid, e) + _dot(lo, e)


def _dot01_l(e, x):
    hi, mid, lo = _split3(x)
    return _dot(e, hi) + _dot(e, mid) + _dot(e, lo)


def _sigmoid(x):
    return 1.0 / (1.0 + jnp.exp(-x))


def _silu(x):
    return x * _sigmoid(x)


def _softplus(x):
    return jnp.maximum(x, 0.0) + jnp.log1p(jnp.exp(-jnp.abs(x)))


def _gelu_tanh(x):
    return 0.5 * x * (1.0 + jnp.tanh(np.sqrt(2.0 / np.pi).astype(np.float32) * (x + 0.044715 * (x * x * x))))


def _rms(x, g, eps=EPS):
    return x * lax.rsqrt(jnp.mean(x * x, axis=-1, keepdims=True) + eps) * g


def _full_spec(shape):
    nd = len(shape)
    return pl.BlockSpec(shape, lambda *_: (0,) * nd)


def _in_proj_kernel(x_ref, g_ref, w_ref, o_ref, s_ref):
    h = _rms(x_ref[...], g_ref[...]).astype(BF16)
    step = 1024
    for c0 in range(0, P_MAIN, step):
        c1 = min(c0 + step, P_MAIN)
        o_ref[:, c0:c1] = _dot(h, w_ref[:, c0:c1]).astype(o_ref.dtype)
    s_ref[...] = _dot(h, w_ref[:, P_MAIN:])


def _in_proj(x2d, g, w_bf16):
    m = x2d.shape[0]
    tm = min(512, m)
    return pl.pallas_call(
        _in_proj_kernel,
        grid=(m // tm,),
        in_specs=[pl.BlockSpec((tm, D_MODEL), lambda i: (i, 0)),
                  pl.BlockSpec((1, D_MODEL), lambda i: (0, 0)),
                  pl.BlockSpec((D_MODEL, P_COLS), lambda i: (0, 0), pipeline_mode=pl.Buffered(1))],
        out_specs=[pl.BlockSpec((tm, P_MAIN), lambda i: (i, 0)), pl.BlockSpec((tm, LANE), lambda i: (i, 0))],
        out_shape=[jax.ShapeDtypeStruct((m, P_MAIN), BF16), jax.ShapeDtypeStruct((m, LANE), F32)],
        compiler_params=pltpu.CompilerParams(dimension_semantics=("arbitrary",), vmem_limit_bytes=VMEM_LIMIT),
        name="in_proj",
    )(x2d, g, w_bf16)


def _merge_kernel(x_ref, g_ref, y0, y1, y2, y3, y4, wg_ref, wb_ref, wo_ref, o_ref):
    x = x_ref[...]
    h = _rms(x, g_ref[...]).astype(BF16)
    acc = None
    for p, y in enumerate((y0, y1, y2, y3, y4)):
        term = _sigmoid(_dot(h, wg_ref[p])) * _dot(y[...], wb_ref[p])
        acc = term if acc is None else acc + term
    o_ref[...] = x + _dot(acc.astype(BF16), wo_ref[...])


def _merge(x2d, g, ys, wg, wb, wo):
    m = x2d.shape[0]
    tm = min(512, m)
    row = lambda i: (i, 0)
    const = pl.Buffered(1)
    return pl.pallas_call(
        _merge_kernel,
        grid=(m // tm,),
        in_specs=[pl.BlockSpec((tm, D_MODEL), row), pl.BlockSpec((1, D_MODEL), lambda i: (0, 0))]
        + [pl.BlockSpec((tm, BRANCH_W), row)] * N_BRANCH
        + [pl.BlockSpec((N_BRANCH, D_MODEL, D_MODEL), lambda i: (0, 0, 0), pipeline_mode=const),
           pl.BlockSpec((N_BRANCH, BRANCH_W, D_MODEL), lambda i: (0, 0, 0), pipeline_mode=const),
           pl.BlockSpec((D_MODEL, D_MODEL), lambda i: (0, 0), pipeline_mode=const)],
        out_specs=pl.BlockSpec((tm, D_MODEL), row),
        out_shape=jax.ShapeDtypeStruct((m, D_MODEL), F32),
        compiler_params=pltpu.CompilerParams(dimension_semantics=("arbitrary",), vmem_limit_bytes=VMEM_LIMIT),
        name="merge_out",
    )(x2d, g, *ys, wg, wb, wo)


def _causal_conv(x, w_ref, halo_ref, buf_ref):
    buf_ref[0:8, :] = halo_ref[...]
    buf_ref[8:8 + TILE, :] = x
    y = w_ref[CONV_K - 1:CONV_K, :] * x
    for j in range(CONV_K - 1):
        y = y + w_ref[j:j + 1, :] * buf_ref[8 - (CONV_K - 1) + j:8 - (CONV_K - 1) + j + TILE, :]
    halo_ref[...] = x[TILE - 8:TILE, :]
    return y


def _np_lane_expand(first_row, n_heads, width):
    e = np.zeros((LANE, n_heads * width), np.float32)
    for h in range(n_heads):
        e[first_row + h, h * width:(h + 1) * width] = 1.0
    return e


def _pad_lanes(vec, at):
    out = jnp.zeros((1, LANE), F32)
    return lax.dynamic_update_slice(out, vec.reshape(1, -1).astype(F32), (0, at))


def _tiles_in_order(tile_fn, seq_refs, const_refs, o_ref, scratch_refs):
    @pl.when(pl.program_id(1) == 0)
    def _():
        for ref in scratch_refs[:2]:
            ref[...] = jnp.zeros_like(ref)

    def body(t, carry):
        rows = pl.ds(pl.multiple_of(t * TILE, TILE), TILE)
        tile_fn(*[r.at[rows] for r in seq_refs], *const_refs, o_ref.at[rows], *scratch_refs)
        return carry

    lax.fori_loop(0, o_ref.shape[0] // TILE, body, 0)


def _gdn_kernel(qkv_ref, small_ref, gate_ref, cw_ref, apad_ref, dtb_ref, ng_ref, btril_ref, eg_ref, eb_ref,
                o_ref, st_ref, halo_ref, buf_ref):
    _tiles_in_order(_gdn_tile, (qkv_ref, small_ref, gate_ref),
                    (cw_ref, apad_ref, dtb_ref, ng_ref, btril_ref, eg_ref, eb_ref), o_ref, (st_ref, halo_ref, buf_ref))


def _gdn_tile(qkv_ref, small_ref, gate_ref, cw_ref, apad_ref, dtb_ref, ng_ref, btril_ref, eg_ref, eb_ref,
              o_ref, st_ref, halo_ref, buf_ref):
    act = _silu(_causal_conv(qkv_ref[...].astype(F32), cw_ref, halo_ref, buf_ref))
    small = small_ref[...]
    g_full = -apad_ref[...] * _softplus(small + dtb_ref[...])
    beta_full = _sigmoid(small)
    gc_full = _dot01_l(btril_ref[...], g_full)
    g_lb = _dot01_r(gc_full, eg_ref[...])
    b_lb = _dot01_r(beta_full, eb_ref[...])

    r = lax.broadcasted_iota(I32, (TILE, TILE), 0)
    c = lax.broadcasted_iota(I32, (TILE, TILE), 1)
    same = (r >> 6) == (c >> 6)
    incl = same & (r >= c)
    strict = same & (r > c)
    eye = (r == c).astype(F32)
    n_sub = TILE // GDN_CHUNK
    ng = ng_ref[...]
    heads = range(GDN_HEADS)
    q, k, v, cb, bb = [], [], [], [], []
    for h in heads:
        qh = act[:, h * GDN_DK:(h + 1) * GDN_DK]
        kh = act[:, BRANCH_W + h * GDN_DK:BRANCH_W + (h + 1) * GDN_DK]
        q.append(qh * lax.rsqrt(jnp.sum(qh * qh, axis=-1, keepdims=True) + EPS) * (GDN_DK ** -0.5))
        k.append(kh * lax.rsqrt(jnp.sum(kh * kh, axis=-1, keepdims=True) + EPS))
        v.append(act[:, 2 * BRANCH_W + h * GDN_DK:2 * BRANCH_W + (h + 1) * GDN_DK])
        cb.append(g_lb[:, h * LANE:(h + 1) * LANE])
        bb.append(b_lb[:, h * LANE:(h + 1) * LANE])
    dm = [jnp.where(incl, jnp.exp(jnp.where(incl, cb[h] - cb[h].T, 0.0)), 0.0) for h in heads]
    kb = [k[h].astype(BF16) for h in heads]
    kk = [_dot_nt(kb[h], kb[h]) for h in heads]
    qkb = [(_dot_nt(q[h].astype(BF16), kb[h]) * dm[h]).astype(BF16) for h in heads]
    mp = [jnp.where(strict, -(bb[h] * kk[h] * dm[h]), 0.0) for h in heads]
    inv = [eye + mp[h] for h in heads]
    for _ in range(int(np.log2(GDN_CHUNK)) - 1):
        mpb = [mp[h].astype(BF16) for h in heads]
        mp = [_dot(mpb[h], mpb[h]) for h in heads]
        inv = [inv[h] + _dot(inv[h].astype(BF16), mp[h].astype(BF16)) for h in heads]
    eg = [jnp.exp(cb[h]) for h in heads]
    sol = [_dot(inv[h].astype(BF16),
                jnp.concatenate([bb[h] * eg[h] * k[h], bb[h] * v[h]], axis=1).astype(BF16)) for h in heads]
    wk = [sol[h][:, :GDN_DK].astype(BF16) for h in heads]
    u0 = [sol[h][:, GDN_DK:] for h in heads]
    qd = [(q[h] * eg[h]).astype(BF16) for h in heads]
    zero = jnp.zeros((GDN_CHUNK, LANE), F32)
    st = [st_ref[h] for h in heads]
    o_parts = [[] for _ in heads]
    for cc in range(n_sub):
        rows = slice(cc * GDN_CHUNK, (cc + 1) * GDN_CHUNK)
        in_chunk = (r >> 6) == cc
        glast = [cb[h][(cc + 1) * GDN_CHUNK - 1:(cc + 1) * GDN_CHUNK, :] for h in heads]
        stb = [st[h].astype(BF16) for h in heads]
        u = [u0[h][rows] - _dot(wk[h][rows], stb[h]) for h in heads]
        upad = [jnp.concatenate([u[h] if j == cc else zero for j in range(n_sub)], axis=0).astype(BF16)
                for h in heads]
        for h in heads:
            o_parts[h].append(_dot(qd[h][rows], stb[h]) + _dot(qkb[h][rows], upad[h]))
        kd = [jnp.where(in_chunk, k[h] * jnp.exp(jnp.where(in_chunk, glast[h] - cb[h], 0.0)), 0.0) for h in heads]
        st = [jnp.exp(glast[h]) * st[h] + _dot(kd[h].T.astype(BF16), upad[h]) for h in heads]
    for h in heads:
        st_ref[h] = st[h]
    outs = [_rms(jnp.concatenate(o_parts[h], axis=0), ng) for h in heads]
    y = jnp.concatenate(outs, axis=1) * _silu(gate_ref[...].astype(F32))
    o_ref[...] = y.astype(o_ref.dtype)


def _gdn(p3, sm3, conv_w, a_log, dt_bias, norm_g):
    b, l, _ = p3.shape
    step = min(SEQ_STEP, l)
    btril = np.zeros((TILE, TILE), np.float32)
    for i in range(TILE):
        lo = (i // GDN_CHUNK) * GDN_CHUNK
        btril[i, lo:i + 1] = 1.0
    consts = [conv_w.astype(F32), _pad_lanes(jnp.exp(a_log.astype(F32)), SM_A), _pad_lanes(dt_bias, SM_A),
              norm_g.reshape(1, -1).astype(F32), jnp.asarray(btril, BF16),
              jnp.asarray(_np_lane_expand(SM_A, GDN_HEADS, LANE), BF16),
              jnp.asarray(_np_lane_expand(SM_B, GDN_HEADS, LANE), BF16)]
    return pl.pallas_call(
        _gdn_kernel,
        grid=(b, l // step),
        in_specs=[pl.BlockSpec((None, step, GDN_QKV), lambda i, j: (i, j, P_OFF["a_qkv"] // GDN_QKV)),
                  pl.BlockSpec((None, step, LANE), lambda i, j: (i, j, 0)),
                  pl.BlockSpec((None, step, BRANCH_W), lambda i, j: (i, j, P_OFF["a_gate"] // BRANCH_W))]
        + [_full_spec(a.shape) for a in consts],
        out_specs=pl.BlockSpec((None, step, BRANCH_W), lambda i, j: (i, j, 0)),
        out_shape=jax.ShapeDtypeStruct((b, l, BRANCH_W), BF16),
        scratch_shapes=[pltpu.VMEM((GDN_HEADS, GDN_DK, GDN_DK), F32), pltpu.VMEM((8, GDN_QKV), F32),
                        pltpu.VMEM((TILE + 8, GDN_QKV), F32)],
        compiler_params=pltpu.CompilerParams(dimension_semantics=("arbitrary", "arbitrary"),
                                             vmem_limit_bytes=VMEM_LIMIT),
        name="gdn",
    )(p3, sm3, p3, *consts)


SG_TILE = 4 * SG_CHUNK


def _sg_kernel(u_ref, v_ref, gate_ref, lng_ref, lnb_ref, w_ref, bs_ref, o_ref):
    r = lax.broadcasted_iota(I32, (SG_CHUNK, SG_CHUNK), 0)
    c = lax.broadcasted_iota(I32, (SG_CHUNK, SG_CHUNK), 1)
    gw = BRANCH_W // SG_GROUPS
    wc = [jnp.where(r >= c, w_ref[g], 0.0).astype(BF16) for g in range(SG_GROUPS)]
    for t0 in range(0, u_ref.shape[0], SG_CHUNK):
        rows = slice(t0, t0 + SG_CHUNK)
        u = _gelu_tanh(u_ref[rows, :].astype(F32))
        v = _gelu_tanh(v_ref[rows, :].astype(F32))
        mu = jnp.mean(v, axis=-1, keepdims=True)
        var = jnp.mean(jnp.square(v - mu), axis=-1, keepdims=True)
        vn = ((v - mu) * lax.rsqrt(var + 1e-5) * lng_ref[...] + lnb_ref[...]).astype(BF16)
        mixed = jnp.concatenate([_dot(wc[g], vn[:, g * gw:(g + 1) * gw]) for g in range(SG_GROUPS)], axis=1)
        o_ref[rows, :] = (u * (mixed + bs_ref[...]) * _silu(gate_ref[rows, :].astype(F32))).astype(o_ref.dtype)


def _sg(p3, ln_g, ln_b, w_s, b_s):
    b, l, _ = p3.shape
    gw = BRANCH_W // SG_GROUPS
    tile = min(SG_TILE, l)
    bs_full = jnp.repeat(jnp.swapaxes(b_s, 0, 1).astype(F32), gw, axis=1)
    consts = [ln_g.reshape(1, -1).astype(F32), ln_b.reshape(1, -1).astype(F32), w_s.astype(F32), bs_full]
    blk = lambda name: pl.BlockSpec((None, tile, BRANCH_W), lambda i, j: (i, j, P_OFF[name] // BRANCH_W))
    return pl.pallas_call(
        _sg_kernel,
        grid=(b, l // tile),
        in_specs=[blk("b_u"), blk("b_v"), blk("b_gate")] + [_full_spec(a.shape) for a in consts],
        out_specs=pl.BlockSpec((None, tile, BRANCH_W), lambda i, j: (i, j, 0)),
        out_shape=jax.ShapeDtypeStruct((b, l, BRANCH_W), BF16),
        compiler_params=pltpu.CompilerParams(dimension_semantics=("arbitrary", "arbitrary"),
                                             vmem_limit_bytes=VMEM_LIMIT),
        name="spatial_gating",
    )(p3, p3, p3, *consts)


_NEG = -1e30
_INT_MIN = -2 ** 31


def _fold8(x, op):
    parts = [x[r:r + 8] for r in range(0, x.shape[0], 8)]
    while len(parts) > 1:
        parts = [op(parts[k], parts[k + 1]) for k in range(0, len(parts) - 1, 2)] + (
            [parts[-1]] if len(parts) % 2 else [])
    return parts[0]


KEY_CHUNK = 2 * TILE


def _bit_planes(u):
    a = [u[8 * g:8 * g + 8] for g in range(32)]
    j, m = 16, 0x0000FFFF
    while j:
        k = 0
        while k < 32:
            t = (a[k] ^ (a[k + j] >> j)) & m
            a[k] = a[k] ^ t
            a[k + j] = a[k + j] ^ (t << j)
            k = (k + j + 1) & ~j
        j >>= 1
        m = m ^ ((m << j) & 0xFFFFFFFF)
    return [a[31 - b] for b in range(32)]


def _dsa_kernel(q_ref, iq_ref, small_ref, gate_ref, kk_ref, vv_ref, ik_ref, qg_ref, kg_ref, sl_ref,
                o_ref, kn_ref, vt_ref, ikb_ref, key_ref, plane_ref, lg_ref, tau_ref, need_ref, nequal_ref, acc_ref,
                *, top_k):
    i = pl.program_id(1)
    nh = DSA_HEADS
    nc = key_ref.shape[0]
    kc = KEY_CHUNK

    srow = lax.broadcasted_iota(I32, (kc, LANE), 0)
    lane = lax.broadcasted_iota(I32, (kc, LANE), 1)
    qpos = i * TILE + lane
    pos_lanes = (lane == HALF) | (lane == HALF + 1)

    @pl.when(i == 0)
    def _():
        def prep(jc, carry):
            rows = pl.ds(pl.multiple_of(jc * kc, kc), kc)
            kpos = jnp.where(lane == HALF, srow, jc * kc).astype(F32)
            kn = jnp.where(lane < HALF, _rms(kk_ref[rows, :].astype(F32), kg_ref[...]), jnp.where(pos_lanes, kpos, 0.0))
            kn_ref[jc] = kn.astype(BF16)
            v = jnp.where(lane < HALF, vv_ref[rows, :].astype(F32), 1.0)
            vt_ref[jc] = jnp.concatenate([v[r:r + TILE].T for r in range(0, kc, TILE)], axis=1).astype(BF16)
            ikb_ref[jc] = ik_ref[rows, :]
            plane_ref[jc] = jnp.zeros(plane_ref.shape[1:], I32)
            return carry
        lax.fori_loop(0, nc, prep, 0)

    n_ch = (i + 2) // 2
    n_ch2 = (n_ch + 1) // 2

    iq = iq_ref[...].astype(F32)
    lane_q = lax.broadcasted_iota(I32, (TILE, LANE), 1)
    lo = lane_q < HALF
    pos_lanes_q = (lane_q == HALF) | (lane_q == HALF + 1)
    iq_st = jnp.concatenate(
        [jnp.where(lo if h % 2 == 0 else ~lo, iq[:, (h // 2) * LANE:(h // 2 + 1) * LANE], 0.0) for h in range(nh)],
        axis=0).astype(BF16)
    small_t = small_ref[...].T * (IDX_HEADS ** -0.5 * IDX_HD ** -0.5)
    iw_row = jnp.concatenate([small_t[SM_IW + h:SM_IW + h + 1, :] for h in range(nh)], axis=1)

    q = q_ref[...].astype(F32)
    q_parts = []
    for mpair in range(nh // 2):
        qp = q[:, mpair * LANE:(mpair + 1) * LANE]
        sq = qp * qp
        s_lo = jnp.sum(jnp.where(lo, sq, 0.0), axis=1, keepdims=True)
        s_hi = jnp.sum(jnp.where(lo, 0.0, sq), axis=1, keepdims=True)
        qn = qp * lax.rsqrt(jnp.where(lo, s_lo, s_hi) * (1.0 / DSA_HD) + EPS) * qg_ref[...] * (DSA_HD ** -0.5)
        for e, qh in enumerate((qn, pltpu.roll(qn, HALF, 1))):
            slope = 2.0 ** (-8.0 * (2 * mpair + e + 1) / nh)
            q_parts.append(jnp.where(lo, qh, jnp.where(pos_lanes_q, slope, 0.0)))
    q_st = jnp.concatenate(q_parts, axis=0).astype(BF16)

    def score_pair(j2, carry):
        jcs = (2 * j2, 2 * j2 + 1)
        raw = [_dot_nt(ikb_ref[jc], iq_st) for jc in jcs]
        for jc in jcs:
            lg_ref[jc] = _dot_nt(kn_ref[jc], q_st)
        for jc, s_raw in zip(jcs, raw):
            s_h = jnp.maximum(s_raw, 0.0) * iw_row
            s = s_h[:, 0:TILE]
            for h in range(1, nh):
                s = s + s_h[:, h * TILE:(h + 1) * TILE]
            bits = pltpu.bitcast(s, I32)
            key = jnp.where(bits < 0, bits ^ 0x7FFFFFFF, bits)
            key = jnp.where(jc * kc + srow <= qpos, key, _INT_MIN)
            key_ref[jc] = key
            planes = _bit_planes(key ^ _INT_MIN)
            for b in range(32):
                plane_ref[jc, b] = planes[b]
        return carry

    lax.fori_loop(0, n_ch2, score_pair, 0)

    tau_ref[...] = jnp.full(tau_ref.shape, _INT_MIN + 1, I32)
    need_ref[...] = jnp.zeros_like(need_ref)
    nequal_ref[...] = jnp.ones_like(nequal_ref)

    @pl.when((i + 1) * TILE > top_k)
    def _():
        def ones_at(live, b):
            acc = jnp.zeros((8, LANE), I32)
            for jc in range(nc):
                acc = acc + lax.population_count(live[jc] & plane_ref[jc, b])
            return jnp.sum(acc.astype(F32), axis=0, keepdims=True)

        live0 = tuple(jnp.where(jc < 2 * n_ch2, jnp.full((8, LANE), -1, I32), 0) for jc in range(nc))

        def bit_body(bi, carry):
            live, cnt, left, tau_u = carry
            b = 31 - bi
            take = cnt >= left
            left = jnp.where(take, left, left - cnt)
            tau_u = jnp.where(take, tau_u | lax.shift_left(jnp.int32(1), b), tau_u)
            b_next = jnp.maximum(b - 1, 0)
            acc = jnp.zeros((8, LANE), I32)
            new_live = []
            for jc in range(nc):
                with_bit = live[jc] & plane_ref[jc, b]
                keep = jnp.where(take, with_bit, live[jc] ^ with_bit)
                new_live.append(keep)
                acc = acc + lax.population_count(keep & plane_ref[jc, b_next])
            return tuple(new_live), jnp.sum(acc.astype(F32), axis=0, keepdims=True), left, tau_u

        init = (live0, ones_at(live0, 31), jnp.full((1, LANE), float(top_k), F32), jnp.zeros((1, LANE), I32))
        live, _, left, tau_u = lax.fori_loop(0, 32, bit_body, init)
        n_equal = jnp.zeros((8, LANE), I32)
        for jc in range(nc):
            n_equal = n_equal + lax.population_count(live[jc])
        n_equal = jnp.sum(n_equal.astype(F32), axis=0, keepdims=True)
        takes_all = qpos[0:1, :] + 1 <= top_k
        tau_ref[...] = jnp.broadcast_to(jnp.where(takes_all, _INT_MIN + 1, tau_u ^ _INT_MIN), tau_ref.shape)
        need_ref[...] = jnp.broadcast_to(jnp.where(takes_all, 0.0, left), need_ref.shape)
        nequal_ref[...] = jnp.broadcast_to(jnp.where(takes_all, 1.0, n_equal), nequal_ref.shape)

    tau = tau_ref[0:1, :]
    need = need_ref[0:1, :]

    def mask_chunk_no_ties(jc, mred):
        sel = key_ref[jc] >= tau
        new = []
        for h in range(nh):
            x = jnp.where(sel, lg_ref[jc, :, h * TILE:(h + 1) * TILE], _NEG)
            lg_ref[jc, :, h * TILE:(h + 1) * TILE] = x
            new.append(jnp.maximum(mred[h], _fold8(x, jnp.maximum)))
        return tuple(new)

    def mask_chunk(jc, carry):
        run, mred = carry
        key = key_ref[jc]
        eq = key == tau
        eqf = jnp.where(eq, 1.0, 0.0)
        prefix = _dot(sl_ref[...], eqf.astype(BF16)) + run
        sel = (key > tau) | (eq & (prefix < need))
        new = []
        for h in range(nh):
            x = jnp.where(sel, lg_ref[jc, :, h * TILE:(h + 1) * TILE], _NEG)
            lg_ref[jc, :, h * TILE:(h + 1) * TILE] = x
            new.append(jnp.maximum(mred[h], _fold8(x, jnp.maximum)))
        return run + jnp.sum(_fold8(eqf, jnp.add), axis=0, keepdims=True), tuple(new)

    mred0 = tuple(jnp.full((8, LANE), _NEG, F32) for _ in range(nh))
    mred = lax.cond(
        jnp.max(nequal_ref[...]) <= 1.0,
        lambda: lax.fori_loop(0, 2 * n_ch2, mask_chunk_no_ties, mred0),
        lambda: lax.fori_loop(0, 2 * n_ch2, mask_chunk, (jnp.zeros((1, LANE), F32), mred0))[1])
    m_row = [jnp.max(mh, axis=0, keepdims=True) for mh in mred]

    acc_ref[...] = jnp.zeros_like(acc_ref)

    def pv_pair(j2, carry):
        pv = None
        for jc in (2 * j2, 2 * j2 + 1):
            ps = [jnp.exp((lg_ref[jc, :, h * TILE:(h + 1) * TILE] - m_row[h]).astype(BF16)) for h in range(nh)]
            term = _dot(vt_ref[jc], jnp.concatenate(ps, axis=1))
            pv = term if pv is None else pv + term
        acc_ref[...] += pv
        return carry

    lax.fori_loop(0, n_ch2, pv_pair, 0)

    heads = []
    for h in range(nh):
        blk = acc_ref[:, h * TILE:(h + 1) * TILE]
        o_t = (blk / blk[HALF:HALF + 1, :]).T
        heads.append(o_t if h % 2 == 0 else pltpu.roll(o_t, HALF, 1))
    pairs = [jnp.where(lo, heads[2 * mp], heads[2 * mp + 1]) for mp in range(nh // 2)]
    o_ref[...] = (jnp.concatenate(pairs, axis=1) * _silu(gate_ref[...].astype(F32))).astype(o_ref.dtype)


def _dsa(p3, sm3, q_norm_g, k_norm_g):
    b, l, _ = p3.shape
    top_k = min(DSA_TOPK_MAX, l // 4)
    nb = l // TILE
    nc = l // KEY_CHUNK
    sl = np.tril(np.ones((KEY_CHUNK, KEY_CHUNK), np.float32), -1)
    dup = lambda g: jnp.concatenate([g, g]).reshape(1, LANE).astype(F32)
    consts = [dup(q_norm_g), dup(k_norm_g), jnp.asarray(sl, BF16)]
    qblk = lambda name: pl.BlockSpec((None, TILE, BRANCH_W), lambda i, j: (i, j, P_OFF[name] // BRANCH_W))
    seq = lambda name: pl.BlockSpec((None, l, LANE), lambda i, j: (i, 0, P_OFF[name] // LANE),
                                    pipeline_mode=pl.Buffered(1))
    chunks = lambda dt: pltpu.VMEM((nc, KEY_CHUNK, LANE), dt)
    return pl.pallas_call(
        functools.partial(_dsa_kernel, top_k=top_k),
        grid=(b, nb),
        in_specs=[qblk("c_q"), qblk("c_iq"),
                  pl.BlockSpec((None, TILE, LANE), lambda i, j: (i, j, 0)),
                  qblk("c_gate"), seq("c_kk"), seq("c_vv"), seq("c_ikik")]
        + [_full_spec(a.shape) for a in consts],
        out_specs=pl.BlockSpec((None, TILE, BRANCH_W), lambda i, j: (i, j, 0)),
        out_shape=jax.ShapeDtypeStruct((b, l, BRANCH_W), BF16),
        scratch_shapes=[chunks(BF16), pltpu.VMEM((nc, LANE, KEY_CHUNK), BF16), chunks(BF16), chunks(I32),
                        pltpu.VMEM((nc, 32, 8, LANE), I32),
                        pltpu.VMEM((nc, KEY_CHUNK, DSA_HEADS * TILE), F32), pltpu.VMEM((8, LANE), I32),
                        pltpu.VMEM((8, LANE), F32), pltpu.VMEM((8, LANE), F32),
                        pltpu.VMEM((LANE, DSA_HEADS * TILE), F32)],
        compiler_params=pltpu.CompilerParams(dimension_semantics=("arbitrary", "arbitrary"),
                                             vmem_limit_bytes=VMEM_LIMIT),
        name="dsa",
    )(p3, p3, sm3, p3, p3, p3, p3, *consts)


def _ssd_kernel(z_ref, xbc_ref, small_ref, cw_ref, cbias_ref, dtb_ref, apad_ref, dexp_ref, ng_ref, tril_ref,
                e64_ref, e128_ref, o_ref, st_ref, halo_ref, buf_ref):
    _tiles_in_order(_ssd_tile, (z_ref, xbc_ref, small_ref),
                    (cw_ref, cbias_ref, dtb_ref, apad_ref, dexp_ref, ng_ref, tril_ref, e64_ref, e128_ref),
                    o_ref, (st_ref, halo_ref, buf_ref))


def _ssd_tile(z_ref, xbc_ref, small_ref, cw_ref, cbias_ref, dtb_ref, apad_ref, dexp_ref, ng_ref, tril_ref,
              e64_ref, e128_ref, o_ref, st_ref, halo_ref, buf_ref):
    nhp = SSD_HEADS * SSD_HD
    gs = SSD_STATE
    act = _silu(_causal_conv(xbc_ref[...].astype(F32), cw_ref, halo_ref, buf_ref) + cbias_ref[...])
    xs = act[:, :nhp]
    bm = act[:, nhp:nhp + SSD_GROUPS * gs]
    cm = act[:, nhp + SSD_GROUPS * gs:]
    dt_full = _softplus(small_ref[...] + dtb_ref[...])
    cs_full = _dot01_l(tril_ref[...], dt_full * apad_ref[...])
    dt_exp = _dot01_r(dt_full, e64_ref[...])
    cs_exp = _dot01_r(cs_full, e64_ref[...])
    cs_lb = _dot01_r(cs_full, e128_ref[...])
    xdt = xs * dt_exp
    cs_last = cs_exp[TILE - 1:TILE, :]
    xds = (xdt * jnp.exp(cs_last - cs_exp)).astype(BF16)
    ecs = jnp.exp(cs_exp)
    xdt_b = xdt.astype(BF16)

    r = lax.broadcasted_iota(I32, (TILE, TILE), 0)
    c = lax.broadcasted_iota(I32, (TILE, TILE), 1)
    incl = r >= c
    lo = c < HALF
    hpg = SSD_HEADS // SSD_GROUPS
    gw = hpg * SSD_HD
    y_parts = []
    for g in range(SSD_GROUPS):
        bmg = bm[:, g * gs:(g + 1) * gs]
        cmg = cm[:, g * gs:(g + 1) * gs].astype(BF16)
        cbg = _dot_nt(cmg, bmg.astype(BF16))
        for mp in range(hpg // 2):
            pair = (g * hpg) // 2 + mp
            xpair = xdt_b[:, pair * LANE:(pair + 1) * LANE]
            ys = []
            for e in range(2):
                h = 2 * pair + e
                cb = cs_lb[:, h * LANE:(h + 1) * LANE]
                lm = jnp.where(incl, jnp.exp(jnp.where(incl, cb - cb.T, 0.0)), 0.0)
                ys.append(_dot((cbg * lm).astype(BF16), xpair))
            y_parts.append(jnp.where(lo, ys[0], ys[1]))
        st = st_ref[g]
        y_off = _dot(cmg, st.astype(BF16)) * ecs[:, g * gw:(g + 1) * gw]
        y_parts[-(hpg // 2):] = [yp + y_off[:, k * LANE:(k + 1) * LANE]
                                 for k, yp in enumerate(y_parts[-(hpg // 2):])]
        new = _dot(bmg.T.astype(BF16), xds[:, g * gw:(g + 1) * gw])
        st_ref[g] = jnp.exp(cs_last[:, g * gw:(g + 1) * gw]) * st + new
    y = jnp.concatenate(y_parts, axis=1) + xs * dexp_ref[...]
    yz = y * _silu(z_ref[...].astype(F32))
    o_ref[...] = _rms(yz, ng_ref[...]).astype(o_ref.dtype)


def _ssd(p3, sm3, conv_w, conv_b, a_log, dt_bias, d_skip, norm_g):
    b, l, _ = p3.shape
    step = min(SEQ_STEP, l)
    consts = [conv_w.astype(F32), conv_b.reshape(1, -1).astype(F32), _pad_lanes(dt_bias, SM_DT),
              _pad_lanes(-jnp.exp(a_log.astype(F32)), SM_DT),
              jnp.repeat(d_skip.astype(F32), SSD_HD).reshape(1, -1), norm_g.reshape(1, -1).astype(F32),
              jnp.asarray(np.tril(np.ones((TILE, TILE), np.float32)), BF16),
              jnp.asarray(_np_lane_expand(SM_DT, SSD_HEADS, SSD_HD), BF16),
              jnp.asarray(_np_lane_expand(SM_DT, SSD_HEADS, LANE), BF16)]
    return pl.pallas_call(
        _ssd_kernel,
        grid=(b, l // step),
        in_specs=[pl.BlockSpec((None, step, BRANCH_W), lambda i, j: (i, j, P_OFF["d_z"] // BRANCH_W)),
                  pl.BlockSpec((None, step, SSD_XBC), lambda i, j: (i, j, P_OFF["d_xbc"] // SSD_XBC)),
                  pl.BlockSpec((None, step, LANE), lambda i, j: (i, j, 0))]
        + [_full_spec(a.shape) for a in consts],
        out_specs=pl.BlockSpec((None, step, BRANCH_W), lambda i, j: (i, j, 0)),
        out_shape=jax.ShapeDtypeStruct((b, l, BRANCH_W), BF16),
        scratch_shapes=[pltpu.VMEM((SSD_GROUPS, SSD_STATE, (SSD_HEADS // SSD_GROUPS) * SSD_HD), F32),
                        pltpu.VMEM((8, SSD_XBC), F32), pltpu.VMEM((TILE + 8, SSD_XBC), F32)],
        compiler_params=pltpu.CompilerParams(dimension_semantics=("arbitrary", "arbitrary"),
                                             vmem_limit_bytes=VMEM_LIMIT),
        name="ssd",
    )(p3, p3, sm3, *consts)


def _mem_kv_kernel(mem_ref, g_ref, w_ref, kg_ref, k_ref, v_ref):
    kv = _dot(_rms(mem_ref[...], g_ref[...]).astype(BF16), w_ref[...])
    for h in range(MEM_HEADS):
        hs = slice(h * MEM_HD, (h + 1) * MEM_HD)
        k_ref[:, hs] = _rms(kv[:, hs], kg_ref[...]).astype(k_ref.dtype)
    v_ref[...] = kv[:, BRANCH_W:].astype(v_ref.dtype)


def _mem_kv(mem, mem_norm_g, w_kv_bf16, k_norm_g):
    b, m, _ = mem.shape
    consts = [mem_norm_g.reshape(1, -1).astype(F32), w_kv_bf16, k_norm_g.reshape(1, -1).astype(F32)]
    out = jax.ShapeDtypeStruct((b, m, BRANCH_W), BF16)
    return pl.pallas_call(
        _mem_kv_kernel,
        grid=(b,),
        in_specs=[pl.BlockSpec((None, m, D_MODEL), lambda i: (i, 0, 0))] + [_full_spec(a.shape) for a in consts],
        out_specs=[pl.BlockSpec((None, m, BRANCH_W), lambda i: (i, 0, 0))] * 2,
        out_shape=[out, out],
        compiler_params=pltpu.CompilerParams(dimension_semantics=("arbitrary",), vmem_limit_bytes=VMEM_LIMIT),
        name="mem_kv",
    )(mem, *consts)


def _mem_attn_kernel(q_ref, gate_ref, k_ref, v_ref, qg_ref, o_ref):
    q = q_ref[...].astype(F32)
    outs = []
    for h in range(MEM_HEADS):
        hs = slice(h * MEM_HD, (h + 1) * MEM_HD)
        qn = _rms(q[:, hs], qg_ref[...]).astype(BF16)
        lg = _dot_nt(qn, k_ref[:, hs]) * (MEM_HD ** -0.5)
        p = jnp.exp(lg - jnp.max(lg, axis=1, keepdims=True))
        outs.append(_dot(p.astype(BF16), v_ref[:, hs]) / jnp.sum(p, axis=1, keepdims=True))
    o_ref[...] = (jnp.concatenate(outs, axis=1) * _silu(gate_ref[...].astype(F32))).astype(o_ref.dtype)


def _mem_attn(p3, k, v, q_norm_g):
    b, l, _ = p3.shape
    m = k.shape[1]
    tm = min(512, l)
    qg = q_norm_g.reshape(1, -1).astype(F32)
    blk = lambda name: pl.BlockSpec((None, tm, BRANCH_W), lambda i, j: (i, j, P_OFF[name] // BRANCH_W))
    kv = pl.BlockSpec((None, m, BRANCH_W), lambda i, j: (i, 0, 0))
    return pl.pallas_call(
        _mem_attn_kernel,
        grid=(b, l // tm),
        in_specs=[blk("m_q"), blk("m_gate"), kv, kv, _full_spec(qg.shape)],
        out_specs=pl.BlockSpec((None, tm, BRANCH_W), lambda i, j: (i, j, 0)),
        out_shape=jax.ShapeDtypeStruct((b, l, BRANCH_W), BF16),
        compiler_params=pltpu.CompilerParams(dimension_semantics=("arbitrary", "arbitrary"),
                                             vmem_limit_bytes=VMEM_LIMIT),
        name="mem_attn",
    )(p3, p3, k, v, qg)


def _reordered_w_in(w):
    parts, start = [], 0
    cols = _P_COLUMNS
    for end in range(1, len(cols) + 1):
        if end == len(cols) or (cols[end] < 0) != (cols[start] < 0) or (cols[start] >= 0 and cols[end] != cols[end - 1] + 1):
            if cols[start] < 0:
                parts.append(jnp.zeros((w.shape[0], end - start), BF16))
            else:
                parts.append(w[:, int(cols[start]):int(cols[end - 1]) + 1].astype(BF16))
            start = end
    return jnp.concatenate(parts, axis=1)


def kernel(x, mem, norm_g, w_in, gdn_conv_w, gdn_a_log, gdn_dt_bias, gdn_norm_g, sg_ln_g, sg_ln_b, sg_w, sg_b, dsa_q_norm_g, dsa_k_norm_g, ssd_conv_w, ssd_conv_b, ssd_a_log, ssd_dt_bias, ssd_d, ssd_norm_g, mem_norm_g, w_mem_kv, mem_q_norm_g, mem_k_norm_g, w_gate, w_branch, w_out):
    b, l, d = x.shape
    depth = norm_g.shape[0]
    x2d = x.reshape(b * l, d)
    for i in range(depth):
        g = norm_g[i].reshape(1, d).astype(F32)
        p2d, sm2d = _in_proj(x2d, g, _reordered_w_in(w_in[i]))
        p3 = p2d.reshape(b, l, P_MAIN)
        sm3 = sm2d.reshape(b, l, LANE)
        mk, mv = _mem_kv(mem, mem_norm_g[i], w_mem_kv[i].astype(BF16), mem_k_norm_g[i])
        ys = (
            _gdn(p3, sm3, gdn_conv_w[i], gdn_a_log[i], gdn_dt_bias[i], gdn_norm_g[i]),
            _sg(p3, sg_ln_g[i], sg_ln_b[i], sg_w[i], sg_b[i]),
            _dsa(p3, sm3, dsa_q_norm_g[i], dsa_k_norm_g[i]),
            _ssd(p3, sm3, ssd_conv_w[i], ssd_conv_b[i], ssd_a_log[i], ssd_dt_bias[i], ssd_d[i], ssd_norm_g[i]),
            _mem_attn(p3, mk, mv, mem_q_norm_g[i]),
        )
        ys = [y.reshape(b * l, BRANCH_W) for y in ys]
        x2d = _merge(x2d, g, ys, w_gate[i].astype(BF16), w_branch[i].astype(BF16), w_out[i].astype(BF16))
    return x2d.reshape(b, l, d)
```

```python
import functools

import numpy as np
import jax
import jax.numpy as jnp
from jax import lax
from jax.experimental import pallas as pl
from jax.experimental.pallas import tpu as pltpu

F32 = jnp.float32
BF16 = jnp.bfloat16
I32 = jnp.int32

D_MODEL = 1024
N_BRANCH = 5
BRANCH_W = D_MODEL // 2
CONV_K = 4
EPS = 1e-6
GDN_HEADS = 4
GDN_DK = BRANCH_W // GDN_HEADS
GDN_CHUNK = 64
GDN_QKV = 3 * BRANCH_W
SG_CHUNK = 128
SG_GROUPS = 4
DSA_HEADS = 8
DSA_HD = BRANCH_W // DSA_HEADS
IDX_HEADS = 8
IDX_HD = 64
DSA_TOPK_MAX = 256
Q_BLOCK = 128
SSD_HEADS = 8
SSD_HD = BRANCH_W // SSD_HEADS
SSD_GROUPS = 2
SSD_STATE = 128
SSD_CHUNK = 128
SSD_XBC = SSD_HEADS * SSD_HD + 2 * SSD_GROUPS * SSD_STATE
MEM_HEADS = 4
MEM_HD = BRANCH_W // MEM_HEADS

IN_SPLITS = (
    GDN_QKV, GDN_HEADS, GDN_HEADS, BRANCH_W,
    BRANCH_W, BRANCH_W, BRANCH_W,
    DSA_HEADS * DSA_HD, DSA_HD, DSA_HD, IDX_HEADS * IDX_HD, IDX_HD, IDX_HEADS, BRANCH_W,
    BRANCH_W, SSD_XBC, SSD_HEADS,
    MEM_HEADS * MEM_HD, BRANCH_W,
)
_SEG_NAMES = ("a_qkv", "a_a", "a_b", "a_gate", "b_u", "b_v", "b_gate", "c_q", "c_k", "c_v", "c_iq", "c_ik",
              "c_iw", "c_gate", "d_z", "d_xbc", "d_dt", "m_q", "m_gate")
_SEG_START = dict(zip(_SEG_NAMES, np.concatenate([[0], np.cumsum(IN_SPLITS)[:-1]]).tolist()))
_SEG_WIDTH = dict(zip(_SEG_NAMES, IN_SPLITS))

LANE = 128
HALF = LANE // 2
TILE = 128
SEQ_STEP = 4 * TILE
VMEM_LIMIT = 56 * 1024 * 1024

_P_LAYOUT = (("a_qkv", 1536), ("b_u", 512), ("d_xbc", 1024), ("a_gate", 512), ("b_v", 512), ("b_gate", 512),
             ("c_q", 512), ("c_iq", 512), ("c_gate", 512), ("d_z", 512), ("m_q", 512), ("m_gate", 512),
             ("c_kk", 128), ("c_vv", 128), ("c_ikik", 128), ("small", 128))
P_OFF = {}
_o = 0
for _n, _w in _P_LAYOUT:
    P_OFF[_n] = _o
    _o += _w
P_COLS = _o
P_MAIN = P_OFF["small"]
SM_A, SM_B, SM_DT, SM_IW = 0, 4, 8, 16


def _p_columns():
    cols = []
    for name, width in _P_LAYOUT:
        if name in _SEG_START:
            cols += list(range(_SEG_START[name], _SEG_START[name] + width))
        elif name in ("c_kk", "c_vv", "c_ikik"):
            src = {"c_kk": "c_k", "c_vv": "c_v", "c_ikik": "c_ik"}[name]
            one = list(range(_SEG_START[src], _SEG_START[src] + _SEG_WIDTH[src]))
            cols += one + one
        else:
            sm = [-1] * LANE
            for seg, at in (("a_a", SM_A), ("a_b", SM_B), ("d_dt", SM_DT), ("c_iw", SM_IW)):
                for j in range(_SEG_WIDTH[seg]):
                    sm[at + j] = _SEG_START[seg] + j
            cols += sm
    return np.asarray(cols, np.int32)


_P_COLUMNS = _p_columns()


def _dot(a, b, precision=None):
    return jnp.dot(a, b, preferred_element_type=F32, precision=precision)


def _dot_nt(a, b):
    return lax.dot_general(a, b, (((1,), (1,)), ((), ())), preferred_element_type=F32)


def _split3(x):
    hi = x.astype(BF16)
    r1 = x - hi.astype(F32)
    mid = r1.astype(BF16)
    lo = (r1 - mid.astype(F32)).astype(BF16)
    return hi, mid, lo


def _dot01_r(x, e):
    hi, mid, lo = _split3(x)
    return _dot(hi, e) + _dot(mid, e) + _dot(lo, e)


def _dot01_l(e, x):
    hi, mid, lo = _split3(x)
    return _dot(e, hi) + _dot(e, mid) + _dot(e, lo)


def _sigmoid(x):
    return 1.0 / (1.0 + jnp.exp(-x))


def _silu(x):
    return x * _sigmoid(x)


def _softplus(x):
    return jnp.maximum(x, 0.0) + jnp.log1p(jnp.exp(-jnp.abs(x)))


def _gelu_tanh(x):
    return 0.5 * x * (1.0 + jnp.tanh(np.sqrt(2.0 / np.pi).astype(np.float32) * (x + 0.044715 * (x * x * x))))


def _rms(x, g, eps=EPS):
    return x * lax.rsqrt(jnp.mean(x * x, axis=-1, keepdims=True) + eps) * g


def _full_spec(shape):
    nd = len(shape)
    return pl.BlockSpec(shape, lambda *_: (0,) * nd)


def _in_proj_kernel(x_ref, g_ref, w_ref, o_ref, s_ref):
    h = _rms(x_ref[...], g_ref[...]).astype(BF16)
    step = 1024
    for c0 in range(0, P_MAIN, step):
        c1 = min(c0 + step, P_MAIN)
        o_ref[:, c0:c1] = _dot(h, w_ref[:, c0:c1]).astype(o_ref.dtype)
    s_ref[...] = _dot(h, w_ref[:, P_MAIN:])


def _in_proj(x2d, g, w_bf16):
    m = x2d.shape[0]
    tm = min(512, m)
    return pl.pallas_call(
        _in_proj_kernel,
        grid=(m // tm,),
        in_specs=[pl.BlockSpec((tm, D_MODEL), lambda i: (i, 0)),
                  pl.BlockSpec((1, D_MODEL), lambda i: (0, 0)),
                  pl.BlockSpec((D_MODEL, P_COLS), lambda i: (0, 0), pipeline_mode=pl.Buffered(1))],
        out_specs=[pl.BlockSpec((tm, P_MAIN), lambda i: (i, 0)), pl.BlockSpec((tm, LANE), lambda i: (i, 0))],
        out_shape=[jax.ShapeDtypeStruct((m, P_MAIN), BF16), jax.ShapeDtypeStruct((m, LANE), F32)],
        compiler_params=pltpu.CompilerParams(dimension_semantics=("arbitrary",), vmem_limit_bytes=VMEM_LIMIT),
        name="in_proj",
    )(x2d, g, w_bf16)


def _merge_kernel(x_ref, g_ref, y0, y1, y2, y3, y4, wg_ref, wb_ref, wo_ref, o_ref):
    x = x_ref[...]
    h = _rms(x, g_ref[...]).astype(BF16)
    acc = None
    for p, y in enumerate((y0, y1, y2, y3, y4)):
        term = _sigmoid(_dot(h, wg_ref[p])) * _dot(y[...], wb_ref[p])
        acc = term if acc is None else acc + term
    o_ref[...] = x + _dot(acc.astype(BF16), wo_ref[...])


def _merge(x2d, g, ys, wg, wb, wo):
    m = x2d.shape[0]
    tm = min(512, m)
    row = lambda i: (i, 0)
    const = pl.Buffered(1)
    return pl.pallas_call(
        _merge_kernel,
        grid=(m // tm,),
        in_specs=[pl.BlockSpec((tm, D_MODEL), row), pl.BlockSpec((1, D_MODEL), lambda i: (0, 0))]
        + [pl.BlockSpec((tm, BRANCH_W), row)] * N_BRANCH
        + [pl.BlockSpec((N_BRANCH, D_MODEL, D_MODEL), lambda i: (0, 0, 0), pipeline_mode=const),
           pl.BlockSpec((N_BRANCH, BRANCH_W, D_MODEL), lambda i: (0, 0, 0), pipeline_mode=const),
           pl.BlockSpec((D_MODEL, D_MODEL), lambda i: (0, 0), pipeline_mode=const)],
        out_specs=pl.BlockSpec((tm, D_MODEL), row),
        out_shape=jax.ShapeDtypeStruct((m, D_MODEL), F32),
        compiler_params=pltpu.CompilerParams(dimension_semantics=("arbitrary",), vmem_limit_bytes=VMEM_LIMIT),
        name="merge_out",
    )(x2d, g, *ys, wg, wb, wo)


def _causal_conv(x, w_ref, halo_ref, buf_ref):
    buf_ref[0:8, :] = halo_ref[...]
    buf_ref[8:8 + TILE, :] = x
    y = w_ref[CONV_K - 1:CONV_K, :] * x
    for j in range(CONV_K - 1):
        y = y + w_ref[j:j + 1, :] * buf_ref[8 - (CONV_K - 1) + j:8 - (CONV_K - 1) + j + TILE, :]
    halo_ref[...] = x[TILE - 8:TILE, :]
    return y


def _np_lane_expand(first_row, n_heads, width):
    e = np.zeros((LANE, n_heads * width), np.float32)
    for h in range(n_heads):
        e[first_row + h, h * width:(h + 1) * width] = 1.0
    return e


def _pad_lanes(vec, at):
    out = jnp.zeros((1, LANE), F32)
    return lax.dynamic_update_slice(out, vec.reshape(1, -1).astype(F32), (0, at))


def _tiles_in_order(tile_fn, seq_refs, const_refs, o_ref, scratch_refs):
    @pl.when(pl.program_id(1) == 0)
    def _():
        for ref in scratch_refs[:2]:
            ref[...] = jnp.zeros_like(ref)

    def body(t, carry):
        rows = pl.ds(pl.multiple_of(t * TILE, TILE), TILE)
        tile_fn(*[r.at[rows] for r in seq_refs], *const_refs, o_ref.at[rows], *scratch_refs)
        return carry

    lax.fori_loop(0, o_ref.shape[0] // TILE, body, 0)


def _gdn_kernel(qkv_ref, small_ref, gate_ref, cw_ref, apad_ref, dtb_ref, ng_ref, btril_ref, eg_ref, eb_ref,
                o_ref, st_ref, halo_ref, buf_ref):
    _tiles_in_order(_gdn_tile, (qkv_ref, small_ref, gate_ref),
                    (cw_ref, apad_ref, dtb_ref, ng_ref, btril_ref, eg_ref, eb_ref), o_ref, (st_ref, halo_ref, buf_ref))


def _gdn_tile(qkv_ref, small_ref, gate_ref, cw_ref, apad_ref, dtb_ref, ng_ref, btril_ref, eg_ref, eb_ref,
              o_ref, st_ref, halo_ref, buf_ref):
    act = _silu(_causal_conv(qkv_ref[...].astype(F32), cw_ref, halo_ref, buf_ref))
    small = small_ref[...]
    g_full = -apad_ref[...] * _softplus(small + dtb_ref[...])
    beta_full = _sigmoid(small)
    gc_full = _dot01_l(btril_ref[...], g_full)
    g_lb = _dot01_r(gc_full, eg_ref[...])
    b_lb = _dot01_r(beta_full, eb_ref[...])

    r = lax.broadcasted_iota(I32, (TILE, TILE), 0)
    c = lax.broadcasted_iota(I32, (TILE, TILE), 1)
    same = (r >> 6) == (c >> 6)
    incl = same & (r >= c)
    strict = same & (r > c)
    eye = (r == c).astype(F32)
    n_sub = TILE // GDN_CHUNK
    ng = ng_ref[...]
    heads = range(GDN_HEADS)
    q, k, v, cb, bb = [], [], [], [], []
    for h in heads:
        qh = act[:, h * GDN_DK:(h + 1) * GDN_DK]
        kh = act[:, BRANCH_W + h * GDN_DK:BRANCH_W + (h + 1) * GDN_DK]
        q.append(qh * lax.rsqrt(jnp.sum(qh * qh, axis=-1, keepdims=True) + EPS) * (GDN_DK ** -0.5))
        k.append(kh * lax.rsqrt(jnp.sum(kh * kh, axis=-1, keepdims=True) + EPS))
        v.append(act[:, 2 * BRANCH_W + h * GDN_DK:2 * BRANCH_W + (h + 1) * GDN_DK])
        cb.append(g_lb[:, h * LANE:(h + 1) * LANE])
        bb.append(b_lb[:, h * LANE:(h + 1) * LANE])
    dm = [jnp.where(incl, jnp.exp(jnp.where(incl, cb[h] - cb[h].T, 0.0)), 0.0) for h in heads]
    kb = [k[h].astype(BF16) for h in heads]
    kk = [_dot_nt(kb[h], kb[h]) for h in heads]
    qkb = [(_dot_nt(q[h].astype(BF16), kb[h]) * dm[h]).astype(BF16) for h in heads]
    mp = [jnp.where(strict, -(bb[h] * kk[h] * dm[h]), 0.0) for h in heads]
    inv = [eye + mp[h] for h in heads]
    for _ in range(int(np.log2(GDN_CHUNK)) - 1):
        mpb = [mp[h].astype(BF16) for h in heads]
        mp = [_dot(mpb[h], mpb[h]) for h in heads]
        inv = [inv[h] + _dot(inv[h].astype(BF16), mp[h].astype(BF16)) for h in heads]
    eg = [jnp.exp(cb[h]) for h in heads]
    sol = [_dot(inv[h].astype(BF16),
                jnp.concatenate([bb[h] * eg[h] * k[h], bb[h] * v[h]], axis=1).astype(BF16)) for h in heads]
    wk = [sol[h][:, :GDN_DK].astype(BF16) for h in heads]
    u0 = [sol[h][:, GDN_DK:] for h in heads]
    qd = [(q[h] * eg[h]).astype(BF16) for h in heads]
    zero = jnp.zeros((GDN_CHUNK, LANE), F32)
    st = [st_ref[h] for h in heads]
    o_parts = [[] for _ in heads]
    for cc in range(n_sub):
        rows = slice(cc * GDN_CHUNK, (cc + 1) * GDN_CHUNK)
        in_chunk = (r >> 6) == cc
        glast = [cb[h][(cc + 1) * GDN_CHUNK - 1:(cc + 1) * GDN_CHUNK, :] for h in heads]
        stb = [st[h].astype(BF16) for h in heads]
        u = [u0[h][rows] - _dot(wk[h][rows], stb[h]) for h in heads]
        upad = [jnp.concatenate([u[h] if j == cc else zero for j in range(n_sub)], axis=0).astype(BF16)
                for h in heads]
        for h in heads:
            o_parts[h].append(_dot(qd[h][rows], stb[h]) + _dot(qkb[h][rows], upad[h]))
        kd = [jnp.where(in_chunk, k[h] * jnp.exp(jnp.where(in_chunk, glast[h] - cb[h], 0.0)), 0.0) for h in heads]
        st = [jnp.exp(glast[h]) * st[h] + _dot(kd[h].T.astype(BF16), upad[h]) for h in heads]
    for h in heads:
        st_ref[h] = st[h]
    outs = [_rms(jnp.concatenate(o_parts[h], axis=0), ng) for h in heads]
    y = jnp.concatenate(outs, axis=1) * _silu(gate_ref[...].astype(F32))
    o_ref[...] = y.astype(o_ref.dtype)


def _gdn(p3, sm3, conv_w, a_log, dt_bias, norm_g):
    b, l, _ = p3.shape
    step = min(SEQ_STEP, l)
    btril = np.zeros((TILE, TILE), np.float32)
    for i in range(TILE):
        lo = (i // GDN_CHUNK) * GDN_CHUNK
        btril[i, lo:i + 1] = 1.0
    consts = [conv_w.astype(F32), _pad_lanes(jnp.exp(a_log.astype(F32)), SM_A), _pad_lanes(dt_bias, SM_A),
              norm_g.reshape(1, -1).astype(F32), jnp.asarray(btril, BF16),
              jnp.asarray(_np_lane_expand(SM_A, GDN_HEADS, LANE), BF16),
              jnp.asarray(_np_lane_expand(SM_B, GDN_HEADS, LANE), BF16)]
    return pl.pallas_call(
        _gdn_kernel,
        grid=(b, l // step),
        in_specs=[pl.BlockSpec((None, step, GDN_QKV), lambda i, j: (i, j, P_OFF["a_qkv"] // GDN_QKV)),
                  pl.BlockSpec((None, step, LANE), lambda i, j: (i, j, 0)),
                  pl.BlockSpec((None, step, BRANCH_W), lambda i, j: (i, j, P_OFF["a_gate"] // BRANCH_W))]
        + [_full_spec(a.shape) for a in consts],
        out_specs=pl.BlockSpec((None, step, BRANCH_W), lambda i, j: (i, j, 0)),
        out_shape=jax.ShapeDtypeStruct((b, l, BRANCH_W), BF16),
        scratch_shapes=[pltpu.VMEM((GDN_HEADS, GDN_DK, GDN_DK), F32), pltpu.VMEM((8, GDN_QKV), F32),
                        pltpu.VMEM((TILE + 8, GDN_QKV), F32)],
        compiler_params=pltpu.CompilerParams(dimension_semantics=("arbitrary", "arbitrary"),
                                             vmem_limit_bytes=VMEM_LIMIT),
        name="gdn",
    )(p3, sm3, p3, *consts)


SG_TILE = 4 * SG_CHUNK


def _sg_kernel(u_ref, v_ref, gate_ref, lng_ref, lnb_ref, w_ref, bs_ref, o_ref):
    r = lax.broadcasted_iota(I32, (SG_CHUNK, SG_CHUNK), 0)
    c = lax.broadcasted_iota(I32, (SG_CHUNK, SG_CHUNK), 1)
    gw = BRANCH_W // SG_GROUPS
    wc = [jnp.where(r >= c, w_ref[g], 0.0).astype(BF16) for g in range(SG_GROUPS)]
    for t0 in range(0, u_ref.shape[0], SG_CHUNK):
        rows = slice(t0, t0 + SG_CHUNK)
        u = _gelu_tanh(u_ref[rows, :].astype(F32))
        v = _gelu_tanh(v_ref[rows, :].astype(F32))
        mu = jnp.mean(v, axis=-1, keepdims=True)
        var = jnp.mean(jnp.square(v - mu), axis=-1, keepdims=True)
        vn = ((v - mu) * lax.rsqrt(var + 1e-5) * lng_ref[...] + lnb_ref[...]).astype(BF16)
        mixed = jnp.concatenate([_dot(wc[g], vn[:, g * gw:(g + 1) * gw]) for g in range(SG_GROUPS)], axis=1)
        o_ref[rows, :] = (u * (mixed + bs_ref[...]) * _silu(gate_ref[rows, :].astype(F32))).astype(o_ref.dtype)


def _sg(p3, ln_g, ln_b, w_s, b_s):
    b, l, _ = p3.shape
    gw = BRANCH_W // SG_GROUPS
    tile = min(SG_TILE, l)
    bs_full = jnp.repeat(jnp.swapaxes(b_s, 0, 1).astype(F32), gw, axis=1)
    consts = [ln_g.reshape(1, -1).astype(F32), ln_b.reshape(1, -1).astype(F32), w_s.astype(F32), bs_full]
    blk = lambda name: pl.BlockSpec((None, tile, BRANCH_W), lambda i, j: (i, j, P_OFF[name] // BRANCH_W))
    return pl.pallas_call(
        _sg_kernel,
        grid=(b, l // tile),
        in_specs=[blk("b_u"), blk("b_v"), blk("b_gate")] + [_full_spec(a.shape) for a in consts],
        out_specs=pl.BlockSpec((None, tile, BRANCH_W), lambda i, j: (i, j, 0)),
        out_shape=jax.ShapeDtypeStruct((b, l, BRANCH_W), BF16),
        compiler_params=pltpu.CompilerParams(dimension_semantics=("arbitrary", "arbitrary"),
                                             vmem_limit_bytes=VMEM_LIMIT),
        name="spatial_gating",
    )(p3, p3, p3, *consts)


_NEG = -1e30
_INT_MIN = -2 ** 31


def _fold8(x, op):
    parts = [x[r:r + 8] for r in range(0, x.shape[0], 8)]
    while len(parts) > 1:
        parts = [op(parts[k], parts[k + 1]) for k in range(0, len(parts) - 1, 2)] + (
            [parts[-1]] if len(parts) % 2 else [])
    return parts[0]


KEY_CHUNK = 2 * TILE


def _bit_planes(u):
    a = [u[8 * g:8 * g + 8] for g in range(32)]
    j, m = 16, 0x0000FFFF
    while j:
        k = 0
        while k < 32:
            t = (a[k] ^ (a[k + j] >> j)) & m
            a[k] = a[k] ^ t
            a[k + j] = a[k + j] ^ (t << j)
            k = (k + j + 1) & ~j
        j >>= 1
        m = m ^ ((m << j) & 0xFFFFFFFF)
    return [a[31 - b] for b in range(32)]


def _dsa_kernel(q_ref, iq_ref, small_ref, gate_ref, kk_ref, vv_ref, ik_ref, qg_ref, kg_ref, sl_ref,
                o_ref, kn_ref, vt_ref, ikb_ref, key_ref, plane_ref, lg_ref, tau_ref, need_ref, nequal_ref, acc_ref,
                *, top_k):
    i = pl.program_id(1)
    nh = DSA_HEADS
    nc = key_ref.shape[0]
    kc = KEY_CHUNK

    srow = lax.broadcasted_iota(I32, (kc, LANE), 0)
    lane = lax.broadcasted_iota(I32, (kc, LANE), 1)
    qpos = i * TILE + lane
    pos_lanes = (lane == HALF) | (lane == HALF + 1)

    @pl.when(i == 0)
    def _():
        def prep(jc, carry):
            rows = pl.ds(pl.multiple_of(jc * kc, kc), kc)
            kpos = jnp.where(lane == HALF, srow, jc * kc).astype(F32)
            kn = jnp.where(lane < HALF, _rms(kk_ref[rows, :].astype(F32), kg_ref[...]), jnp.where(pos_lanes, kpos, 0.0))
            kn_ref[jc] = kn.astype(BF16)
            v = jnp.where(lane < HALF, vv_ref[rows, :].astype(F32), 1.0)
            vt_ref[jc] = jnp.concatenate([v[r:r + TILE].T for r in range(0, kc, TILE)], axis=1).astype(BF16)
            ikb_ref[jc] = ik_ref[rows, :]
            plane_ref[jc] = jnp.zeros(plane_ref.shape[1:], I32)
            return carry
        lax.fori_loop(0, nc, prep, 0)

    n_ch = (i + 2) // 2
    n_ch2 = (n_ch + 1) // 2

    iq = iq_ref[...].astype(F32)
    lane_q = lax.broadcasted_iota(I32, (TILE, LANE), 1)
    lo = lane_q < HALF
    pos_lanes_q = (lane_q == HALF) | (lane_q == HALF + 1)
    iq_st = jnp.concatenate(
        [jnp.where(lo if h % 2 == 0 else ~lo, iq[:, (h // 2) * LANE:(h // 2 + 1) * LANE], 0.0) for h in range(nh)],
        axis=0).astype(BF16)
    small_t = small_ref[...].T * (IDX_HEADS ** -0.5 * IDX_HD ** -0.5)
    iw_row = jnp.concatenate([small_t[SM_IW + h:SM_IW + h + 1, :] for h in range(nh)], axis=1)

    q = q_ref[...].astype(F32)
    q_parts = []
    for mpair in range(nh // 2):
        qp = q[:, mpair * LANE:(mpair + 1) * LANE]
        sq = qp * qp
        s_lo = jnp.sum(jnp.where(lo, sq, 0.0), axis=1, keepdims=True)
        s_hi = jnp.sum(jnp.where(lo, 0.0, sq), axis=1, keepdims=True)
        qn = qp * lax.rsqrt(jnp.where(lo, s_lo, s_hi) * (1.0 / DSA_HD) + EPS) * qg_ref[...] * (DSA_HD ** -0.5)
        for e, qh in enumerate((qn, pltpu.roll(qn, HALF, 1))):
            slope = 2.0 ** (-8.0 * (2 * mpair + e + 1) / nh)
            q_parts.append(jnp.where(lo, qh, jnp.where(pos_lanes_q, slope, 0.0)))
    q_st = jnp.concatenate(q_parts, axis=0).astype(BF16)

    def score_pair(j2, carry):
        jcs = (2 * j2, 2 * j2 + 1)
        raw = [_dot_nt(ikb_ref[jc], iq_st) for jc in jcs]
        for jc in jcs:
            lg_ref[jc] = _dot_nt(kn_ref[jc], q_st)
        for jc, s_raw in zip(jcs, raw):
            s_h = jnp.maximum(s_raw, 0.0) * iw_row
            s = s_h[:, 0:TILE]
            for h in range(1, nh):
                s = s + s_h[:, h * TILE:(h + 1) * TILE]
            bits = pltpu.bitcast(s, I32)
            key = jnp.where(bits < 0, bits ^ 0x7FFFFFFF, bits)
            key = jnp.where(jc * kc + srow <= qpos, key, _INT_MIN)
            key_ref[jc] = key
            planes = _bit_planes(key ^ _INT_MIN)
            for b in range(32):
                plane_ref[jc, b] = planes[b]
        return carry

    lax.fori_loop(0, n_ch2, score_pair, 0)

    tau_ref[...] = jnp.full(tau_ref.shape, _INT_MIN + 1, I32)
    need_ref[...] = jnp.zeros_like(need_ref)
    nequal_ref[...] = jnp.ones_like(nequal_ref)

    @pl.when((i + 1) * TILE > top_k)
    def _():
        def ones_at(live, b):
            acc = jnp.zeros((8, LANE), I32)
            for jc in range(nc):
                acc = acc + lax.population_count(live[jc] & plane_ref[jc, b])
            return jnp.sum(acc.astype(F32), axis=0, keepdims=True)

        live0 = tuple(jnp.where(jc < 2 * n_ch2, jnp.full((8, LANE), -1, I32), 0) for jc in range(nc))

        def bit_body(bi, carry):
            live, cnt, left, tau_u = carry
            b = 31 - bi
            take = cnt >= left
            left = jnp.where(take, left, left - cnt)
            tau_u = jnp.where(take, tau_u | lax.shift_left(jnp.int32(1), b), tau_u)
            b_next = jnp.maximum(b - 1, 0)
            acc = jnp.zeros((8, LANE), I32)
            new_live = []
            for jc in range(nc):
                with_bit = live[jc] & plane_ref[jc, b]
                keep = jnp.where(take, with_bit, live[jc] ^ with_bit)
                new_live.append(keep)
                acc = acc + lax.population_count(keep & plane_ref[jc, b_next])
            return tuple(new_live), jnp.sum(acc.astype(F32), axis=0, keepdims=True), left, tau_u

        init = (live0, ones_at(live0, 31), jnp.full((1, LANE), float(top_k), F32), jnp.zeros((1, LANE), I32))
        live, _, left, tau_u = lax.fori_loop(0, 32, bit_body, init)
        n_equal = jnp.zeros((8, LANE), I32)
        for jc in range(nc):
            n_equal = n_equal + lax.population_count(live[jc])
        n_equal = jnp.sum(n_equal.astype(F32), axis=0, keepdims=True)
        takes_all = qpos[0:1, :] + 1 <= top_k
        tau_ref[...] = jnp.broadcast_to(jnp.where(takes_all, _INT_MIN + 1, tau_u ^ _INT_MIN), tau_ref.shape)
        need_ref[...] = jnp.broadcast_to(jnp.where(takes_all, 0.0, left), need_ref.shape)
        nequal_ref[...] = jnp.broadcast_to(jnp.where(takes_all, 1.0, n_equal), nequal_ref.shape)

    tau = tau_ref[0:1, :]
    need = need_ref[0:1, :]

    def mask_chunk_no_ties(jc, mred):
        sel = key_ref[jc] >= tau
        new = []
        for h in range(nh):
            x = jnp.where(sel, lg_ref[jc, :, h * TILE:(h + 1) * TILE], _NEG)
            new.append(jnp.maximum(mred[h], _fold8(x, jnp.maximum)))
        key_ref[jc] = jnp.where(sel, 1, 0)
        return tuple(new)

    def mask_chunk(jc, carry):
        run, mred = carry
        key = key_ref[jc]
        eq = key == tau
        eqf = jnp.where(eq, 1.0, 0.0)
        prefix = _dot(sl_ref[...], eqf.astype(BF16)) + run
        sel = (key > tau) | (eq & (prefix < need))
        new = []
        for h in range(nh):
            x = jnp.where(sel, lg_ref[jc, :, h * TILE:(h + 1) * TILE], _NEG)
            new.append(jnp.maximum(mred[h], _fold8(x, jnp.maximum)))
        key_ref[jc] = jnp.where(sel, 1, 0)
        return run + jnp.sum(_fold8(eqf, jnp.add), axis=0, keepdims=True), tuple(new)

    mred0 = tuple(jnp.full((8, LANE), _NEG, F32) for _ in range(nh))
    mred = lax.cond(
        jnp.max(nequal_ref[...]) <= 1.0,
        lambda: lax.fori_loop(0, 2 * n_ch2, mask_chunk_no_ties, mred0),
        lambda: lax.fori_loop(0, 2 * n_ch2, mask_chunk, (jnp.zeros((1, LANE), F32), mred0))[1])
    m_row = [jnp.max(mh, axis=0, keepdims=True) for mh in mred]

    acc_ref[...] = jnp.zeros_like(acc_ref)

    def pv_pair(j2, carry):
        pv = None
        for jc in (2 * j2, 2 * j2 + 1):
            sel = key_ref[jc] != 0
            ps = [jnp.exp((jnp.where(sel, lg_ref[jc, :, h * TILE:(h + 1) * TILE], _NEG) - m_row[h]).astype(BF16))
                  for h in range(nh)]
            term = _dot(vt_ref[jc], jnp.concatenate(ps, axis=1))
            pv = term if pv is None else pv + term
        acc_ref[...] += pv
        return carry

    lax.fori_loop(0, n_ch2, pv_pair, 0)

    heads = []
    for h in range(nh):
        blk = acc_ref[:, h * TILE:(h + 1) * TILE]
        o_t = (blk / blk[HALF:HALF + 1, :]).T
        heads.append(o_t if h % 2 == 0 else pltpu.roll(o_t, HALF, 1))
    pairs = [jnp.where(lo, heads[2 * mp], heads[2 * mp + 1]) for mp in range(nh // 2)]
    o_ref[...] = (jnp.concatenate(pairs, axis=1) * _silu(gate_ref[...].astype(F32))).astype(o_ref.dtype)


def _dsa(p3, sm3, q_norm_g, k_norm_g):
    b, l, _ = p3.shape
    top_k = min(DSA_TOPK_MAX, l // 4)
    nb = l // TILE
    nc = l // KEY_CHUNK
    sl = np.tril(np.ones((KEY_CHUNK, KEY_CHUNK), np.float32), -1)
    dup = lambda g: jnp.concatenate([g, g]).reshape(1, LANE).astype(F32)
    consts = [dup(q_norm_g), dup(k_norm_g), jnp.asarray(sl, BF16)]
    qblk = lambda name: pl.BlockSpec((None, TILE, BRANCH_W), lambda i, j: (i, j, P_OFF[name] // BRANCH_W))
    seq = lambda name: pl.BlockSpec((None, l, LANE), lambda i, j: (i, 0, P_OFF[name] // LANE),
                                    pipeline_mode=pl.Buffered(1))
    chunks = lambda dt: pltpu.VMEM((nc, KEY_CHUNK, LANE), dt)
    return pl.pallas_call(
        functools.partial(_dsa_kernel, top_k=top_k),
        grid=(b, nb),
        in_specs=[qblk("c_q"), qblk("c_iq"),
                  pl.BlockSpec((None, TILE, LANE), lambda i, j: (i, j, 0)),
                  qblk("c_gate"), seq("c_kk"), seq("c_vv"), seq("c_ikik")]
        + [_full_spec(a.shape) for a in consts],
        out_specs=pl.BlockSpec((None, TILE, BRANCH_W), lambda i, j: (i, j, 0)),
        out_shape=jax.ShapeDtypeStruct((b, l, BRANCH_W), BF16),
        scratch_shapes=[chunks(BF16), pltpu.VMEM((nc, LANE, KEY_CHUNK), BF16), chunks(BF16), chunks(I32),
                        pltpu.VMEM((nc, 32, 8, LANE), I32),
                        pltpu.VMEM((nc, KEY_CHUNK, DSA_HEADS * TILE), F32), pltpu.VMEM((8, LANE), I32),
                        pltpu.VMEM((8, LANE), F32), pltpu.VMEM((8, LANE), F32),
                        pltpu.VMEM((LANE, DSA_HEADS * TILE), F32)],
        compiler_params=pltpu.CompilerParams(dimension_semantics=("arbitrary", "arbitrary"),
                                             vmem_limit_bytes=VMEM_LIMIT),
        name="dsa",
    )(p3, p3, sm3, p3, p3, p3, p3, *consts)


def _ssd_kernel(z_ref, xbc_ref, small_ref, cw_ref, cbias_ref, dtb_ref, apad_ref, dexp_ref, ng_ref, tril_ref,
                e64_ref, e128_ref, o_ref, st_ref, halo_ref, buf_ref):
    _tiles_in_order(_ssd_tile, (z_ref, xbc_ref, small_ref),
                    (cw_ref, cbias_ref, dtb_ref, apad_ref, dexp_ref, ng_ref, tril_ref, e64_ref, e128_ref),
                    o_ref, (st_ref, halo_ref, buf_ref))


def _ssd_tile(z_ref, xbc_ref, small_ref, cw_ref, cbias_ref, dtb_ref, apad_ref, dexp_ref, ng_ref, tril_ref,
              e64_ref, e128_ref, o_ref, st_ref, halo_ref, buf_ref):
    nhp = SSD_HEADS * SSD_HD
    gs = SSD_STATE
    act = _silu(_causal_conv(xbc_ref[...].astype(F32), cw_ref, halo_ref, buf_ref) + cbias_ref[...])
    xs = act[:, :nhp]
    bm = act[:, nhp:nhp + SSD_GROUPS * gs]
    cm = act[:, nhp + SSD_GROUPS * gs:]
    dt_full = _softplus(small_ref[...] + dtb_ref[...])
    cs_full = _dot01_l(tril_ref[...], dt_full * apad_ref[...])
    dt_exp = _dot01_r(dt_full, e64_ref[...])
    cs_exp = _dot01_r(cs_full, e64_ref[...])
    cs_lb = _dot01_r(cs_full, e128_ref[...])
    xdt = xs * dt_exp
    cs_last = cs_exp[TILE - 1:TILE, :]
    xds = (xdt * jnp.exp(cs_last - cs_exp)).astype(BF16)
    ecs = jnp.exp(cs_exp)
    xdt_b = xdt.astype(BF16)

    r = lax.broadcasted_iota(I32, (TILE, TILE), 0)
    c = lax.broadcasted_iota(I32, (TILE, TILE), 1)
    incl = r >= c
    lo = c < HALF
    hpg = SSD_HEADS // SSD_GROUPS
    gw = hpg * SSD_HD
    y_parts = []
    for g in range(SSD_GROUPS):
        bmg = bm[:, g * gs:(g + 1) * gs]
        cmg = cm[:, g * gs:(g + 1) * gs].astype(BF16)
        cbg = _dot_nt(cmg, bmg.astype(BF16))
        for mp in range(hpg // 2):
            pair = (g * hpg) // 2 + mp
            xpair = xdt_b[:, pair * LANE:(pair + 1) * LANE]
            ys = []
            for e in range(2):
                h = 2 * pair + e
                cb = cs_lb[:, h * LANE:(h + 1) * LANE]
                lm = jnp.where(incl, jnp.exp(jnp.where(incl, cb - cb.T, 0.0)), 0.0)
                ys.append(_dot((cbg * lm).astype(BF16), xpair))
            y_parts.append(jnp.where(lo, ys[0], ys[1]))
        st = st_ref[g]
        y_off = _dot(cmg, st.astype(BF16)) * ecs[:, g * gw:(g + 1) * gw]
        y_parts[-(hpg // 2):] = [yp + y_off[:, k * LANE:(k + 1) * LANE]
                                 for k, yp in enumerate(y_parts[-(hpg // 2):])]
        new = _dot(bmg.T.astype(BF16), xds[:, g * gw:(g + 1) * gw])
        st_ref[g] = jnp.exp(cs_last[:, g * gw:(g + 1) * gw]) * st + new
    y = jnp.concatenate(y_parts, axis=1) + xs * dexp_ref[...]
    yz = y * _silu(z_ref[...].astype(F32))
    o_ref[...] = _rms(yz, ng_ref[...]).astype(o_ref.dtype)


def _ssd(p3, sm3, conv_w, conv_b, a_log, dt_bias, d_skip, norm_g):
    b, l, _ = p3.shape
    step = min(SEQ_STEP, l)
    consts = [conv_w.astype(F32), conv_b.reshape(1, -1).astype(F32), _pad_lanes(dt_bias, SM_DT),
              _pad_lanes(-jnp.exp(a_log.astype(F32)), SM_DT),
              jnp.repeat(d_skip.astype(F32), SSD_HD).reshape(1, -1), norm_g.reshape(1, -1).astype(F32),
              jnp.asarray(np.tril(np.ones((TILE, TILE), np.float32)), BF16),
              jnp.asarray(_np_lane_expand(SM_DT, SSD_HEADS, SSD_HD), BF16),
              jnp.asarray(_np_lane_expand(SM_DT, SSD_HEADS, LANE), BF16)]
    return pl.pallas_call(
        _ssd_kernel,
        grid=(b, l // step),
        in_specs=[pl.BlockSpec((None, step, BRANCH_W), lambda i, j: (i, j, P_OFF["d_z"] // BRANCH_W)),
                  pl.BlockSpec((None, step, SSD_XBC), lambda i, j: (i, j, P_OFF["d_xbc"] // SSD_XBC)),
                  pl.BlockSpec((None, step, LANE), lambda i, j: (i, j, 0))]
        + [_full_spec(a.shape) for a in consts],
        out_specs=pl.BlockSpec((None, step, BRANCH_W), lambda i, j: (i, j, 0)),
        out_shape=jax.ShapeDtypeStruct((b, l, BRANCH_W), BF16),
        scratch_shapes=[pltpu.VMEM((SSD_GROUPS, SSD_STATE, (SSD_HEADS // SSD_GROUPS) * SSD_HD), F32),
                        pltpu.VMEM((8, SSD_XBC), F32), pltpu.VMEM((TILE + 8, SSD_XBC), F32)],
        compiler_params=pltpu.CompilerParams(dimension_semantics=("arbitrary", "arbitrary"),
                                             vmem_limit_bytes=VMEM_LIMIT),
        name="ssd",
    )(p3, p3, sm3, *consts)


def _mem_kv_kernel(mem_ref, g_ref, w_ref, kg_ref, k_ref, v_ref):
    kv = _dot(_rms(mem_ref[...], g_ref[...]).astype(BF16), w_ref[...])
    for h in range(MEM_HEADS):
        hs = slice(h * MEM_HD, (h + 1) * MEM_HD)
        k_ref[:, hs] = _rms(kv[:, hs], kg_ref[...]).astype(k_ref.dtype)
    v_ref[...] = kv[:, BRANCH_W:].astype(v_ref.dtype)


def _mem_kv(mem, mem_norm_g, w_kv_bf16, k_norm_g):
    b, m, _ = mem.shape
    consts = [mem_norm_g.reshape(1, -1).astype(F32), w_kv_bf16, k_norm_g.reshape(1, -1).astype(F32)]
    out = jax.ShapeDtypeStruct((b, m, BRANCH_W), BF16)
    return pl.pallas_call(
        _mem_kv_kernel,
        grid=(b,),
        in_specs=[pl.BlockSpec((None, m, D_MODEL), lambda i: (i, 0, 0))] + [_full_spec(a.shape) for a in consts],
        out_specs=[pl.BlockSpec((None, m, BRANCH_W), lambda i: (i, 0, 0))] * 2,
        out_shape=[out, out],
        compiler_params=pltpu.CompilerParams(dimension_semantics=("arbitrary",), vmem_limit_bytes=VMEM_LIMIT),
        name="mem_kv",
    )(mem, *consts)


def _mem_attn_kernel(q_ref, gate_ref, k_ref, v_ref, qg_ref, o_ref):
    q = q_ref[...].astype(F32)
    outs = []
    for h in range(MEM_HEADS):
        hs = slice(h * MEM_HD, (h + 1) * MEM_HD)
        qn = _rms(q[:, hs], qg_ref[...]).astype(BF16)
        lg = _dot_nt(qn, k_ref[:, hs]) * (MEM_HD ** -0.5)
        p = jnp.exp(lg - jnp.max(lg, axis=1, keepdims=True))
        outs.append(_dot(p.astype(BF16), v_ref[:, hs]) / jnp.sum(p, axis=1, keepdims=True))
    o_ref[...] = (jnp.concatenate(outs, axis=1) * _silu(gate_ref[...].astype(F32))).astype(o_ref.dtype)


def _mem_attn(p3, k, v, q_norm_g):
    b, l, _ = p3.shape
    m = k.shape[1]
    tm = min(512, l)
    qg = q_norm_g.reshape(1, -1).astype(F32)
    blk = lambda name: pl.BlockSpec((None, tm, BRANCH_W), lambda i, j: (i, j, P_OFF[name] // BRANCH_W))
    kv = pl.BlockSpec((None, m, BRANCH_W), lambda i, j: (i, 0, 0))
    return pl.pallas_call(
        _mem_attn_kernel,
        grid=(b, l // tm),
        in_specs=[blk("m_q"), blk("m_gate"), kv, kv, _full_spec(qg.shape)],
        out_specs=pl.BlockSpec((None, tm, BRANCH_W), lambda i, j: (i, j, 0)),
        out_shape=jax.ShapeDtypeStruct((b, l, BRANCH_W), BF16),
        compiler_params=pltpu.CompilerParams(dimension_semantics=("arbitrary", "arbitrary"),
                                             vmem_limit_bytes=VMEM_LIMIT),
        name="mem_attn",
    )(p3, p3, k, v, qg)


def _reordered_w_in(w):
    parts, start = [], 0
    cols = _P_COLUMNS
    for end in range(1, len(cols) + 1):
        if end == len(cols) or (cols[end] < 0) != (cols[start] < 0) or (cols[start] >= 0 and cols[end] != cols[end - 1] + 1):
            if cols[start] < 0:
                parts.append(jnp.zeros((w.shape[0], end - start), BF16))
            else:
                parts.append(w[:, int(cols[start]):int(cols[end - 1]) + 1].astype(BF16))
            start = end
    return jnp.concatenate(parts, axis=1)


def kernel(x, mem, norm_g, w_in, gdn_conv_w, gdn_a_log, gdn_dt_bias, gdn_norm_g, sg_ln_g, sg_ln_b, sg_w, sg_b, dsa_q_norm_g, dsa_k_norm_g, ssd_conv_w, ssd_conv_b, ssd_a_log, ssd_dt_bias, ssd_d, ssd_norm_g, mem_norm_g, w_mem_kv, mem_q_norm_g, mem_k_norm_g, w_gate, w_branch, w_out):
    b, l, d = x.shape
    depth = norm_g.shape[0]
    x2d = x.reshape(b * l, d)
    for i in range(depth):
        g = norm_g[i].reshape(1, d).astype(F32)
        p2d, sm2d = _in_proj(x2d, g, _reordered_w_in(w_in[i]))
        p3 = p2d.reshape(b, l, P_MAIN)
        sm3 = sm2d.reshape(b, l, LANE)
        mk, mv = _mem_kv(mem, mem_norm_g[i], w_mem_kv[i].astype(BF16), mem_k_norm_g[i])
        ys = (
            _gdn(p3, sm3, gdn_conv_w[i], gdn_a_log[i], gdn_dt_bias[i], gdn_norm_g[i]),
            _sg(p3, sg_ln_g[i], sg_ln_b[i], sg_w[i], sg_b[i]),
            _dsa(p3, sm3, dsa_q_norm_g[i], dsa_k_norm_g[i]),
            _ssd(p3, sm3, ssd_conv_w[i], ssd_conv_b[i], ssd_a_log[i], ssd_dt_bias[i], ssd_d[i], ssd_norm_g[i]),
            _mem_attn(p3, mk, mv, mem_q_norm_g[i]),
        )
        ys = [y.reshape(b * l, BRANCH_W) for y in ys]
        x2d = _merge(x2d, g, ys, w_gate[i].astype(BF16), w_branch[i].astype(BF16), w_out[i].astype(BF16))
    return x2d.reshape(b, l, d)
```

```python
import functools

import numpy as np
import jax
import jax.numpy as jnp
from jax import lax
from jax.experimental import pallas as pl
from jax.experimental.pallas import tpu as pltpu

F32 = jnp.float32
BF16 = jnp.bfloat16
I32 = jnp.int32

D_MODEL = 1024
N_BRANCH = 5
BRANCH_W = D_MODEL // 2
CONV_K = 4
EPS = 1e-6
GDN_HEADS = 4
GDN_DK = BRANCH_W // GDN_HEADS
GDN_CHUNK = 64
GDN_QKV = 3 * BRANCH_W
SG_CHUNK = 128
SG_GROUPS = 4
DSA_HEADS = 8
DSA_HD = BRANCH_W // DSA_HEADS
IDX_HEADS = 8
IDX_HD = 64
DSA_TOPK_MAX = 256
Q_BLOCK = 128
SSD_HEADS = 8
SSD_HD = BRANCH_W // SSD_HEADS
SSD_GROUPS = 2
SSD_STATE = 128
SSD_CHUNK = 128
SSD_XBC = SSD_HEADS * SSD_HD + 2 * SSD_GROUPS * SSD_STATE
MEM_HEADS = 4
MEM_HD = BRANCH_W // MEM_HEADS

IN_SPLITS = (
    GDN_QKV, GDN_HEADS, GDN_HEADS, BRANCH_W,
    BRANCH_W, BRANCH_W, BRANCH_W,
    DSA_HEADS * DSA_HD, DSA_HD, DSA_HD, IDX_HEADS * IDX_HD, IDX_HD, IDX_HEADS, BRANCH_W,
    BRANCH_W, SSD_XBC, SSD_HEADS,
    MEM_HEADS * MEM_HD, BRANCH_W,
)
_SEG_NAMES = ("a_qkv", "a_a", "a_b", "a_gate", "b_u", "b_v", "b_gate", "c_q", "c_k", "c_v", "c_iq", "c_ik",
              "c_iw", "c_gate", "d_z", "d_xbc", "d_dt", "m_q", "m_gate")
_SEG_START = dict(zip(_SEG_NAMES, np.concatenate([[0], np.cumsum(IN_SPLITS)[:-1]]).tolist()))
_SEG_WIDTH = dict(zip(_SEG_NAMES, IN_SPLITS))

LANE = 128
HALF = LANE // 2
TILE = 128
SEQ_STEP = 4 * TILE
VMEM_LIMIT = 56 * 1024 * 1024

_P_LAYOUT = (("a_qkv", 1536), ("b_u", 512), ("d_xbc", 1024), ("a_gate", 512), ("b_v", 512), ("b_gate", 512),
             ("c_q", 512), ("c_iq", 512), ("c_gate", 512), ("d_z", 512), ("m_q", 512), ("m_gate", 512),
             ("c_kk", 128), ("c_vv", 128), ("c_ikik", 128), ("small", 128))
P_OFF = {}
_o = 0
for _n, _w in _P_LAYOUT:
    P_OFF[_n] = _o
    _o += _w
P_COLS = _o
P_MAIN = P_OFF["small"]
SM_A, SM_B, SM_DT, SM_IW = 0, 4, 8, 16


def _p_columns():
    cols = []
    for name, width in _P_LAYOUT:
        if name in _SEG_START:
            cols += list(range(_SEG_START[name], _SEG_START[name] + width))
        elif name in ("c_kk", "c_vv", "c_ikik"):
            src = {"c_kk": "c_k", "c_vv": "c_v", "c_ikik": "c_ik"}[name]
            one = list(range(_SEG_START[src], _SEG_START[src] + _SEG_WIDTH[src]))
            cols += one + one
        else:
            sm = [-1] * LANE
            for seg, at in (("a_a", SM_A), ("a_b", SM_B), ("d_dt", SM_DT), ("c_iw", SM_IW)):
                for j in range(_SEG_WIDTH[seg]):
                    sm[at + j] = _SEG_START[seg] + j
            cols += sm
    return np.asarray(cols, np.int32)


_P_COLUMNS = _p_columns()


def _dot(a, b, precision=None):
    return jnp.dot(a, b, preferred_element_type=F32, precision=precision)


def _dot_nt(a, b):
    return lax.dot_general(a, b, (((1,), (1,)), ((), ())), preferred_element_type=F32)


def _split3(x):
    hi = x.astype(BF16)
    r1 = x - hi.astype(F32)
    mid = r1.astype(BF16)
    lo = (r1 - mid.astype(F32)).astype(BF16)
    return hi, mid, lo


def _dot01_r(x, e):
    hi, mid, lo = _split3(x)
    return _dot(hi, e) + _dot(mid, e) + _dot(lo, e)


def _dot01_l(e, x):
    hi, mid, lo = _split3(x)
    return _dot(e, hi) + _dot(e, mid) + _dot(e, lo)


def _sigmoid(x):
    return 1.0 / (1.0 + jnp.exp(-x))


def _silu(x):
    return x * _sigmoid(x)


def _softplus(x):
    return jnp.maximum(x, 0.0) + jnp.log1p(jnp.exp(-jnp.abs(x)))


def _gelu_tanh(x):
    return 0.5 * x * (1.0 + jnp.tanh(np.sqrt(2.0 / np.pi).astype(np.float32) * (x + 0.044715 * (x * x * x))))


def _rms(x, g, eps=EPS):
    return x * lax.rsqrt(jnp.mean(x * x, axis=-1, keepdims=True) + eps) * g


def _full_spec(shape):
    nd = len(shape)
    return pl.BlockSpec(shape, lambda *_: (0,) * nd)


def _in_proj_kernel(x_ref, g_ref, w_ref, o_ref, s_ref):
    h = _rms(x_ref[...], g_ref[...]).astype(BF16)
    step = 1024
    for c0 in range(0, P_MAIN, step):
        c1 = min(c0 + step, P_MAIN)
        o_ref[:, c0:c1] = _dot(h, w_ref[:, c0:c1]).astype(o_ref.dtype)
    s_ref[...] = _dot(h, w_ref[:, P_MAIN:])


def _in_proj(x2d, g, w_bf16):
    m = x2d.shape[0]
    tm = min(512, m)
    return pl.pallas_call(
        _in_proj_kernel,
        grid=(m // tm,),
        in_specs=[pl.BlockSpec((tm, D_MODEL), lambda i: (i, 0)),
                  pl.BlockSpec((1, D_MODEL), lambda i: (0, 0)),
                  pl.BlockSpec((D_MODEL, P_COLS), lambda i: (0, 0), pipeline_mode=pl.Buffered(1))],
        out_specs=[pl.BlockSpec((tm, P_MAIN), lambda i: (i, 0)), pl.BlockSpec((tm, LANE), lambda i: (i, 0))],
        out_shape=[jax.ShapeDtypeStruct((m, P_MAIN), BF16), jax.ShapeDtypeStruct((m, LANE), F32)],
        compiler_params=pltpu.CompilerParams(dimension_semantics=("arbitrary",), vmem_limit_bytes=VMEM_LIMIT),
        name="in_proj",
    )(x2d, g, w_bf16)


def _merge_kernel(x_ref, g_ref, y0, y1, y2, y3, y4, wg_ref, wb_ref, wo_ref, o_ref):
    x = x_ref[...]
    h = _rms(x, g_ref[...]).astype(BF16)
    acc = None
    for p, y in enumerate((y0, y1, y2, y3, y4)):
        term = _sigmoid(_dot(h, wg_ref[p])) * _dot(y[...], wb_ref[p])
        acc = term if acc is None else acc + term
    o_ref[...] = x + _dot(acc.astype(BF16), wo_ref[...])


def _merge(x2d, g, ys, wg, wb, wo):
    m = x2d.shape[0]
    tm = min(512, m)
    row = lambda i: (i, 0)
    const = pl.Buffered(1)
    return pl.pallas_call(
        _merge_kernel,
        grid=(m // tm,),
        in_specs=[pl.BlockSpec((tm, D_MODEL), row), pl.BlockSpec((1, D_MODEL), lambda i: (0, 0))]
        + [pl.BlockSpec((tm, BRANCH_W), row)] * N_BRANCH
        + [pl.BlockSpec((N_BRANCH, D_MODEL, D_MODEL), lambda i: (0, 0, 0), pipeline_mode=const),
           pl.BlockSpec((N_BRANCH, BRANCH_W, D_MODEL), lambda i: (0, 0, 0), pipeline_mode=const),
           pl.BlockSpec((D_MODEL, D_MODEL), lambda i: (0, 0), pipeline_mode=const)],
        out_specs=pl.BlockSpec((tm, D_MODEL), row),
        out_shape=jax.ShapeDtypeStruct((m, D_MODEL), F32),
        compiler_params=pltpu.CompilerParams(dimension_semantics=("arbitrary",), vmem_limit_bytes=VMEM_LIMIT),
        name="merge_out",
    )(x2d, g, *ys, wg, wb, wo)


def _causal_conv(x, w_ref, halo_ref, buf_ref):
    buf_ref[0:8, :] = halo_ref[...]
    buf_ref[8:8 + TILE, :] = x
    y = w_ref[CONV_K - 1:CONV_K, :] * x
    for j in range(CONV_K - 1):
        y = y + w_ref[j:j + 1, :] * buf_ref[8 - (CONV_K - 1) + j:8 - (CONV_K - 1) + j + TILE, :]
    halo_ref[...] = x[TILE - 8:TILE, :]
    return y


def _np_lane_expand(first_row, n_heads, width):
    e = np.zeros((LANE, n_heads * width), np.float32)
    for h in range(n_heads):
        e[first_row + h, h * width:(h + 1) * width] = 1.0
    return e


def _pad_lanes(vec, at):
    out = jnp.zeros((1, LANE), F32)
    return lax.dynamic_update_slice(out, vec.reshape(1, -1).astype(F32), (0, at))


def _tiles_in_order(tile_fn, seq_refs, const_refs, o_ref, scratch_refs):
    @pl.when(pl.program_id(1) == 0)
    def _():
        for ref in scratch_refs[:2]:
            ref[...] = jnp.zeros_like(ref)

    def body(t, carry):
        rows = pl.ds(pl.multiple_of(t * TILE, TILE), TILE)
        tile_fn(*[r.at[rows] for r in seq_refs], *const_refs, o_ref.at[rows], *scratch_refs)
        return carry

    lax.fori_loop(0, o_ref.shape[0] // TILE, body, 0)


def _gdn_kernel(qkv_ref, small_ref, gate_ref, cw_ref, apad_ref, dtb_ref, ng_ref, btril_ref, eg_ref, eb_ref,
                o_ref, st_ref, halo_ref, buf_ref):
    _tiles_in_order(_gdn_tile, (qkv_ref, small_ref, gate_ref),
                    (cw_ref, apad_ref, dtb_ref, ng_ref, btril_ref, eg_ref, eb_ref), o_ref, (st_ref, halo_ref, buf_ref))


def _gdn_tile(qkv_ref, small_ref, gate_ref, cw_ref, apad_ref, dtb_ref, ng_ref, btril_ref, eg_ref, eb_ref,
              o_ref, st_ref, halo_ref, buf_ref):
    act = _silu(_causal_conv(qkv_ref[...].astype(F32), cw_ref, halo_ref, buf_ref))
    small = small_ref[...]
    g_full = -apad_ref[...] * _softplus(small + dtb_ref[...])
    beta_full = _sigmoid(small)
    gc_full = _dot01_l(btril_ref[...], g_full)
    g_lb = _dot01_r(gc_full, eg_ref[...])
    b_lb = _dot01_r(beta_full, eb_ref[...])

    r = lax.broadcasted_iota(I32, (TILE, TILE), 0)
    c = lax.broadcasted_iota(I32, (TILE, TILE), 1)
    same = (r >> 6) == (c >> 6)
    incl = same & (r >= c)
    strict = same & (r > c)
    eye = (r == c).astype(F32)
    n_sub = TILE // GDN_CHUNK
    ng = ng_ref[...]
    heads = range(GDN_HEADS)
    q, k, v, cb, bb = [], [], [], [], []
    for h in heads:
        qh = act[:, h * GDN_DK:(h + 1) * GDN_DK]
        kh = act[:, BRANCH_W + h * GDN_DK:BRANCH_W + (h + 1) * GDN_DK]
        q.append(qh * lax.rsqrt(jnp.sum(qh * qh, axis=-1, keepdims=True) + EPS) * (GDN_DK ** -0.5))
        k.append(kh * lax.rsqrt(jnp.sum(kh * kh, axis=-1, keepdims=True) + EPS))
        v.append(act[:, 2 * BRANCH_W + h * GDN_DK:2 * BRANCH_W + (h + 1) * GDN_DK])
        cb.append(g_lb[:, h * LANE:(h + 1) * LANE])
        bb.append(b_lb[:, h * LANE:(h + 1) * LANE])
    dm = [jnp.where(incl, jnp.exp(jnp.where(incl, cb[h] - cb[h].T, 0.0)), 0.0) for h in heads]
    kb = [k[h].astype(BF16) for h in heads]
    kk = [_dot_nt(kb[h], kb[h]) for h in heads]
    qkb = [(_dot_nt(q[h].astype(BF16), kb[h]) * dm[h]).astype(BF16) for h in heads]
    mp = [jnp.where(strict, -(bb[h] * kk[h] * dm[h]), 0.0) for h in heads]
    inv = [eye + mp[h] for h in heads]
    for _ in range(int(np.log2(GDN_CHUNK)) - 1):
        mpb = [mp[h].astype(BF16) for h in heads]
        mp = [_dot(mpb[h], mpb[h]) for h in heads]
        inv = [inv[h] + _dot(inv[h].astype(BF16), mp[h].astype(BF16)) for h in heads]
    eg = [jnp.exp(cb[h]) for h in heads]
    sol = [_dot(inv[h].astype(BF16),
                jnp.concatenate([bb[h] * eg[h] * k[h], bb[h] * v[h]], axis=1).astype(BF16)) for h in heads]
    wk = [sol[h][:, :GDN_DK].astype(BF16) for h in heads]
    u0 = [sol[h][:, GDN_DK:] for h in heads]
    qd = [(q[h] * eg[h]).astype(BF16) for h in heads]
    zero = jnp.zeros((GDN_CHUNK, LANE), F32)
    st = [st_ref[h] for h in heads]
    o_parts = [[] for _ in heads]
    for cc in range(n_sub):
        rows = slice(cc * GDN_CHUNK, (cc + 1) * GDN_CHUNK)
        in_chunk = (r >> 6) == cc
        glast = [cb[h][(cc + 1) * GDN_CHUNK - 1:(cc + 1) * GDN_CHUNK, :] for h in heads]
        stb = [st[h].astype(BF16) for h in heads]
        u = [u0[h][rows] - _dot(wk[h][rows], stb[h]) for h in heads]
        upad = [jnp.concatenate([u[h] if j == cc else zero for j in range(n_sub)], axis=0).astype(BF16)
                for h in heads]
        for h in heads:
            o_parts[h].append(_dot(qd[h][rows], stb[h]) + _dot(qkb[h][rows], upad[h]))
        kd = [jnp.where(in_chunk, k[h] * jnp.exp(jnp.where(in_chunk, glast[h] - cb[h], 0.0)), 0.0) for h in heads]
        st = [jnp.exp(glast[h]) * st[h] + _dot(kd[h].T.astype(BF16), upad[h]) for h in heads]
    for h in heads:
        st_ref[h] = st[h]
    outs = [_rms(jnp.concatenate(o_parts[h], axis=0), ng) for h in heads]
    y = jnp.concatenate(outs, axis=1) * _silu(gate_ref[...].astype(F32))
    o_ref[...] = y.astype(o_ref.dtype)


def _gdn(p3, sm3, conv_w, a_log, dt_bias, norm_g):
    b, l, _ = p3.shape
    step = min(SEQ_STEP, l)
    btril = np.zeros((TILE, TILE), np.float32)
    for i in range(TILE):
        lo = (i // GDN_CHUNK) * GDN_CHUNK
        btril[i, lo:i + 1] = 1.0
    consts = [conv_w.astype(F32), _pad_lanes(jnp.exp(a_log.astype(F32)), SM_A), _pad_lanes(dt_bias, SM_A),
              norm_g.reshape(1, -1).astype(F32), jnp.asarray(btril, BF16),
              jnp.asarray(_np_lane_expand(SM_A, GDN_HEADS, LANE), BF16),
              jnp.asarray(_np_lane_expand(SM_B, GDN_HEADS, LANE), BF16)]
    return pl.pallas_call(
        _gdn_kernel,
        grid=(b, l // step),
        in_specs=[pl.BlockSpec((None, step, GDN_QKV), lambda i, j: (i, j, P_OFF["a_qkv"] // GDN_QKV)),
                  pl.BlockSpec((None, step, LANE), lambda i, j: (i, j, 0)),
                  pl.BlockSpec((None, step, BRANCH_W), lambda i, j: (i, j, P_OFF["a_gate"] // BRANCH_W))]
        + [_full_spec(a.shape) for a in consts],
        out_specs=pl.BlockSpec((None, step, BRANCH_W), lambda i, j: (i, j, 0)),
        out_shape=jax.ShapeDtypeStruct((b, l, BRANCH_W), BF16),
        scratch_shapes=[pltpu.VMEM((GDN_HEADS, GDN_DK, GDN_DK), F32), pltpu.VMEM((8, GDN_QKV), F32),
                        pltpu.VMEM((TILE + 8, GDN_QKV), F32)],
        compiler_params=pltpu.CompilerParams(dimension_semantics=("arbitrary", "arbitrary"),
                                             vmem_limit_bytes=VMEM_LIMIT),
        name="gdn",
    )(p3, sm3, p3, *consts)


SG_TILE = 4 * SG_CHUNK


def _sg_kernel(u_ref, v_ref, gate_ref, lng_ref, lnb_ref, w_ref, bs_ref, o_ref):
    r = lax.broadcasted_iota(I32, (SG_CHUNK, SG_CHUNK), 0)
    c = lax.broadcasted_iota(I32, (SG_CHUNK, SG_CHUNK), 1)
    gw = BRANCH_W // SG_GROUPS
    wc = [jnp.where(r >= c, w_ref[g], 0.0).astype(BF16) for g in range(SG_GROUPS)]
    for t0 in range(0, u_ref.shape[0], SG_CHUNK):
        rows = slice(t0, t0 + SG_CHUNK)
        u = _gelu_tanh(u_ref[rows, :].astype(F32))
        v = _gelu_tanh(v_ref[rows, :].astype(F32))
        mu = jnp.mean(v, axis=-1, keepdims=True)
        var = jnp.mean(jnp.square(v - mu), axis=-1, keepdims=True)
        vn = ((v - mu) * lax.rsqrt(var + 1e-5) * lng_ref[...] + lnb_ref[...]).astype(BF16)
        mixed = jnp.concatenate([_dot(wc[g], vn[:, g * gw:(g + 1) * gw]) for g in range(SG_GROUPS)], axis=1)
        o_ref[rows, :] = (u * (mixed + bs_ref[...]) * _silu(gate_ref[rows, :].astype(F32))).astype(o_ref.dtype)


def _sg(p3, ln_g, ln_b, w_s, b_s):
    b, l, _ = p3.shape
    gw = BRANCH_W // SG_GROUPS
    tile = min(SG_TILE, l)
    bs_full = jnp.repeat(jnp.swapaxes(b_s, 0, 1).astype(F32), gw, axis=1)
    consts = [ln_g.reshape(1, -1).astype(F32), ln_b.reshape(1, -1).astype(F32), w_s.astype(F32), bs_full]
    blk = lambda name: pl.BlockSpec((None, tile, BRANCH_W), lambda i, j: (i, j, P_OFF[name] // BRANCH_W))
    return pl.pallas_call(
        _sg_kernel,
        grid=(b, l // tile),
        in_specs=[blk("b_u"), blk("b_v"), blk("b_gate")] + [_full_spec(a.shape) for a in consts],
        out_specs=pl.BlockSpec((None, tile, BRANCH_W), lambda i, j: (i, j, 0)),
        out_shape=jax.ShapeDtypeStruct((b, l, BRANCH_W), BF16),
        compiler_params=pltpu.CompilerParams(dimension_semantics=("arbitrary", "arbitrary"),
                                             vmem_limit_bytes=VMEM_LIMIT),
        name="spatial_gating",
    )(p3, p3, p3, *consts)


_NEG = -1e30
_INT_MIN = -2 ** 31
_INT_MAX = 2 ** 31 - 1


def _fold8(x, op):
    parts = [x[r:r + 8] for r in range(0, x.shape[0], 8)]
    while len(parts) > 1:
        parts = [op(parts[k], parts[k + 1]) for k in range(0, len(parts) - 1, 2)] + (
            [parts[-1]] if len(parts) % 2 else [])
    return parts[0]


KEY_CHUNK = 2 * TILE


def _bit_planes(u):
    a = [u[8 * g:8 * g + 8] for g in range(32)]
    j, m = 16, 0x0000FFFF
    while j:
        k = 0
        while k < 32:
            t = (a[k] ^ (a[k + j] >> j)) & m
            a[k] = a[k] ^ t
            a[k + j] = a[k + j] ^ (t << j)
            k = (k + j + 1) & ~j
        j >>= 1
        m = m ^ ((m << j) & 0xFFFFFFFF)
    return [a[31 - b] for b in range(32)]


def _dsa_kernel(q_ref, iq_ref, small_ref, gate_ref, kk_ref, vv_ref, ik_ref, qg_ref, kg_ref, sl_ref,
                o_ref, kn_ref, vt_ref, ikb_ref, key_ref, plane_ref, lg_ref, tau_ref, need_ref, nequal_ref, acc_ref,
                *, top_k):
    i = pl.program_id(1)
    nh = DSA_HEADS
    nc = key_ref.shape[0]
    kc = KEY_CHUNK

    srow = lax.broadcasted_iota(I32, (kc, LANE), 0)
    lane = lax.broadcasted_iota(I32, (kc, LANE), 1)
    qpos = i * TILE + lane
    pos_lanes = (lane == HALF) | (lane == HALF + 1)

    @pl.when(i == 0)
    def _():
        def prep(jc, carry):
            rows = pl.ds(pl.multiple_of(jc * kc, kc), kc)
            kpos = jnp.where(lane == HALF, srow, jc * kc).astype(F32)
            kn = jnp.where(lane < HALF, _rms(kk_ref[rows, :].astype(F32), kg_ref[...]), jnp.where(pos_lanes, kpos, 0.0))
            kn_ref[jc] = kn.astype(BF16)
            v = jnp.where(lane < HALF, vv_ref[rows, :].astype(F32), 1.0)
            vt_ref[jc] = jnp.concatenate([v[r:r + TILE].T for r in range(0, kc, TILE)], axis=1).astype(BF16)
            ikb_ref[jc] = ik_ref[rows, :]
            plane_ref[jc] = jnp.zeros(plane_ref.shape[1:], I32)
            return carry
        lax.fori_loop(0, nc, prep, 0)

    n_ch = (i + 2) // 2
    n_ch2 = (n_ch + 1) // 2

    iq = iq_ref[...].astype(F32)
    lane_q = lax.broadcasted_iota(I32, (TILE, LANE), 1)
    lo = lane_q < HALF
    pos_lanes_q = (lane_q == HALF) | (lane_q == HALF + 1)
    iq_st = jnp.concatenate(
        [jnp.where(lo if h % 2 == 0 else ~lo, iq[:, (h // 2) * LANE:(h // 2 + 1) * LANE], 0.0) for h in range(nh)],
        axis=0).astype(BF16)
    small_t = small_ref[...].T * (IDX_HEADS ** -0.5 * IDX_HD ** -0.5)
    iw_row = jnp.concatenate([small_t[SM_IW + h:SM_IW + h + 1, :] for h in range(nh)], axis=1)

    q = q_ref[...].astype(F32)
    q_parts = []
    for mpair in range(nh // 2):
        qp = q[:, mpair * LANE:(mpair + 1) * LANE]
        sq = qp * qp
        s_lo = jnp.sum(jnp.where(lo, sq, 0.0), axis=1, keepdims=True)
        s_hi = jnp.sum(jnp.where(lo, 0.0, sq), axis=1, keepdims=True)
        qn = qp * lax.rsqrt(jnp.where(lo, s_lo, s_hi) * (1.0 / DSA_HD) + EPS) * qg_ref[...] * (DSA_HD ** -0.5)
        for e, qh in enumerate((qn, pltpu.roll(qn, HALF, 1))):
            slope = 2.0 ** (-8.0 * (2 * mpair + e + 1) / nh)
            q_parts.append(jnp.where(lo, qh, jnp.where(pos_lanes_q, slope, 0.0)))
    q_st = jnp.concatenate(q_parts, axis=0).astype(BF16)

    def score_pair(j2, carry):
        jcs = (2 * j2, 2 * j2 + 1)
        raw = [_dot_nt(ikb_ref[jc], iq_st) for jc in jcs]
        for jc in jcs:
            lg_ref[jc] = _dot_nt(kn_ref[jc], q_st)
        for jc, s_raw in zip(jcs, raw):
            s_h = jnp.maximum(s_raw, 0.0) * iw_row
            s = s_h[:, 0:TILE]
            for h in range(1, nh):
                s = s + s_h[:, h * TILE:(h + 1) * TILE]
            bits = pltpu.bitcast(s, I32)
            key = jnp.where(bits < 0, bits ^ 0x7FFFFFFF, bits)
            key = jnp.where(jc * kc + srow <= qpos, key, _INT_MIN)
            key_ref[jc] = key
            planes = _bit_planes(key ^ _INT_MIN)
            for b in range(32):
                plane_ref[jc, b] = planes[b]
        return carry

    lax.fori_loop(0, n_ch2, score_pair, 0)

    tau_ref[...] = jnp.full(tau_ref.shape, _INT_MIN + 1, I32)
    need_ref[...] = jnp.zeros_like(need_ref)
    nequal_ref[...] = jnp.ones_like(nequal_ref)

    @pl.when((i + 1) * TILE > top_k)
    def _():
        def ones_at(live, b):
            acc = jnp.zeros((8, LANE), I32)
            for jc in range(nc):
                acc = acc + lax.population_count(live[jc] & plane_ref[jc, b])
            return jnp.sum(acc.astype(F32), axis=0, keepdims=True)

        live0 = tuple(jnp.where(jc < 2 * n_ch2, jnp.full((8, LANE), -1, I32), 0) for jc in range(nc))

        def bit_body(bi, carry):
            live, cnt, left, tau_u = carry
            b = 31 - bi
            take = cnt >= left
            left = jnp.where(take, left, left - cnt)
            tau_u = jnp.where(take, tau_u | lax.shift_left(jnp.int32(1), b), tau_u)
            b_next = jnp.maximum(b - 1, 0)
            acc = jnp.zeros((8, LANE), I32)
            new_live = []
            for jc in range(nc):
                with_bit = live[jc] & plane_ref[jc, b]
                keep = jnp.where(take, with_bit, live[jc] ^ with_bit)
                new_live.append(keep)
                acc = acc + lax.population_count(keep & plane_ref[jc, b_next])
            return tuple(new_live), jnp.sum(acc.astype(F32), axis=0, keepdims=True), left, tau_u

        init = (live0, ones_at(live0, 31), jnp.full((1, LANE), float(top_k), F32), jnp.zeros((1, LANE), I32))
        live, _, left, tau_u = lax.fori_loop(0, 32, bit_body, init)
        n_equal = jnp.zeros((8, LANE), I32)
        for jc in range(nc):
            n_equal = n_equal + lax.population_count(live[jc])
        n_equal = jnp.sum(n_equal.astype(F32), axis=0, keepdims=True)
        takes_all = qpos[0:1, :] + 1 <= top_k
        tau_ref[...] = jnp.broadcast_to(jnp.where(takes_all, _INT_MIN + 1, tau_u ^ _INT_MIN), tau_ref.shape)
        need_ref[...] = jnp.broadcast_to(jnp.where(takes_all, 0.0, left), need_ref.shape)
        nequal_ref[...] = jnp.broadcast_to(jnp.where(takes_all, 1.0, n_equal), nequal_ref.shape)

    tau = tau_ref[0:1, :]
    need = need_ref[0:1, :]

    def mask_chunk_no_ties(jc, mred):
        sel = key_ref[jc] >= tau
        new = []
        for h in range(nh):
            x = jnp.where(sel, lg_ref[jc, :, h * TILE:(h + 1) * TILE], _NEG)
            new.append(jnp.maximum(mred[h], _fold8(x, jnp.maximum)))
        return tuple(new)

    def mask_chunk(jc, carry):
        run, mred = carry
        key = key_ref[jc]
        eq = key == tau
        eqf = jnp.where(eq, 1.0, 0.0)
        prefix = _dot(sl_ref[...], eqf.astype(BF16)) + run
        sel = (key > tau) | (eq & (prefix < need))
        new = []
        for h in range(nh):
            x = jnp.where(sel, lg_ref[jc, :, h * TILE:(h + 1) * TILE], _NEG)
            new.append(jnp.maximum(mred[h], _fold8(x, jnp.maximum)))
        key_ref[jc] = jnp.where(sel, _INT_MAX, _INT_MIN)
        return run + jnp.sum(_fold8(eqf, jnp.add), axis=0, keepdims=True), tuple(new)

    mred0 = tuple(jnp.full((8, LANE), _NEG, F32) for _ in range(nh))
    mred = lax.cond(
        jnp.max(nequal_ref[...]) <= 1.0,
        lambda: lax.fori_loop(0, 2 * n_ch2, mask_chunk_no_ties, mred0),
        lambda: lax.fori_loop(0, 2 * n_ch2, mask_chunk, (jnp.zeros((1, LANE), F32), mred0))[1])
    m_row = [jnp.max(mh, axis=0, keepdims=True) for mh in mred]

    acc_ref[...] = jnp.zeros_like(acc_ref)

    def pv_pair(j2, carry):
        pv = None
        for jc in (2 * j2, 2 * j2 + 1):
            sel = key_ref[jc] >= tau
            ps = [jnp.exp((jnp.where(sel, lg_ref[jc, :, h * TILE:(h + 1) * TILE], _NEG) - m_row[h]).astype(BF16))
                  for h in range(nh)]
            term = _dot(vt_ref[jc], jnp.concatenate(ps, axis=1))
            pv = term if pv is None else pv + term
        acc_ref[...] += pv
        return carry

    lax.fori_loop(0, n_ch2, pv_pair, 0)

    heads = []
    for h in range(nh):
        blk = acc_ref[:, h * TILE:(h + 1) * TILE]
        o_t = (blk / blk[HALF:HALF + 1, :]).T
        heads.append(o_t if h % 2 == 0 else pltpu.roll(o_t, HALF, 1))
    pairs = [jnp.where(lo, heads[2 * mp], heads[2 * mp + 1]) for mp in range(nh // 2)]
    o_ref[...] = (jnp.concatenate(pairs, axis=1) * _silu(gate_ref[...].astype(F32))).astype(o_ref.dtype)


def _dsa(p3, sm3, q_norm_g, k_norm_g):
    b, l, _ = p3.shape
    top_k = min(DSA_TOPK_MAX, l // 4)
    nb = l // TILE
    nc = l // KEY_CHUNK
    sl = np.tril(np.ones((KEY_CHUNK, KEY_CHUNK), np.float32), -1)
    dup = lambda g: jnp.concatenate([g, g]).reshape(1, LANE).astype(F32)
    consts = [dup(q_norm_g), dup(k_norm_g), jnp.asarray(sl, BF16)]
    qblk = lambda name: pl.BlockSpec((None, TILE, BRANCH_W), lambda i, j: (i, j, P_OFF[name] // BRANCH_W))
    seq = lambda name: pl.BlockSpec((None, l, LANE), lambda i, j: (i, 0, P_OFF[name] // LANE),
                                    pipeline_mode=pl.Buffered(1))
    chunks = lambda dt: pltpu.VMEM((nc, KEY_CHUNK, LANE), dt)
    return pl.pallas_call(
        functools.partial(_dsa_kernel, top_k=top_k),
        grid=(b, nb),
        in_specs=[qblk("c_q"), qblk("c_iq"),
                  pl.BlockSpec((None, TILE, LANE), lambda i, j: (i, j, 0)),
                  qblk("c_gate"), seq("c_kk"), seq("c_vv"), seq("c_ikik")]
        + [_full_spec(a.shape) for a in consts],
        out_specs=pl.BlockSpec((None, TILE, BRANCH_W), lambda i, j: (i, j, 0)),
        out_shape=jax.ShapeDtypeStruct((b, l, BRANCH_W), BF16),
        scratch_shapes=[chunks(BF16), pltpu.VMEM((nc, LANE, KEY_CHUNK), BF16), chunks(BF16), chunks(I32),
                        pltpu.VMEM((nc, 32, 8, LANE), I32),
                        pltpu.VMEM((nc, KEY_CHUNK, DSA_HEADS * TILE), F32), pltpu.VMEM((8, LANE), I32),
                        pltpu.VMEM((8, LANE), F32), pltpu.VMEM((8, LANE), F32),
                        pltpu.VMEM((LANE, DSA_HEADS * TILE), F32)],
        compiler_params=pltpu.CompilerParams(dimension_semantics=("arbitrary", "arbitrary"),
                                             vmem_limit_bytes=VMEM_LIMIT),
        name="dsa",
    )(p3, p3, sm3, p3, p3, p3, p3, *consts)


def _ssd_kernel(z_ref, xbc_ref, small_ref, cw_ref, cbias_ref, dtb_ref, apad_ref, dexp_ref, ng_ref, tril_ref,
                e64_ref, e128_ref, o_ref, st_ref, halo_ref, buf_ref):
    _tiles_in_order(_ssd_tile, (z_ref, xbc_ref, small_ref),
                    (cw_ref, cbias_ref, dtb_ref, apad_ref, dexp_ref, ng_ref, tril_ref, e64_ref, e128_ref),
                    o_ref, (st_ref, halo_ref, buf_ref))


def _ssd_tile(z_ref, xbc_ref, small_ref, cw_ref, cbias_ref, dtb_ref, apad_ref, dexp_ref, ng_ref, tril_ref,
              e64_ref, e128_ref, o_ref, st_ref, halo_ref, buf_ref):
    nhp = SSD_HEADS * SSD_HD
    gs = SSD_STATE
    act = _silu(_causal_conv(xbc_ref[...].astype(F32), cw_ref, halo_ref, buf_ref) + cbias_ref[...])
    xs = act[:, :nhp]
    bm = act[:, nhp:nhp + SSD_GROUPS * gs]
    cm = act[:, nhp + SSD_GROUPS * gs:]
    dt_full = _softplus(small_ref[...] + dtb_ref[...])
    cs_full = _dot01_l(tril_ref[...], dt_full * apad_ref[...])
    dt_exp = _dot01_r(dt_full, e64_ref[...])
    cs_exp = _dot01_r(cs_full, e64_ref[...])
    cs_lb = _dot01_r(cs_full, e128_ref[...])
    xdt = xs * dt_exp
    cs_last = cs_exp[TILE - 1:TILE, :]
    xds = (xdt * jnp.exp(cs_last - cs_exp)).astype(BF16)
    ecs = jnp.exp(cs_exp)
    xdt_b = xdt.astype(BF16)

    r = lax.broadcasted_iota(I32, (TILE, TILE), 0)
    c = lax.broadcasted_iota(I32, (TILE, TILE), 1)
    incl = r >= c
    lo = c < HALF
    hpg = SSD_HEADS // SSD_GROUPS
    gw = hpg * SSD_HD
    y_parts = []
    for g in range(SSD_GROUPS):
        bmg = bm[:, g * gs:(g + 1) * gs]
        cmg = cm[:, g * gs:(g + 1) * gs].astype(BF16)
        cbg = _dot_nt(cmg, bmg.astype(BF16))
        for mp in range(hpg // 2):
            pair = (g * hpg) // 2 + mp
            xpair = xdt_b[:, pair * LANE:(pair + 1) * LANE]
            ys = []
            for e in range(2):
                h = 2 * pair + e
                cb = cs_lb[:, h * LANE:(h + 1) * LANE]
                lm = jnp.where(incl, jnp.exp(jnp.where(incl, cb - cb.T, 0.0)), 0.0)
                ys.append(_dot((cbg * lm).astype(BF16), xpair))
            y_parts.append(jnp.where(lo, ys[0], ys[1]))
        st = st_ref[g]
        y_off = _dot(cmg, st.astype(BF16)) * ecs[:, g * gw:(g + 1) * gw]
        y_parts[-(hpg // 2):] = [yp + y_off[:, k * LANE:(k + 1) * LANE]
                                 for k, yp in enumerate(y_parts[-(hpg // 2):])]
        new = _dot(bmg.T.astype(BF16), xds[:, g * gw:(g + 1) * gw])
        st_ref[g] = jnp.exp(cs_last[:, g * gw:(g + 1) * gw]) * st + new
    y = jnp.concatenate(y_parts, axis=1) + xs * dexp_ref[...]
    yz = y * _silu(z_ref[...].astype(F32))
    o_ref[...] = _rms(yz, ng_ref[...]).astype(o_ref.dtype)


def _ssd(p3, sm3, conv_w, conv_b, a_log, dt_bias, d_skip, norm_g):
    b, l, _ = p3.shape
    step = min(SEQ_STEP, l)
    consts = [conv_w.astype(F32), conv_b.reshape(1, -1).astype(F32), _pad_lanes(dt_bias, SM_DT),
              _pad_lanes(-jnp.exp(a_log.astype(F32)), SM_DT),
              jnp.repeat(d_skip.astype(F32), SSD_HD).reshape(1, -1), norm_g.reshape(1, -1).astype(F32),
              jnp.asarray(np.tril(np.ones((TILE, TILE), np.float32)), BF16),
              jnp.asarray(_np_lane_expand(SM_DT, SSD_HEADS, SSD_HD), BF16),
              jnp.asarray(_np_lane_expand(SM_DT, SSD_HEADS, LANE), BF16)]
    return pl.pallas_call(
        _ssd_kernel,
        grid=(b, l // step),
        in_specs=[pl.BlockSpec((None, step, BRANCH_W), lambda i, j: (i, j, P_OFF["d_z"] // BRANCH_W)),
                  pl.BlockSpec((None, step, SSD_XBC), lambda i, j: (i, j, P_OFF["d_xbc"] // SSD_XBC)),
                  pl.BlockSpec((None, step, LANE), lambda i, j: (i, j, 0))]
        + [_full_spec(a.shape) for a in consts],
        out_specs=pl.BlockSpec((None, step, BRANCH_W), lambda i, j: (i, j, 0)),
        out_shape=jax.ShapeDtypeStruct((b, l, BRANCH_W), BF16),
        scratch_shapes=[pltpu.VMEM((SSD_GROUPS, SSD_STATE, (SSD_HEADS // SSD_GROUPS) * SSD_HD), F32),
                        pltpu.VMEM((8, SSD_XBC), F32), pltpu.VMEM((TILE + 8, SSD_XBC), F32)],
        compiler_params=pltpu.CompilerParams(dimension_semantics=("arbitrary", "arbitrary"),
                                             vmem_limit_bytes=VMEM_LIMIT),
        name="ssd",
    )(p3, p3, sm3, *consts)


def _mem_kv_kernel(mem_ref, g_ref, w_ref, kg_ref, k_ref, v_ref):
    kv = _dot(_rms(mem_ref[...], g_ref[...]).astype(BF16), w_ref[...])
    for h in range(MEM_HEADS):
        hs = slice(h * MEM_HD, (h + 1) * MEM_HD)
        k_ref[:, hs] = _rms(kv[:, hs], kg_ref[...]).astype(k_ref.dtype)
    v_ref[...] = kv[:, BRANCH_W:].astype(v_ref.dtype)


def _mem_kv(mem, mem_norm_g, w_kv_bf16, k_norm_g):
    b, m, _ = mem.shape
    consts = [mem_norm_g.reshape(1, -1).astype(F32), w_kv_bf16, k_norm_g.reshape(1, -1).astype(F32)]
    out = jax.ShapeDtypeStruct((b, m, BRANCH_W), BF16)
    return pl.pallas_call(
        _mem_kv_kernel,
        grid=(b,),
        in_specs=[pl.BlockSpec((None, m, D_MODEL), lambda i: (i, 0, 0))] + [_full_spec(a.shape) for a in consts],
        out_specs=[pl.BlockSpec((None, m, BRANCH_W), lambda i: (i, 0, 0))] * 2,
        out_shape=[out, out],
        compiler_params=pltpu.CompilerParams(dimension_semantics=("arbitrary",), vmem_limit_bytes=VMEM_LIMIT),
        name="mem_kv",
    )(mem, *consts)


def _mem_attn_kernel(q_ref, gate_ref, k_ref, v_ref, qg_ref, o_ref):
    q = q_ref[...].astype(F32)
    outs = []
    for h in range(MEM_HEADS):
        hs = slice(h * MEM_HD, (h + 1) * MEM_HD)
        qn = _rms(q[:, hs], qg_ref[...]).astype(BF16)
        lg = _dot_nt(qn, k_ref[:, hs]) * (MEM_HD ** -0.5)
        p = jnp.exp(lg - jnp.max(lg, axis=1, keepdims=True))
        outs.append(_dot(p.astype(BF16), v_ref[:, hs]) / jnp.sum(p, axis=1, keepdims=True))
    o_ref[...] = (jnp.concatenate(outs, axis=1) * _silu(gate_ref[...].astype(F32))).astype(o_ref.dtype)


def _mem_attn(p3, k, v, q_norm_g):
    b, l, _ = p3.shape
    m = k.shape[1]
    tm = min(512, l)
    qg = q_norm_g.reshape(1, -1).astype(F32)
    blk = lambda name: pl.BlockSpec((None, tm, BRANCH_W), lambda i, j: (i, j, P_OFF[name] // BRANCH_W))
    kv = pl.BlockSpec((None, m, BRANCH_W), lambda i, j: (i, 0, 0))
    return pl.pallas_call(
        _mem_attn_kernel,
        grid=(b, l // tm),
        in_specs=[blk("m_q"), blk("m_gate"), kv, kv, _full_spec(qg.shape)],
        out_specs=pl.BlockSpec((None, tm, BRANCH_W), lambda i, j: (i, j, 0)),
        out_shape=jax.ShapeDtypeStruct((b, l, BRANCH_W), BF16),
        compiler_params=pltpu.CompilerParams(dimension_semantics=("arbitrary", "arbitrary"),
                                             vmem_limit_bytes=VMEM_LIMIT),
        name="mem_attn",
    )(p3, p3, k, v, qg)


def _reordered_w_in(w):
    parts, start = [], 0
    cols = _P_COLUMNS
    for end in range(1, len(cols) + 1):
        if end == len(cols) or (cols[end] < 0) != (cols[start] < 0) or (cols[start] >= 0 and cols[end] != cols[end - 1] + 1):
            if cols[start] < 0:
                parts.append(jnp.zeros((w.shape[0], end - start), BF16))
            else:
                parts.append(w[:, int(cols[start]):int(cols[end - 1]) + 1].astype(BF16))
            start = end
    return jnp.concatenate(parts, axis=1)


def kernel(x, mem, norm_g, w_in, gdn_conv_w, gdn_a_log, gdn_dt_bias, gdn_norm_g, sg_ln_g, sg_ln_b, sg_w, sg_b, dsa_q_norm_g, dsa_k_norm_g, ssd_conv_w, ssd_conv_b, ssd_a_log, ssd_dt_bias, ssd_d, ssd_norm_g, mem_norm_g, w_mem_kv, mem_q_norm_g, mem_k_norm_g, w_gate, w_branch, w_out):
    b, l, d = x.shape
    depth = norm_g.shape[0]
    x2d = x.reshape(b * l, d)
    for i in range(depth):
        g = norm_g[i].reshape(1, d).astype(F32)
        p2d, sm2d = _in_proj(x2d, g, _reordered_w_in(w_in[i]))
        p3 = p2d.reshape(b, l, P_MAIN)
        sm3 = sm2d.reshape(b, l, LANE)
        mk, mv = _mem_kv(mem, mem_norm_g[i], w_mem_kv[i].astype(BF16), mem_k_norm_g[i])
        ys = (
            _gdn(p3, sm3, gdn_conv_w[i], gdn_a_log[i], gdn_dt_bias[i], gdn_norm_g[i]),
            _sg(p3, sg_ln_g[i], sg_ln_b[i], sg_w[i], sg_b[i]),
            _dsa(p3, sm3, dsa_q_norm_g[i], dsa_k_norm_g[i]),
            _ssd(p3, sm3, ssd_conv_w[i], ssd_conv_b[i], ssd_a_log[i], ssd_dt_bias[i], ssd_d[i], ssd_norm_g[i]),
            _mem_attn(p3, mk, mv, mem_q_norm_g[i]),
        )
        ys = [y.reshape(b * l, BRANCH_W) for y in ys]
        x2d = _merge(x2d, g, ys, w_gate[i].astype(BF16), w_branch[i].astype(BF16), w_out[i].astype(BF16))
    return x2d.reshape(b, l, d)
```
